```python
import math
import jax, jax.numpy as jnp
from jax import lax
import numpy as np

D_MODEL = 2048
BATCH = 1
SEQ = 8192
DEPTH = 2
DEC_BATCH = 32
DEC_SEQ = 4
PAST_LEN = 8192
PAGE_SIZE = 128

D_RNN = D_MODEL
N_RNN_BLOCKS = 8
CONV_W = 4
LRU_C = 8.0
N_HEADS = 16
HEAD_DIM = D_MODEL // N_HEADS
N_KV = 4
CMP_BLOCK = 32
CMP_STRIDE = 16
CMP_HIDDEN = 256
SEL_BLOCK = 64
N_SELECT = 16
WINDOW = 512
Q_BLOCK = 128
N_BUCKETS = 32
MAX_DISTANCE = 4096
PLE_DIM = 256
EPS = 1e-6
NEG = -1e30
FORCE = 1e9

kernel_name = "yoco_rglru_nsa_decode_step"


def rms_norm(x, g):
    xf = x.astype(jnp.float32)
    y = xf * lax.rsqrt(jnp.mean(xf * xf, axis=-1, keepdims=True) + EPS)
    return (y * g.astype(jnp.float32)).astype(x.dtype)


def masked_softmax(s, m):
    p = jax.nn.softmax(jnp.where(m, s, NEG), axis=-1)
    return p * m


def rel_bucket(dist):
    n = jnp.maximum(dist, 0)
    max_exact = N_BUCKETS // 2
    nf = jnp.maximum(n, 1).astype(jnp.float32)
    large = max_exact + (jnp.log(nf / max_exact) / math.log(MAX_DISTANCE / max_exact)
                         * (N_BUCKETS - max_exact)).astype(jnp.int32)
    large = jnp.minimum(large, N_BUCKETS - 1)
    return jnp.where(n < max_exact, n, large)


def rglru_block(xn, conv_buf, h0, w_in, conv_w, conv_b, w_r, b_r, w_i, b_i, lam, w_out):
    B, T, _ = xn.shape
    u, z = jnp.split(xn @ w_in, 2, axis=-1)
    up = jnp.concatenate([conv_buf.astype(u.dtype), u], axis=1)
    xc = conv_b + sum(up[:, k:k + T] * conv_w[k] for k in range(CONV_W))
    new_buf = up[:, T:]
    xb = xc.reshape(B, T, N_RNN_BLOCKS, D_RNN // N_RNN_BLOCKS)
    r = jax.nn.sigmoid(jnp.einsum('btnc,ncd->btnd', xb, w_r).reshape(B, T, D_RNN) + b_r)
    i = jax.nn.sigmoid(jnp.einsum('btnc,ncd->btnd', xb, w_i).reshape(B, T, D_RNN) + b_i)
    log_a = -LRU_C * r.astype(jnp.float32) * jax.nn.softplus(-lam.astype(jnp.float32))
    a = jnp.exp(log_a)
    b = jnp.sqrt(-jnp.expm1(2.0 * log_a)) * (i * xc).astype(jnp.float32)

    def step(h, ab):
        h = ab[0] * h + ab[1]
        return h, h

    h_last, hs = lax.scan(step, h0.astype(jnp.float32), (jnp.swapaxes(a, 0, 1), jnp.swapaxes(b, 0, 1)))
    hs = jnp.swapaxes(hs, 0, 1).astype(xn.dtype)
    return (hs * jax.nn.silu(z)) @ w_out, new_buf, h_last


def shared_kv_rows(h, kv_norm_g, w_kv, g_k_sel, g_k_win):
    B, T, _ = h.shape
    kv = (rms_norm(h, kv_norm_g) @ w_kv).reshape(B, T, 6, N_KV, HEAD_DIM)
    return (kv[:, :, 0], kv[:, :, 1], rms_norm(kv[:, :, 2], g_k_sel), kv[:, :, 3],
            rms_norm(kv[:, :, 4], g_k_win), kv[:, :, 5])


def compress_blocks(x, pos, w1a, w1b, b1, w2, b2):
    B, L, G, Dh = x.shape
    n_cmp = (L - CMP_BLOCK) // CMP_STRIDE + 1
    n_piece = n_cmp + 1
    pieces = x[:, :n_piece * CMP_STRIDE].reshape(B, n_piece, CMP_STRIDE, G, Dh)
    pe = pos.reshape(2, CMP_STRIDE, 1, Dh)
    first = (pieces[:, :-1] + pe[0]).transpose(0, 1, 3, 2, 4).reshape(B, n_cmp, G, CMP_STRIDE * Dh)
    second = (pieces[:, 1:] + pe[1]).transpose(0, 1, 3, 2, 4).reshape(B, n_cmp, G, CMP_STRIDE * Dh)
    hid = jax.nn.silu(first @ w1a + second @ w1b + b1)
    return hid @ w2 + b2


def compress_kv(kc, vc, cmp_pos, cmp_w1a, cmp_w1b, cmp_b1, cmp_w2, cmp_b2, g_k_cmp):
    kcmp = compress_blocks(kc, cmp_pos[0], cmp_w1a[0], cmp_w1b[0], cmp_b1[0], cmp_w2[0], cmp_b2[0])
    vcmp = compress_blocks(vc, cmp_pos[1], cmp_w1a[1], cmp_w1b[1], cmp_b1[1], cmp_w2[1], cmp_b2[1])
    return rms_norm(kcmp, g_k_cmp), vcmp


def nsa_branches(q, qpos0, kcmp, vcmp, ks, vs, kw, vw, rel_bias):
    B, Tq = q.shape[:2]
    L = ks.shape[1]
    hg = N_HEADS // N_KV
    qch = Q_BLOCK if Tq % Q_BLOCK == 0 else Tq
    n_ch = Tq // qch
    n_cmp = kcmp.shape[1]
    n_sb = -(-L // SEL_BLOCK)
    k_top = min(N_SELECT, n_sb)
    scale = HEAD_DIM ** -0.5
    pad = n_sb * SEL_BLOCK - L
    ksb = jnp.pad(ks, ((0, 0), (0, pad), (0, 0), (0, 0))).reshape(B, n_sb, SEL_BLOCK, N_KV, HEAD_DIM).transpose(0, 3, 1, 2, 4)
    vsb = jnp.pad(vs, ((0, 0), (0, pad), (0, 0), (0, 0))).reshape(B, n_sb, SEL_BLOCK, N_KV, HEAD_DIM).transpose(0, 3, 1, 2, 4)
    cmp_end = jnp.arange(n_cmp) * CMP_STRIDE + CMP_BLOCK - 1
    ci = jnp.arange(n_cmp)[:, None] * CMP_STRIDE
    sj = jnp.arange(n_sb)[None, :] * SEL_BLOCK
    overlap = ((ci < sj + SEL_BLOCK) & (ci + CMP_BLOCK > sj)).astype(jnp.float32)
    tbl = rel_bias.astype(jnp.float32).reshape(N_BUCKETS, N_KV, hg)
    gidx = jnp.arange(N_KV)[None, :, None, None, None]
    blocks = jnp.arange(n_sb)
    qg = q.reshape(B, n_ch, qch, N_KV, hg, HEAD_DIM).transpose(1, 0, 2, 3, 4, 5)
    gather = jax.vmap(jax.vmap(lambda kb, ix: kb[ix]))

    def one_block(args):
        qb, c = args
        t = qpos0 + c * qch + jnp.arange(qch)
        s = jnp.einsum('bqghd,bigd->bghqi', qb, kcmp).astype(jnp.float32) * scale
        s = s + tbl[rel_bucket(t[:, None] - cmp_end[None, :])].transpose(2, 3, 0, 1)
        p_cmp = masked_softmax(s, cmp_end[None, :] <= t[:, None])
        o_cmp = jnp.einsum('bghqi,bigd->bqghd', p_cmp.astype(vcmp.dtype), vcmp)
        imp = jnp.einsum('bghqi,ij->bgqj', p_cmp, overlap)
        cur = t // SEL_BLOCK
        blk_ok = blocks[None, :] <= cur[:, None]
        forced = (blocks[None, :] == 0) | (blocks[None, :] == cur[:, None]) | (blocks[None, :] == cur[:, None] - 1)
        score = jnp.where(blk_ok, jnp.where(forced, FORCE, imp), -FORCE)
        _, idx = lax.top_k(score, k_top)
        sel_ok = jnp.take_along_axis(jnp.broadcast_to(blk_ok, score.shape), idx, axis=-1)
        kg = gather(ksb, idx)
        vg = gather(vsb, idx)
        kpos = idx[..., None] * SEL_BLOCK + jnp.arange(SEL_BLOCK)
        s = jnp.einsum('bqghd,bgqnsd->bghqns', qb, kg).astype(jnp.float32) * scale
        s = s + tbl[rel_bucket(t[:, None, None] - kpos), gidx].transpose(0, 1, 5, 2, 3, 4)
        m = (kpos <= t[:, None, None]) & sel_ok[..., None]
        s = s.reshape(B, N_KV, hg, qch, k_top * SEL_BLOCK)
        m = m.reshape(B, N_KV, 1, qch, k_top * SEL_BLOCK)
        p_sel = masked_softmax(s, m)
        o_sel = jnp.einsum('bghqm,bgqmd->bqghd', p_sel.astype(vg.dtype),
                           vg.reshape(B, N_KV, qch, k_top * SEL_BLOCK, HEAD_DIM))
        kwb = lax.dynamic_slice_in_dim(kw, c * qch, WINDOW + qch, axis=1)
        vwb = lax.dynamic_slice_in_dim(vw, c * qch, WINDOW + qch, axis=1)
        wpos = qpos0 - WINDOW + c * qch + jnp.arange(WINDOW + qch)
        d = t[:, None] - wpos[None, :]
        mw = (d >= 0) & (d < WINDOW) & (wpos[None, :] >= 0)
        s = jnp.einsum('bqghd,bkgd->bghqk', qb, kwb).astype(jnp.float32) * scale
        s = s + tbl[rel_bucket(d)].transpose(2, 3, 0, 1)
        p_win = masked_softmax(s, mw)
        o_win = jnp.einsum('bghqk,bkgd->bqghd', p_win.astype(vwb.dtype), vwb)
        return o_cmp, o_sel, o_win

    outs = lax.map(one_block, (qg, jnp.arange(n_ch)))
    return [o.transpose(1, 0, 2, 3, 4, 5).reshape(B, Tq, N_HEADS, HEAD_DIM) for o in outs]


def nsa_layer(xn, qpos0, attn_kv, w_in, g_q, w_out, rel_bias):
    B, T, _ = xn.shape
    hd = N_HEADS * HEAD_DIM
    proj = xn @ w_in
    q = rms_norm(proj[..., :hd].reshape(B, T, N_HEADS, HEAD_DIM), g_q)
    z = proj[..., hd:4 * hd].reshape(B, T, 3, N_HEADS, HEAD_DIM)
    gl = proj[..., 4 * hd:].reshape(B, T, 3, N_HEADS)
    o_cmp, o_sel, o_win = nsa_branches(q, qpos0, *attn_kv, rel_bias)
    o = jnp.stack([o_cmp, o_sel, o_win], axis=2)
    o = jnp.sum(jax.nn.sigmoid(gl)[..., None] * o * jax.nn.silu(z), axis=2).reshape(B, T, hd)
    return o @ w_out


def ple_add(h, p_i, w_proj, g_norm, w_gate):
    return h + (p_i @ w_proj) * jax.nn.sigmoid(rms_norm(h, g_norm) @ w_gate)


def run_trunk(x, p, qpos0, conv_state, h_state, make_kv, norm_g, a_w_in, a_conv_w, a_conv_b,
              a_w_r, a_b_r, a_w_i, a_b_i, a_lam, a_w_out, b_w_in, b_g_q, b_w_out, rel_bias,
              ple_w_proj, ple_norm_g, ple_w_gate):
    n_a = DEPTH // 2
    h = x
    conv_out, h_out = [], []
    attn_kv, new_kv = None, None
    for i in range(DEPTH):
        if i < n_a:
            y, cb, hl = rglru_block(rms_norm(h, norm_g[i]), conv_state[i], h_state[i], a_w_in[i],
                                    a_conv_w[i], a_conv_b[i], a_w_r[i], a_b_r[i], a_w_i[i], a_b_i[i],
                                    a_lam[i], a_w_out[i])
            conv_out.append(cb)
            h_out.append(hl)
        else:
            if i == n_a:
                attn_kv, new_kv = make_kv(h)
            j = i - n_a
            y = nsa_layer(rms_norm(h, norm_g[i]), qpos0, attn_kv, b_w_in[j], b_g_q[j], b_w_out[j], rel_bias)
        h = h + y
        h = ple_add(h, p[i], ple_w_proj[i], ple_norm_g[i], ple_w_gate[i])
    return h, jnp.stack(conv_out), jnp.stack(h_out), new_kv


def setup_inputs(seed: int = 0) -> dict:
    key = jax.random.key(seed)
    keys = list(jax.random.split(key, 64))
    f32 = jnp.float32
    n_a = DEPTH // 2
    n_b = DEPTH - n_a
    n_pages = PAST_LEN // PAGE_SIZE
    n_pool = (5 * DEC_BATCH * n_pages + 3) // 4
    wbuf = min(WINDOW, PAST_LEN)
    hd = N_HEADS * HEAD_DIM
    bw = D_RNN // N_RNN_BLOCKS

    def nk():
        return keys.pop()

    def nrm(shape, scale):
        return jax.random.normal(nk(), shape, f32) * scale

    def gain(shape):
        return 1.0 + nrm(shape, 0.02)

    u = jax.random.uniform(nk(), (n_a, D_RNN), f32, 0.9, 0.999)
    s_ = u ** (1.0 / LRU_C)
    a_lam = jnp.log(s_) - jnp.log1p(-s_)
    page_table = jax.random.permutation(nk(), n_pool)[:DEC_BATCH * n_pages].reshape(DEC_BATCH, n_pages).astype(jnp.int32)
    cache_shape = (n_pool, PAGE_SIZE, N_KV, HEAD_DIM)
    return {
        "x_prompt": nrm((BATCH, SEQ, D_MODEL), 1.0),
        "x_sample": nrm((DEC_BATCH, DEC_SEQ, D_MODEL), 1.0),
        "p_prompt": nrm((DEPTH, BATCH, SEQ, PLE_DIM), 1.0),
        "p_sample": nrm((DEPTH, DEC_BATCH, DEC_SEQ, PLE_DIM), 1.0),
        "state_conv": nrm((n_a, DEC_BATCH, CONV_W - 1, D_RNN), 1.0),
        "state_h": nrm((n_a, DEC_BATCH, D_RNN), 0.5),
        "cache_k_cmp": nrm(cache_shape, 1.0),
        "cache_v_cmp": nrm(cache_shape, 1.0),
        "cache_k_sel": nrm(cache_shape, 1.0),
        "cache_v_sel": nrm(cache_shape, 1.0),
        "cache_k_win": nrm((DEC_BATCH, wbuf, N_KV, HEAD_DIM), 1.0),
        "cache_v_win": nrm((DEC_BATCH, wbuf, N_KV, HEAD_DIM), 1.0),
        "page_table": page_table,
        "norm_g": gain((DEPTH, D_MODEL)),
        "a_w_in": nrm((n_a, D_MODEL, 2 * D_RNN), D_MODEL ** -0.5),
        "a_conv_w": nrm((n_a, CONV_W, D_RNN), CONV_W ** -0.5),
        "a_conv_b": nrm((n_a, D_RNN), 0.01),
        "a_w_r": nrm((n_a, N_RNN_BLOCKS, bw, bw), bw ** -0.5),
        "a_b_r": nrm((n_a, D_RNN), 0.01),
        "a_w_i": nrm((n_a, N_RNN_BLOCKS, bw, bw), bw ** -0.5),
        "a_b_i": nrm((n_a, D_RNN), 0.01),
        "a_lam": a_lam,
        "a_w_out": nrm((n_a, D_RNN, D_MODEL), D_RNN ** -0.5),
        "kv_norm_g": gain((D_MODEL,)),
        "w_kv": nrm((D_MODEL, 6 * N_KV * HEAD_DIM), D_MODEL ** -0.5),
        "g_k_cmp": gain((HEAD_DIM,)),
        "g_k_sel": gain((HEAD_DIM,)),
        "g_k_win": gain((HEAD_DIM,)),
        "cmp_pos": nrm((2, CMP_BLOCK, HEAD_DIM), 0.1),
        "cmp_w1a": nrm((2, CMP_STRIDE * HEAD_DIM, CMP_HIDDEN), (CMP_BLOCK * HEAD_DIM) ** -0.5),
        "cmp_w1b": nrm((2, CMP_STRIDE * HEAD_DIM, CMP_HIDDEN), (CMP_BLOCK * HEAD_DIM) ** -0.5),
        "cmp_b1": nrm((2, CMP_HIDDEN), 0.01),
        "cmp_w2": nrm((2, CMP_HIDDEN, HEAD_DIM), CMP_HIDDEN ** -0.5),
        "cmp_b2": nrm((2, HEAD_DIM), 0.01),
        "b_w_in": nrm((n_b, D_MODEL, 4 * hd + 3 * N_HEADS), D_MODEL ** -0.5),
        "b_g_q": gain((n_b, HEAD_DIM)),
        "b_w_out": nrm((n_b, hd, D_MODEL), hd ** -0.5),
        "rel_bias": nrm((N_BUCKETS, N_HEADS), 0.5),
        "ple_w_proj": nrm((DEPTH, PLE_DIM, D_MODEL), 0.5 * PLE_DIM ** -0.5),
        "ple_norm_g": gain((DEPTH, D_MODEL)),
        "ple_w_gate": nrm((DEPTH, D_MODEL, D_MODEL), D_MODEL ** -0.5),
    }


def reference(x_prompt, x_sample, p_prompt, p_sample, state_conv, state_h, cache_k_cmp, cache_v_cmp,
              cache_k_sel, cache_v_sel, cache_k_win, cache_v_win, page_table, norm_g, a_w_in, a_conv_w,
              a_conv_b, a_w_r, a_b_r, a_w_i, a_b_i, a_lam, a_w_out, kv_norm_g, w_kv, g_k_cmp, g_k_sel,
              g_k_win, cmp_pos, cmp_w1a, cmp_w1b, cmp_b1, cmp_w2, cmp_b2, b_w_in, b_g_q, b_w_out,
              rel_bias, ple_w_proj, ple_norm_g, ple_w_gate):
    n_a = DEPTH // 2
    n_pages = page_table.shape[1]
    page = cache_k_sel.shape[1]
    past_len = n_pages * page
    db = x_sample.shape[0]
    wpad = ((0, 0), (WINDOW, 0), (0, 0), (0, 0))

    def kv_prompt(h):
        kc, vc, ks, vs, kw, vw = shared_kv_rows(h, kv_norm_g, w_kv, g_k_sel, g_k_win)
        kcmp, vcmp = compress_kv(kc, vc, cmp_pos, cmp_w1a, cmp_w1b, cmp_b1, cmp_w2, cmp_b2, g_k_cmp)
        wk = min(WINDOW, kw.shape[1])
        attn = (kcmp, vcmp, ks, vs, jnp.pad(kw, wpad), jnp.pad(vw, wpad))
        return attn, (kc, vc, ks, vs, kw[:, -wk:], vw[:, -wk:])

    def kv_sample(h):
        kc, vc, ks, vs, kw, vw = shared_kv_rows(h, kv_norm_g, w_kv, g_k_sel, g_k_win)

        def past(cache):
            return cache[page_table].reshape(db, past_len, N_KV, HEAD_DIM)

        kc_all = jnp.concatenate([past(cache_k_cmp), kc], axis=1)
        vc_all = jnp.concatenate([past(cache_v_cmp), vc], axis=1)
        ks_all = jnp.concatenate([past(cache_k_sel), ks], axis=1)
        vs_all = jnp.concatenate([past(cache_v_sel), vs], axis=1)
        kcmp, vcmp = compress_kv(kc_all, vc_all, cmp_pos, cmp_w1a, cmp_w1b, cmp_b1, cmp_w2, cmp_b2, g_k_cmp)
        wbuf = cache_k_win.shape[1]
        kw_all = jnp.concatenate([cache_k_win, kw], axis=1)
        vw_all = jnp.concatenate([cache_v_win, vw], axis=1)
        fp = ((0, 0), (WINDOW - wbuf, 0), (0, 0), (0, 0))
        wk = min(WINDOW, kw_all.shape[1])
        attn = (kcmp, vcmp, ks_all, vs_all, jnp.pad(kw_all, fp), jnp.pad(vw_all, fp))
        return attn, (kc, vc, ks, vs, kw_all[:, -wk:], vw_all[:, -wk:])

    weights = (norm_g, a_w_in, a_conv_w, a_conv_b, a_w_r, a_b_r, a_w_i, a_b_i, a_lam, a_w_out,
               b_w_in, b_g_q, b_w_out, rel_bias, ple_w_proj, ple_norm_g, ple_w_gate)
    bp = x_prompt.shape[0]
    conv0 = jnp.zeros((n_a, bp, CONV_W - 1, D_RNN), x_prompt.dtype)
    h0 = jnp.zeros((n_a, bp, D_RNN), jnp.float32)
    y_prompt, conv_p, h_p, kv_p = run_trunk(x_prompt, p_prompt, 0, conv0, h0, kv_prompt, *weights)
    y_sample, conv_s, h_s, kv_s = run_trunk(x_sample, p_sample, past_len, state_conv, state_h, kv_sample, *weights)
    k_cmp_p, v_cmp_p, k_sel_p, v_sel_p, k_win_p, v_win_p = kv_p
    k_cmp_s, v_cmp_s, k_sel_s, v_sel_s, k_win_s, v_win_s = kv_s
    return (y_prompt, y_sample, conv_p, h_p, k_cmp_p, v_cmp_p, k_sel_p, v_sel_p, k_win_p, v_win_p,
            conv_s, h_s, k_cmp_s, v_cmp_s, k_sel_s, v_sel_s, k_win_s, v_win_s)
```

```python
import functools
import math

import jax
import jax.numpy as jnp
from jax import lax
from jax.experimental import pallas as pl
from jax.experimental.pallas import tpu as pltpu

F32 = jnp.float32
BF16 = jnp.bfloat16

N_RNN_BLOCKS = 8
CONV_W = 4
LRU_C = 8.0
N_HEADS = 16
HEAD_DIM = 128
N_KV = 4
HEADS_PER_GROUP = N_HEADS // N_KV
GROUP_W = HEADS_PER_GROUP * HEAD_DIM
KV_W = N_KV * HEAD_DIM
CMP_BLOCK = 32
CMP_STRIDE = 16
SEL_BLOCK = 64
N_SELECT = 16
WINDOW = 512
N_BUCKETS = 32
MAX_DISTANCE = 4096
EPS = 1e-6
NEG = -1e30
FORCE = 1e9
REMOVED = -3e38

LANES = 128
SUBLANES = 8
VMEM_LIMIT = 56 * 1024 * 1024

TQ = 128
TK_SEL = 512
CMP_PAGES_PER_STEP = 16


def _cparams(sem):
    return pltpu.CompilerParams(dimension_semantics=sem, vmem_limit_bytes=VMEM_LIMIT)


def _resident(shape, index_map):
    return pl.BlockSpec(shape, index_map, pipeline_mode=pl.Buffered(1))


def _rms_rows(x, g):
    return x * lax.rsqrt(jnp.mean(x * x, axis=-1, keepdims=True) + EPS) * g


def _sigmoid(x):
    return 1.0 / (1.0 + jnp.exp(-x))


def _norm_matmul_kernel(x_ref, g_ref, w_ref, o_ref, xn_ref):
    @pl.when(pl.program_id(1) == 0)
    def _():
        xn_ref[...] = _rms_rows(x_ref[...], g_ref[...]).astype(BF16)

    o_ref[...] = jnp.dot(xn_ref[...], w_ref[...], preferred_element_type=F32)


def _norm_matmul(x, g, w, tn):
    m, d = x.shape
    n = w.shape[1]
    tm = min(m, 512)
    return pl.pallas_call(
        _norm_matmul_kernel,
        grid=(m // tm, n // tn),
        in_specs=[pl.BlockSpec((tm, d), lambda i, j: (i, 0)),
                  pl.BlockSpec((1, d), lambda i, j: (0, 0)),
                  pl.BlockSpec((d, tn), lambda i, j: (0, j))],
        out_specs=pl.BlockSpec((tm, tn), lambda i, j: (i, j)),
        out_shape=jax.ShapeDtypeStruct((m, n), F32),
        scratch_shapes=[pltpu.VMEM((tm, d), BF16)],
        compiler_params=_cparams(("parallel", "arbitrary")),
        name="norm_matmul",
    )(x, g.reshape(1, d), w)


def _lru_gates(xc, wri_ref, br_ref, bi_ref, sp_ref, n, bw):
    sl = slice(n * bw, (n + 1) * bw)
    ri = jnp.dot(xc.astype(BF16), wri_ref[n], preferred_element_type=F32)
    r = _sigmoid(ri[:, :bw] + br_ref[:, sl])
    i = _sigmoid(ri[:, bw:] + bi_ref[:, sl])
    log_a = (-LRU_C) * r * sp_ref[:, sl]
    a = jnp.exp(log_a)
    b = jnp.sqrt(1.0 - jnp.exp(2.0 * log_a)) * (i * xc)
    return a, b


def _softplus(x):
    return jnp.maximum(x, 0.0) + jnp.log(1.0 + jnp.exp(-jnp.abs(x)))


def _rglru_prompt_kernel(u_ref, z_ref, cinit_ref, hinit_ref, cw_ref, cb_ref, wri_ref, br_ref, bi_ref,
                         lam_ref, g_ref, hlast_ref, ubuf, a_scr, b_scr, hs_scr, hcar, sp_scr):
    tm, d = u_ref.shape
    bw = d // N_RNN_BLOCKS

    @pl.when(pl.program_id(0) == 0)
    def _():
        ubuf[0:SUBLANES, :] = cinit_ref[...]
        hcar[...] = hinit_ref[...]
        sp_scr[...] = _softplus(-lam_ref[...])

    ubuf[SUBLANES:, :] = u_ref[...]
    for n in range(N_RNN_BLOCKS):
        sl = slice(n * bw, (n + 1) * bw)
        up = ubuf[:, sl]
        xc = cb_ref[:, sl] + up * cw_ref[CONV_W - 1:CONV_W, sl]
        for k in range(CONV_W - 1):
            xc = xc + pltpu.roll(up, CONV_W - 1 - k, axis=0) * cw_ref[k:k + 1, sl]
        xc = xc[SUBLANES:]
        a, b = _lru_gates(xc, wri_ref, br_ref, bi_ref, sp_scr, n, bw)
        a_scr[:, sl] = a
        b_scr[:, sl] = b
    ubuf[0:SUBLANES, :] = u_ref[tm - SUBLANES:tm, :]

    def step(t, h):
        h = a_scr[pl.ds(t, 1), :] * h + b_scr[pl.ds(t, 1), :]
        hs_scr[pl.ds(t, 1), :] = h
        return h

    h = lax.fori_loop(0, tm, step, hcar[...], unroll=8)
    hcar[...] = h
    hlast_ref[...] = h
    z = z_ref[...]
    g_ref[...] = (hs_scr[...] * (z * _sigmoid(z))).astype(BF16)


def _rglru_prompt(uz, conv_init, h_init, cw, cb, wri, br, bi, lam):
    t, d2 = uz.shape
    d = d2 // 2
    tm = min(t, 256)
    full = lambda shape: pl.BlockSpec(shape, lambda i: (0,) * len(shape))
    return pl.pallas_call(
        _rglru_prompt_kernel,
        grid=(t // tm,),
        in_specs=[pl.BlockSpec((tm, d), lambda i: (i, 0)),
                  pl.BlockSpec((tm, d), lambda i: (i, 1)),
                  full((SUBLANES, d)), full((1, d)), full((CONV_W, d)), full((1, d)),
                  full(wri.shape), full((1, d)), full((1, d)), full((1, d))],
        out_specs=[pl.BlockSpec((tm, d), lambda i: (i, 0)), full((1, d))],
        out_shape=[jax.ShapeDtypeStruct((t, d), BF16), jax.ShapeDtypeStruct((1, d), F32)],
        scratch_shapes=[pltpu.VMEM((tm + SUBLANES, d), F32), pltpu.VMEM((tm, d), F32),
                        pltpu.VMEM((tm, d), F32), pltpu.VMEM((tm, d), F32),
                        pltpu.VMEM((1, d), F32), pltpu.VMEM((1, d), F32)],
        compiler_params=_cparams(("arbitrary",)),
        name="rglru_prompt",
    )(uz, uz, conv_init, h_init, cw, cb.reshape(1, d), wri, br.reshape(1, d), bi.reshape(1, d),
      lam.reshape(1, d))


def _rglru_sample_kernel(up_ref, z_ref, h0_ref, cw_ref, cb_ref, wri_ref, br_ref, bi_ref, lam_ref,
                         g_ref, hlast_ref, sp_scr):
    n_t = z_ref.shape[0]
    d = z_ref.shape[2]
    bw = d // N_RNN_BLOCKS
    sp_scr[...] = _softplus(-lam_ref[...])
    for n in range(N_RNN_BLOCKS):
        sl = slice(n * bw, (n + 1) * bw)
        h = h0_ref[:, sl]
        for t in range(n_t):
            xc = cb_ref[:, sl]
            for k in range(CONV_W):
                xc = xc + up_ref[t + k, :, sl] * cw_ref[k:k + 1, sl]
            a, b = _lru_gates(xc, wri_ref, br_ref, bi_ref, sp_scr, n, bw)
            h = a * h + b
            z = z_ref[t, :, sl]
            g_ref[t, :, sl] = (h * (z * _sigmoid(z))).astype(BF16)
        hlast_ref[:, sl] = h


def _rglru_sample(up, z, h0, cw, cb, wri, br, bi, lam):
    n_t, nb, d = z.shape
    return pl.pallas_call(
        _rglru_sample_kernel,
        out_shape=[jax.ShapeDtypeStruct((n_t, nb, d), BF16), jax.ShapeDtypeStruct((nb, d), F32)],
        scratch_shapes=[pltpu.VMEM((1, d), F32)],
        compiler_params=pltpu.CompilerParams(vmem_limit_bytes=VMEM_LIMIT),
        name="rglru_sample",
    )(up, z, h0, cw, cb.reshape(1, d), wri, br.reshape(1, d), bi.reshape(1, d), lam.reshape(1, d))


def _outproj_ple_kernel(a_ref, res_ref, p_ref, wo_ref, wp_ref, gn_ref, wg_ref, o_ref, h_scr, hn_scr):
    tm, d = res_ref.shape
    ch = min(d, 512)
    a = a_ref[...]
    ssq = jnp.zeros((tm, 1), F32)
    for c in range(d // ch):
        sl = slice(c * ch, (c + 1) * ch)
        h = res_ref[:, sl] + jnp.dot(a, wo_ref[:, sl], preferred_element_type=F32)
        h_scr[:, sl] = h
        ssq = ssq + jnp.sum(h * h, axis=-1, keepdims=True)
    inv = lax.rsqrt(ssq * (1.0 / d) + EPS)
    for c in range(d // ch):
        sl = slice(c * ch, (c + 1) * ch)
        hn_scr[:, sl] = (h_scr[:, sl] * inv * gn_ref[:, sl]).astype(BF16)
    pb = p_ref[...].astype(BF16)
    hn = hn_scr[...]
    for c in range(d // ch):
        sl = slice(c * ch, (c + 1) * ch)
        gate = _sigmoid(jnp.dot(hn, wg_ref[:, sl], preferred_element_type=F32))
        pp = jnp.dot(pb, wp_ref[:, sl], preferred_element_type=F32)
        o_ref[:, sl] = h_scr[:, sl] + pp * gate


def _outproj_ple(a, res, p, w_out, w_proj, g_norm, w_gate):
    m, d = res.shape
    pd = p.shape[1]
    tm = min(m, 512)
    return pl.pallas_call(
        _outproj_ple_kernel,
        grid=(m // tm,),
        in_specs=[pl.BlockSpec((tm, a.shape[1]), lambda i: (i, 0)),
                  pl.BlockSpec((tm, d), lambda i: (i, 0)),
                  pl.BlockSpec((tm, pd), lambda i: (i, 0)),
                  _resident(w_out.shape, lambda i: (0, 0)),
                  _resident(w_proj.shape, lambda i: (0, 0)),
                  _resident((1, d), lambda i: (0, 0)),
                  _resident(w_gate.shape, lambda i: (0, 0))],
        out_specs=pl.BlockSpec((tm, d), lambda i: (i, 0)),
        out_shape=jax.ShapeDtypeStruct((m, d), F32),
        scratch_shapes=[pltpu.VMEM((tm, d), F32), pltpu.VMEM((tm, d), BF16)],
        compiler_params=_cparams(("parallel",)),
        name="outproj_ple",
    )(a, res, p, w_out, w_proj, g_norm.reshape(1, d), w_gate)


def _kv_proj_kernel(x_ref, g_ref, w_ref, gsel_ref, gwin_ref,
                    kc_ref, vc_ref, ks_ref, vs_ref, kw_ref, vw_ref, ksb_ref, vsb_ref, kwb_ref, vwb_ref):
    tm = x_ref.shape[0]
    xn = _rms_rows(x_ref[...], g_ref[...]).astype(BF16)
    outs = (kc_ref, vc_ref, ks_ref, vs_ref, kw_ref, vw_ref)
    head_gain = {2: gsel_ref, 4: gwin_ref}
    bf_outs = {2: ksb_ref, 3: vsb_ref, 4: kwb_ref, 5: vwb_ref}
    for s in range(6):
        y = jnp.dot(xn, w_ref[:, s * KV_W:(s + 1) * KV_W], preferred_element_type=F32)
        for g in range(N_KV):
            yg = y[:, g * HEAD_DIM:(g + 1) * HEAD_DIM]
            if s in head_gain:
                yg = _rms_rows(yg, head_gain[s][...])
            outs[s][pl.ds(g, tm, stride=N_KV), :] = yg
            if s in bf_outs:
                bf_outs[s][:, g * HEAD_DIM:(g + 1) * HEAD_DIM] = yg.astype(BF16)


def _kv_proj(x, g, w, g_sel, g_win):
    m, d = x.shape
    tm = min(m, 512)
    row = lambda i: (i, 0)
    return pl.pallas_call(
        _kv_proj_kernel,
        grid=(m // tm,),
        in_specs=[pl.BlockSpec((tm, d), row), _resident((1, d), lambda i: (0, 0)),
                  _resident(w.shape, lambda i: (0, 0)),
                  _resident((1, HEAD_DIM), lambda i: (0, 0)), _resident((1, HEAD_DIM), lambda i: (0, 0))],
        out_specs=[pl.BlockSpec((tm * N_KV, HEAD_DIM), row)] * 6 + [pl.BlockSpec((tm, KV_W), row)] * 4,
        out_shape=([jax.ShapeDtypeStruct((m * N_KV, HEAD_DIM), F32)] * 6
                   + [jax.ShapeDtypeStruct((m, KV_W), BF16)] * 4),
        compiler_params=_cparams(("parallel",)),
        name="kv_proj",
    )(x, g.reshape(1, d), w, g_sel.reshape(1, HEAD_DIM), g_win.reshape(1, HEAD_DIM))


PIECE_ROWS = CMP_STRIDE * N_KV
PIECE_PITCH = PIECE_ROWS + SUBLANES


def _piece_rows(buf, r, n):
    flat = buf.reshape(buf.shape[0] * PIECE_PITCH, HEAD_DIM)
    return flat[pl.ds(r, n, stride=PIECE_PITCH), :]


def _compress_kernel(pt_ref, kc_hbm, vc_hbm, pe_ref, w1_ref, b1_ref, w2_ref, b2_ref, gk_ref,
                     kcmp_ref, vcmp_ref, kbuf, vbuf, sem, lhs, *, n_quarters, pages_per_step):
    s = pl.program_id(0)
    n_steps = pl.num_programs(0)
    pieces_per_page = kc_hbm.shape[1]
    n_piece = pages_per_step * pieces_per_page
    n_load = n_piece + SUBLANES
    rows_g = n_piece + 2 * SUBLANES
    hidden = w2_ref.shape[1]

    def copies(step, slot):
        b = step // n_quarters
        q = step % n_quarters
        out = []
        for hbm, buf, which in ((kc_hbm, kbuf, 0), (vc_hbm, vbuf, 1)):
            for p in range(pages_per_step):
                pg = pt_ref[b, q * pages_per_step + p]
                dst = buf.at[slot, pl.ds(p * pieces_per_page, pieces_per_page), pl.ds(0, PIECE_ROWS), :]
                out.append(pltpu.make_async_copy(hbm.at[pg], dst, sem.at[slot, which]))
            pg = pt_ref[b, q * pages_per_step + pages_per_step]
            out.append(pltpu.make_async_copy(hbm.at[pg, 0], buf.at[slot, n_piece, pl.ds(0, PIECE_ROWS), :],
                                             sem.at[slot, which]))
        return out

    slot = s % 2

    @pl.when(s == 0)
    def _():
        kbuf[...] = jnp.zeros(kbuf.shape, F32)
        vbuf[...] = jnp.zeros(vbuf.shape, F32)
        lhs[...] = jnp.zeros(lhs.shape, BF16)
        lhs[:, N_KV * rows_g:, :] = pe_ref[...]
        for c in copies(0, 0):
            c.start()

    @pl.when(s + 1 < n_steps)
    def _():
        for c in copies(s + 1, 1 - slot):
            c.start()

    for c in copies(s, slot):
        c.wait()

    for which, buf, out_ref in ((0, kbuf, kcmp_ref), (1, vbuf, vcmp_ref)):
        for g in range(N_KV):
            for pos in range(CMP_STRIDE):
                x = _piece_rows(buf.at[slot], pos * N_KV + g, n_load)
                lhs[which, g * rows_g:g * rows_g + n_load, pos * HEAD_DIM:(pos + 1) * HEAD_DIM] = x.astype(BF16)
        hab = jnp.dot(lhs[which], w1_ref[which], preferred_element_type=F32)
        pe0 = N_KV * rows_g
        bias = (jnp.sum(hab[pe0:pe0 + SUBLANES, :hidden], axis=0, keepdims=True)
                + jnp.sum(hab[pe0 + SUBLANES:pe0 + 2 * SUBLANES, hidden:], axis=0, keepdims=True) + b1_ref[which])
        for g in range(N_KV):
            blk = hab[g * rows_g:(g + 1) * rows_g]
            nxt = pltpu.roll(blk[:, hidden:], rows_g - 1, axis=0)
            hid = blk[:n_piece, :hidden] + nxt[:n_piece] + bias
            hid = hid * _sigmoid(hid)
            y = jnp.dot(hid.astype(BF16), w2_ref[which], preferred_element_type=F32) + b2_ref[which]
            if which == 0:
                y = _rms_rows(y, gk_ref[...])
            out_ref[0, g] = y.astype(BF16)


def _compress(kc_pages, vc_pages, table, pe_rows, w1, b1, w2, b2, g_k):
    nb, n_pages = table.shape[0], table.shape[1] - 1
    ppp = kc_pages.shape[1]
    pps = min(CMP_PAGES_PER_STEP, n_pages)
    n_quarters = n_pages // pps
    n_piece = pps * ppp
    rows_g = n_piece + 2 * SUBLANES
    kdim = CMP_STRIDE * HEAD_DIM
    full = lambda shape: pl.BlockSpec(shape, lambda s, pt: (0,) * len(shape))
    out_spec = pl.BlockSpec((1, N_KV, n_piece, HEAD_DIM), lambda s, pt: (s // n_quarters, 0, s % n_quarters, 0))
    out_sd = jax.ShapeDtypeStruct((nb, N_KV, n_pages * ppp, HEAD_DIM), BF16)
    buf = pltpu.VMEM((2, n_piece + SUBLANES, PIECE_PITCH, HEAD_DIM), F32)
    grid_spec = pltpu.PrefetchScalarGridSpec(
        num_scalar_prefetch=1,
        grid=(nb * n_quarters,),
        in_specs=[pl.BlockSpec(memory_space=pl.ANY), pl.BlockSpec(memory_space=pl.ANY),
                  full(pe_rows.shape), full(w1.shape), full(b1.shape), full(w2.shape), full(b2.shape),
                  full((1, HEAD_DIM))],
        out_specs=[out_spec, out_spec],
        scratch_shapes=[buf, buf, pltpu.SemaphoreType.DMA((2, 2)),
                        pltpu.VMEM((2, N_KV * rows_g + 2 * SUBLANES, kdim), BF16)],
    )
    return pl.pallas_call(
        functools.partial(_compress_kernel, n_quarters=n_quarters, pages_per_step=pps),
        grid_spec=grid_spec,
        out_shape=[out_sd, out_sd],
        compiler_params=_cparams(("arbitrary",)),
        name="compress",
    )(table, kc_pages, vc_pages, pe_rows, w1, b1, w2, b2, g_k.reshape(1, HEAD_DIM))


def _rel_bucket(dist):
    n = jnp.maximum(dist, 0)
    max_exact = N_BUCKETS // 2
    nf = jnp.maximum(n, 1).astype(F32)
    large = max_exact + (jnp.log(nf / max_exact) / math.log(MAX_DISTANCE / max_exact)
                         * (N_BUCKETS - max_exact)).astype(jnp.int32)
    large = jnp.minimum(large, N_BUCKETS - 1)
    return jnp.where(n < max_exact, n, large)


def _bias_by_distance(rel_bias, n_dist):
    return rel_bias.astype(F32)[_rel_bucket(jnp.arange(n_dist))].T


def _first_constant_distance():
    ratio = MAX_DISTANCE / (N_BUCKETS // 2)
    return int(math.ceil((N_BUCKETS // 2) * ratio ** ((N_BUCKETS // 2 - 1) / (N_BUCKETS // 2)))) + 2


def _topk_rows(score, blk, k_top):
    n_blk = score.shape[0]
    sel = jnp.zeros(score.shape, jnp.bool_)
    for _ in range(k_top):
        m = jnp.max(score, axis=0, keepdims=True)
        idx = jnp.min(jnp.where(score == m, blk, n_blk), axis=0, keepdims=True)
        pick = blk == idx
        sel = sel | pick
        score = jnp.where(pick, REMOVED, score)
    return sel


def _attn_prompt_kernel(q_ref, z0_ref, z1_ref, z2_ref, gl_ref, gq_ref, kcmp_ref, vcmp_ref,
                        ks_ref, vs_ref, kw_ref, vw_ref, rr_ref, sb_ref, ex_ref, o_ref,
                        q_scr, psum_scr, sel_scr, m_scr, l_scr, acc_scr, *, n_qtiles, bank_k):
    qi = pl.program_id(1)
    q0 = qi * TQ
    n_cmp_rows = kcmp_ref.shape[1]
    n_sb = n_cmp_rows * CMP_STRIDE // SEL_BLOCK
    hg = HEADS_PER_GROUP
    scale = HEAD_DIM ** -0.5

    for h in range(hg):
        qh = _rms_rows(q_ref[:, h * HEAD_DIM:(h + 1) * HEAD_DIM], gq_ref[...])
        q_scr[h * TQ:(h + 1) * TQ, :] = qh.astype(BF16)
    q_all = q_scr[...]

    t_row = q0 + lax.broadcasted_iota(jnp.int32, (1, TQ), 1)
    t_col = q0 + lax.broadcasted_iota(jnp.int32, (TQ, 1), 0)

    s_t = lax.dot_general(kcmp_ref[0], q_all, (((1,), (1,)), ((), ())), preferred_element_type=F32)
    ci = lax.broadcasted_iota(jnp.int32, (n_cmp_rows, 1), 0)
    cmp_valid = (ci * CMP_STRIDE + (CMP_BLOCK - 1) <= t_row) & (ci < n_cmp_rows - 1)
    rr_start = pl.multiple_of((n_qtiles - 1 - qi) * SUBLANES, SUBLANES)
    psum = jnp.zeros((n_cmp_rows, TQ), F32)
    o_cmp = []
    for h in range(hg):
        sh = s_t[:, h * TQ:(h + 1) * TQ] * scale + rr_ref[h, pl.ds(rr_start, n_cmp_rows), :]
        sh = jnp.where(cmp_valid, sh, NEG)
        mx = jnp.max(sh, axis=0, keepdims=True)
        e = jnp.where(cmp_valid, jnp.exp(sh - mx), 0.0)
        den = jnp.sum(e, axis=0, keepdims=True)
        p = e / jnp.where(den > 0.0, den, 1.0)
        psum = psum + p
        o_cmp.append(lax.dot_general(p.astype(BF16), vcmp_ref[0], (((0,), (0,)), ((), ())),
                                     preferred_element_type=F32))

    psum_scr[0:SUBLANES, :] = jnp.zeros((SUBLANES, TQ), F32)
    psum_scr[SUBLANES:, :] = psum
    ratio = SEL_BLOCK // CMP_STRIDE
    imp = psum_scr[pl.ds(SUBLANES - 1, n_sb, stride=ratio), :]
    for k in range(ratio):
        imp = imp + psum_scr[pl.ds(SUBLANES + k, n_sb, stride=ratio), :]
    blk = lax.broadcasted_iota(jnp.int32, (n_sb, 1), 0)
    cur = t_row // SEL_BLOCK
    blk_ok = blk <= cur
    forced = (blk == 0) | (blk == cur) | (blk == cur - 1)
    score = jnp.where(blk_ok, jnp.where(forced, FORCE, imp), -FORCE)
    sel_t = _topk_rows(score, blk, min(N_SELECT, n_sb)) & blk_ok
    sel_scr[...] = jnp.where(sel_t, 1.0, 0.0).T.astype(BF16)

    m_scr[...] = jnp.full(m_scr.shape, NEG, F32)
    l_scr[...] = jnp.zeros(l_scr.shape, F32)
    acc_scr[...] = jnp.zeros(acc_scr.shape, F32)
    n_kv = (q0 + TQ + TK_SEL - 1) // TK_SEL
    rep = TK_SEL // LANES

    def kv_step(j, carry):
        k0 = pl.multiple_of(j * TK_SEL, TK_SEL)
        kt = ks_ref[pl.ds(k0, TK_SEL), :]
        vt = vs_ref[pl.ds(k0, TK_SEL), :]
        s = lax.dot_general(q_all, kt, (((1,), (1,)), ((), ())), preferred_element_type=F32)
        picked = jnp.dot(sel_scr[...], ex_ref[:, pl.ds(k0, TK_SEL)], preferred_element_type=F32)
        kpos = k0 + lax.broadcasted_iota(jnp.int32, (1, TK_SEL), 1)
        valid = (picked > 0.5) & (kpos <= t_col)
        b_start = pl.multiple_of(jnp.maximum(bank_k - (q0 - k0), 0), LANES)
        for h in range(hg):
            rows = slice(h * TQ, (h + 1) * TQ)
            sh = s[rows] * scale + sb_ref[h, :, pl.ds(b_start, TK_SEL)]
            sh = jnp.where(valid, sh, NEG)
            m_prev = m_scr[rows]
            m_next = jnp.maximum(m_prev, jnp.max(sh, axis=1, keepdims=True))
            alpha = jnp.exp(m_prev - m_next)
            e = jnp.exp(sh - jnp.tile(m_next, (1, rep)))
            l_scr[rows] = alpha * l_scr[rows] + jnp.sum(e, axis=1, keepdims=True)
            acc_scr[rows] = alpha * acc_scr[rows] + jnp.dot(e.astype(BF16), vt, preferred_element_type=F32)
            m_scr[rows] = m_next
        return carry

    lax.fori_loop(0, n_kv, kv_step, 0)

    wk = WINDOW + TQ
    w0 = pl.multiple_of(jnp.maximum(q0 - WINDOW, 0), LANES)
    s = lax.dot_general(q_all, kw_ref[pl.ds(w0, wk), :], (((1,), (1,)), ((), ())), preferred_element_type=F32)
    vwin = vw_ref[pl.ds(w0, wk), :]
    dist = t_col - (w0 + lax.broadcasted_iota(jnp.int32, (1, wk), 1))
    win_valid = (dist >= 0) & (dist < WINDOW)
    wb_start = pl.multiple_of(bank_k - (q0 - w0), LANES)

    sig_gl = _sigmoid(gl_ref[0])
    z_refs = (z0_ref, z1_ref, z2_ref)
    for h in range(hg):
        rows = slice(h * TQ, (h + 1) * TQ)
        cols = slice(h * HEAD_DIM, (h + 1) * HEAD_DIM)
        sh = s[rows] * scale + sb_ref[h, :, pl.ds(wb_start, wk)]
        sh = jnp.where(win_valid, sh, NEG)
        e = jnp.exp(sh - jnp.max(sh, axis=1, keepdims=True))
        p = e / jnp.sum(e, axis=1, keepdims=True)
        o_win = jnp.dot(p.astype(BF16), vwin, preferred_element_type=F32)
        o_sel = acc_scr[rows] / l_scr[rows]
        out = jnp.zeros((TQ, HEAD_DIM), F32)
        for br, o_br in enumerate((o_cmp[h], o_sel, o_win)):
            z = z_refs[br][:, cols]
            c = br * hg + h
            out = out + sig_gl[:, c:c + 1] * o_br * (z * _sigmoid(z))
        o_ref[:, cols] = out.astype(BF16)


def _attn_prompt(proj, gl_g, g_q, kcmp, vcmp, ks_b, vs_b, kw_b, vw_b, rr, sb, expand, bank_k):
    s_len = proj.shape[0]
    n_qt = s_len // TQ
    n_cmp_rows = kcmp.shape[2]
    n_sb = expand.shape[0]
    gcols = N_HEADS * HEAD_DIM // GROUP_W
    qz = lambda seg: pl.BlockSpec((TQ, GROUP_W), lambda g, i, seg=seg: (i, seg * gcols + g))
    per_group = lambda shape: _resident((1,) + shape, lambda g, i: (g,) + (0,) * len(shape))
    kv_spec = _resident((s_len, HEAD_DIM), lambda g, i: (0, g))
    return pl.pallas_call(
        functools.partial(_attn_prompt_kernel, n_qtiles=n_qt, bank_k=bank_k),
        grid=(N_KV, n_qt),
        in_specs=[qz(0), qz(1), qz(2), qz(3),
                  pl.BlockSpec((1, TQ, LANES), lambda g, i: (g, i, 0)),
                  _resident((1, HEAD_DIM), lambda g, i: (0, 0)),
                  per_group((n_cmp_rows, HEAD_DIM)), per_group((n_cmp_rows, HEAD_DIM)),
                  kv_spec, kv_spec, kv_spec, kv_spec,
                  _resident((HEADS_PER_GROUP,) + rr.shape[1:], lambda g, i: (g, 0, 0)),
                  _resident((HEADS_PER_GROUP,) + sb.shape[1:], lambda g, i: (g, 0, 0)),
                  _resident(expand.shape, lambda g, i: (0, 0))],
        out_specs=pl.BlockSpec((TQ, GROUP_W), lambda g, i: (i, g)),
        out_shape=jax.ShapeDtypeStruct((s_len, N_HEADS * HEAD_DIM), BF16),
        scratch_shapes=[pltpu.VMEM((HEADS_PER_GROUP * TQ, HEAD_DIM), BF16),
                        pltpu.VMEM((n_cmp_rows + SUBLANES, TQ), F32),
                        pltpu.VMEM((TQ, n_sb), BF16),
                        pltpu.VMEM((HEADS_PER_GROUP * TQ, LANES), F32),
                        pltpu.VMEM((HEADS_PER_GROUP * TQ, LANES), F32),
                        pltpu.VMEM((HEADS_PER_GROUP * TQ, HEAD_DIM), F32)],
        compiler_params=_cparams(("arbitrary", "arbitrary")),
        name="attn_prompt",
    )(proj, proj, proj, proj, gl_g, g_q.reshape(1, HEAD_DIM), kcmp[0], vcmp[0],
      ks_b, vs_b, kw_b, vw_b, rr, sb, expand)


def _attn_sample_kernel(*refs, n_tok, pages_per_chunk, page, past_len):
    pt_ref = refs[0]
    q_ref, z_ref, gl_ref, gq_ref, kcmp_ref, vcmp_ref = refs[1:7]
    kpages = refs[7:7 + pages_per_chunk]
    vpages = refs[7 + pages_per_chunk:7 + 2 * pages_per_chunk]
    (ksn_ref, vsn_ref, kwn_ref, vwn_ref, kwc_ref, vwc_ref, bcmp_ref, bsel_ref, bwin_ref, ov_ref, ex_ref,
     o_ref, sel_scr, m_scr, l_scr, acc_scr, q_scr, ocmp_scr) = refs[7 + 2 * pages_per_chunk:]
    del pt_ref
    c = pl.program_id(1)
    n_chunks = pl.num_programs(1)
    hg = HEADS_PER_GROUP
    nr = hg * n_tok
    scale = HEAD_DIM ** -0.5
    n_cmp_rows = kcmp_ref.shape[2]
    n_sb = ov_ref.shape[1]
    chunk = pages_per_chunk * page
    tok = lax.broadcasted_iota(jnp.int32, (nr, 1), 0) % n_tok

    @pl.when(c == 0)
    def _():
        q_scr[...] = _rms_rows(q_ref[0], gq_ref[...]).astype(BF16)
        m_scr[...] = jnp.full(m_scr.shape, NEG, F32)
        l_scr[...] = jnp.zeros(l_scr.shape, F32)
        acc_scr[...] = jnp.zeros(acc_scr.shape, F32)
        ci = lax.broadcasted_iota(jnp.int32, (1, n_cmp_rows), 1)
        cmp_valid = ci < n_cmp_rows - 1
        for g in range(N_KV):
            rows = slice(g * nr, (g + 1) * nr)
            s = lax.dot_general(q_scr[rows], kcmp_ref[0, g], (((1,), (1,)), ((), ())),
                                preferred_element_type=F32)
            s = jnp.where(cmp_valid, s * scale + bcmp_ref[g], NEG)
            e = jnp.where(cmp_valid, jnp.exp(s - jnp.max(s, axis=1, keepdims=True)), 0.0)
            p = e / jnp.sum(e, axis=1, keepdims=True)
            ocmp_scr[rows] = jnp.dot(p.astype(BF16), vcmp_ref[0, g], preferred_element_type=F32)
            psum = p[0:n_tok]
            for h in range(1, hg):
                psum = psum + p[h * n_tok:(h + 1) * n_tok]
            imp = jnp.dot(psum, ov_ref[...], preferred_element_type=F32, precision=lax.Precision.HIGHEST)
            blk = lax.broadcasted_iota(jnp.int32, (1, n_sb), 1)
            forced = (blk == 0) | (blk == n_sb - 1)
            score = jnp.where(forced, FORCE, imp)
            sel = jnp.zeros(score.shape, jnp.bool_)
            for _ in range(min(N_SELECT, n_sb + 1) - 1):
                mx = jnp.max(score, axis=1, keepdims=True)
                idx = jnp.min(jnp.where(score == mx, blk, n_sb), axis=1, keepdims=True)
                pick = blk == idx
                sel = sel | pick
                score = jnp.where(pick, REMOVED, score)
            sel_scr[rows] = jnp.tile(jnp.where(sel, 1.0, 0.0), (hg, 1)).astype(BF16)

    def online_update(rows, sh, v):
        m_prev = m_scr[rows]
        m_next = jnp.maximum(m_prev, jnp.max(sh, axis=1, keepdims=True))
        alpha = jnp.exp(m_prev - m_next)
        e = jnp.exp(sh - m_next[:, 0:1])
        l_scr[rows] = alpha * l_scr[rows] + jnp.sum(e, axis=1, keepdims=True)
        acc_scr[rows] = alpha * acc_scr[rows] + jnp.dot(e.astype(BF16), v, preferred_element_type=F32)
        m_scr[rows] = m_next

    def group_rows(ref, g):
        return ref[0, pl.ds(g, ref.shape[1] // N_KV, stride=N_KV), :].astype(BF16)

    k0 = pl.multiple_of(c * chunk, chunk)
    for g in range(N_KV):
        rows = slice(g * nr, (g + 1) * nr)
        kt = jnp.concatenate([group_rows(r, g) for r in kpages], axis=0)
        vt = jnp.concatenate([group_rows(r, g) for r in vpages], axis=0)
        s = lax.dot_general(q_scr[rows], kt, (((1,), (1,)), ((), ())), preferred_element_type=F32)
        picked = jnp.dot(sel_scr[rows], ex_ref[:, pl.ds(k0, chunk)], preferred_element_type=F32)
        sh = jnp.where(picked > 0.5, s * scale + bsel_ref[g, :, pl.ds(k0, chunk)], NEG)
        online_update(rows, sh, vt)

    @pl.when(c == n_chunks - 1)
    def _():
        new_w = ksn_ref.shape[1] // N_KV
        jn = lax.broadcasted_iota(jnp.int32, (1, new_w), 1)
        new_valid = jn <= tok
        wbuf = kwc_ref.shape[1] // N_KV
        jw = lax.broadcasted_iota(jnp.int32, (1, wbuf), 1)
        dist_c = (past_len + tok) - (past_len - wbuf + jw)
        win_valid_c = dist_c < WINDOW
        sig_gl = _sigmoid(gl_ref[0])
        for g in range(N_KV):
            rows = slice(g * nr, (g + 1) * nr)
            qg = q_scr[rows]
            s = lax.dot_general(qg, group_rows(ksn_ref, g), (((1,), (1,)), ((), ())),
                                preferred_element_type=F32)
            sh = jnp.where(new_valid, s * scale + bsel_ref[g, :, pl.ds(past_len, new_w)], NEG)
            online_update(rows, sh, group_rows(vsn_ref, g))
            o_sel = acc_scr[rows] / l_scr[rows]
            s_c = lax.dot_general(qg, group_rows(kwc_ref, g), (((1,), (1,)), ((), ())),
                                  preferred_element_type=F32)
            s_n = lax.dot_general(qg, group_rows(kwn_ref, g), (((1,), (1,)), ((), ())),
                                  preferred_element_type=F32)
            s_c = jnp.where(win_valid_c, s_c * scale + bwin_ref[g, :, 0:wbuf], NEG)
            s_n = jnp.where(new_valid, s_n * scale + bwin_ref[g, :, wbuf:wbuf + new_w], NEG)
            mx = jnp.maximum(jnp.max(s_c, axis=1, keepdims=True), jnp.max(s_n, axis=1, keepdims=True))
            e_c = jnp.exp(s_c - mx)
            e_n = jnp.exp(s_n - mx)
            den = jnp.sum(e_c, axis=1, keepdims=True) + jnp.sum(e_n, axis=1, keepdims=True)
            o_win = (jnp.dot(e_c.astype(BF16), group_rows(vwc_ref, g), preferred_element_type=F32)
                     + jnp.dot(e_n.astype(BF16), group_rows(vwn_ref, g), preferred_element_type=F32)) / den
            out = jnp.zeros((nr, HEAD_DIM), F32)
            for br, o_br in enumerate((ocmp_scr[rows], o_sel, o_win)):
                z = z_ref[0, br, rows]
                out = out + sig_gl[br, rows] * o_br * (z * _sigmoid(z))
            o_ref[0, rows] = out.astype(BF16)


def _attn_sample(q_s, z_s, gl_s, g_q, kcmp, vcmp, ksel_pages, vsel_pages, page_table, ks_new, vs_new,
                 kw_new, vw_new, kw_cache, vw_cache, bcmp, bsel, bwin, overlap, expand, n_tok):
    nb = q_s.shape[0]
    n_pages = page_table.shape[1]
    page = ksel_pages.shape[1] // N_KV
    ppc = min(8, n_pages)
    n_chunks = n_pages // ppc
    nrow = q_s.shape[1]
    per_b = lambda shape: pl.BlockSpec((1,) + shape, lambda b, c, pt: (b,) + (0,) * len(shape))
    full = lambda shape: _resident(shape, lambda b, c, pt: (0,) * len(shape))
    page_spec = lambda i: pl.BlockSpec((1, page * N_KV, HEAD_DIM),
                                       lambda b, c, pt, i=i: (pt[b, c * ppc + i], 0, 0))
    in_specs = ([per_b(q_s.shape[1:]), per_b(z_s.shape[1:]), per_b(gl_s.shape[1:]), full((1, HEAD_DIM)),
                 per_b(kcmp.shape[1:]), per_b(vcmp.shape[1:])]
                + [page_spec(i) for i in range(ppc)] + [page_spec(i) for i in range(ppc)]
                + [per_b(ks_new.shape[1:])] * 4 + [per_b(kw_cache.shape[1:])] * 2
                + [full(bcmp.shape), full(bsel.shape), full(bwin.shape), full(overlap.shape), full(expand.shape)])
    grid_spec = pltpu.PrefetchScalarGridSpec(
        num_scalar_prefetch=1,
        grid=(nb, n_chunks),
        in_specs=in_specs,
        out_specs=pl.BlockSpec((1, nrow, HEAD_DIM), lambda b, c, pt: (b, 0, 0)),
        scratch_shapes=[pltpu.VMEM((nrow, expand.shape[0]), BF16),
                        pltpu.VMEM((nrow, LANES), F32), pltpu.VMEM((nrow, LANES), F32),
                        pltpu.VMEM((nrow, HEAD_DIM), F32), pltpu.VMEM((nrow, HEAD_DIM), BF16),
                        pltpu.VMEM((nrow, HEAD_DIM), F32)],
    )
    return pl.pallas_call(
        functools.partial(_attn_sample_kernel, n_tok=n_tok, pages_per_chunk=ppc, page=page,
                          past_len=n_pages * page),
        grid_spec=grid_spec,
        out_shape=jax.ShapeDtypeStruct((nb, nrow, HEAD_DIM), BF16),
        compiler_params=_cparams(("arbitrary", "arbitrary")),
        name="attn_sample",
    )(page_table, q_s, z_s, gl_s, g_q.reshape(1, HEAD_DIM), kcmp, vcmp,
      *([ksel_pages] * ppc), *([vsel_pages] * ppc), ks_new, vs_new, kw_new, vw_new, kw_cache, vw_cache,
      bcmp, bsel, bwin, overlap, expand)


def _toeplitz_bank(fd, n_rows, bank_k, width):
    i = jnp.arange(n_rows)[:, None]
    c = jnp.arange(width)[None, :]
    return fd[:, jnp.clip(i - c + bank_k, 0, fd.shape[1] - 1)]


def kernel(x_prompt, x_sample, p_prompt, p_sample, state_conv, state_h, cache_k_cmp, cache_v_cmp, cache_k_sel, cache_v_sel, cache_k_win, cache_v_win, page_table, norm_g, a_w_in, a_conv_w, a_conv_b, a_w_r, a_b_r, a_w_i, a_b_i, a_lam, a_w_out, kv_norm_g, w_kv, g_k_cmp, g_k_sel, g_k_win, cmp_pos, cmp_w1a, cmp_w1b, cmp_b1, cmp_w2, cmp_b2, b_w_in, b_g_q, b_w_out, rel_bias, ple_w_proj, ple_norm_g, ple_w_gate):
    bp, s_len, d = x_prompt.shape
    db, ds, _ = x_sample.shape
    n_pool, page = cache_k_sel.shape[:2]
    n_pages = page_table.shape[1]
    past_len = n_pages * page
    wbuf = cache_k_win.shape[1]
    hd = N_HEADS * HEAD_DIM
    assert bp == 1 and norm_g.shape[0] == 2 and a_w_in.shape[0] == 1 and b_w_in.shape[0] == 1
    assert d == hd and s_len % TK_SEL == 0 and s_len >= WINDOW + TQ and page % SEL_BLOCK == 0
    assert wbuf == WINDOW and ds <= SEL_BLOCK and past_len % TK_SEL == 0

    w_in0 = a_w_in[0].astype(BF16)
    w_ri = jnp.concatenate([a_w_r[0], a_w_i[0]], axis=-1).astype(BF16)
    w_out0 = a_w_out[0].astype(BF16)
    w_gate = ple_w_gate.astype(BF16)
    w_proj = ple_w_proj.astype(BF16)
    w_kv_b = w_kv.astype(BF16)
    n_in1 = b_w_in.shape[2]
    n_in1_pad = -(-n_in1 // 640) * 640
    w_in1 = jnp.pad(b_w_in[0], ((0, 0), (0, n_in1_pad - n_in1))).astype(BF16)
    w_out1 = b_w_out[0].astype(BF16)
    w1 = jnp.concatenate([cmp_w1a, cmp_w1b], axis=-1).astype(BF16)
    w2 = cmp_w2.astype(BF16)
    pe = cmp_pos.reshape(2, 2, 1, CMP_STRIDE * HEAD_DIM)
    pe_hi = pe.astype(BF16)
    pe_lo = (pe - pe_hi.astype(F32)).astype(BF16)
    pe_rows = jnp.concatenate([pe_hi, pe_lo, jnp.zeros((2, 2, SUBLANES - 2, pe.shape[-1]), BF16)], axis=2)
    pe_rows = pe_rows.reshape(2, 2 * SUBLANES, pe.shape[-1])
    b1 = cmp_b1.reshape(2, 1, -1)
    b2 = cmp_b2.reshape(2, 1, HEAD_DIM)

    fd = _bias_by_distance(rel_bias, past_len + 2 * LANES + s_len)

    xp = x_prompt.reshape(s_len, d)
    xs = x_sample.reshape(db * ds, d)
    uz_p = _norm_matmul(xp, norm_g[0], w_in0, 512)
    uz_s = _norm_matmul(xs, norm_g[0], w_in0, 512)
    rg = (a_conv_w[0], a_conv_b[0], w_ri, a_b_r[0], a_b_i[0], a_lam[0])
    gated_p, h_p = _rglru_prompt(uz_p, jnp.zeros((SUBLANES, d), F32), jnp.zeros((1, d), F32), *rg)
    conv_p = uz_p[s_len - (CONV_W - 1):, :d].reshape(1, bp, CONV_W - 1, d)
    u_s = uz_s[:, :d].reshape(db, ds, d).transpose(1, 0, 2)
    z_s = uz_s[:, d:].reshape(db, ds, d).transpose(1, 0, 2)
    up_s = jnp.concatenate([state_conv[0].transpose(1, 0, 2), u_s], axis=0)
    gated_s, h_s = _rglru_sample(up_s, z_s, state_h[0], *rg)
    conv_s = up_s[ds:].transpose(1, 0, 2)[None]
    gated_s = gated_s.transpose(1, 0, 2).reshape(db * ds, d)
    ple0 = (w_proj[0], ple_norm_g[0], w_gate[0])
    h1_p = _outproj_ple(gated_p, xp, p_prompt[0].reshape(s_len, -1), w_out0, *ple0)
    h1_s = _outproj_ple(gated_s, xs, p_sample[0].reshape(db * ds, -1), w_out0, *ple0)

    kv_p = _kv_proj(h1_p, kv_norm_g, w_kv_b, g_k_sel, g_k_win)
    kv_s = _kv_proj(h1_s, kv_norm_g, w_kv_b, g_k_sel, g_k_win)
    kc_p, vc_p, ks_p, vs_p, kw_p, vw_p = kv_p[:6]
    kc_s, vc_s, ks_s, vs_s, kw_s, vw_s = kv_s[:6]

    p_pages = s_len // page
    ident = jnp.minimum(jnp.arange(p_pages + 1, dtype=jnp.int32), p_pages - 1)[None]
    cmp_w = (pe_rows, w1, b1, w2, b2, g_k_cmp)
    pieces = lambda x: x.reshape(-1, page // CMP_STRIDE, PIECE_ROWS, HEAD_DIM)
    kcmp_p, vcmp_p = _compress(pieces(kc_p), pieces(vc_p), ident, *cmp_w)
    table_ext = jnp.concatenate([page_table, page_table[:, -1:]], axis=1)
    kcmp_s, vcmp_s = _compress(pieces(cache_k_cmp), pieces(cache_v_cmp), table_ext, *cmp_w)

    proj_p = _norm_matmul(h1_p, norm_g[1], w_in1, 640)
    proj_s = _norm_matmul(h1_s, norm_g[1], w_in1, 640)

    def gate_logits(proj, n_rows):
        gl = proj[:, 4 * hd:4 * hd + 3 * N_HEADS].reshape(n_rows, 3, N_KV, HEADS_PER_GROUP)
        return gl.transpose(2, 0, 1, 3).reshape(N_KV, n_rows, 3 * HEADS_PER_GROUP)

    n_qt = s_len // TQ
    n_cmp_rows = s_len // CMP_STRIDE
    j = (n_qt - 1) - jnp.arange(n_qt + n_cmp_rows // SUBLANES - 1)
    dist = (TQ * j[:, None, None] + jnp.arange(TQ)[None, None, :]
            - CMP_STRIDE * jnp.arange(SUBLANES)[None, :, None] - (CMP_BLOCK - 1))
    rr = fd[:, jnp.clip(dist, 0, None)].reshape(N_HEADS, -1, TQ)
    bank_w = max(TK_SEL, WINDOW + TQ)
    bank_k = min(-(-(_first_constant_distance() + bank_w) // LANES) * LANES, s_len)
    sb = _toeplitz_bank(fd, TQ, bank_k, bank_k + bank_w)
    n_sb_p = s_len // SEL_BLOCK
    expand_p = (jnp.arange(s_len)[None, :] // SEL_BLOCK == jnp.arange(n_sb_p)[:, None]).astype(BF16)
    gl_p = jnp.pad(gate_logits(proj_p, s_len), ((0, 0), (0, 0), (0, LANES - 3 * HEADS_PER_GROUP)))
    o_p = _attn_prompt(proj_p, gl_p, b_g_q[0], kcmp_p, vcmp_p, kv_p[6], kv_p[7], kv_p[8], kv_p[9],
                       rr, sb, expand_p, bank_k)

    def head_rows(x):
        return x.reshape(db, ds, N_HEADS, HEAD_DIM).transpose(0, 2, 1, 3).reshape(db, N_HEADS * ds, HEAD_DIM)

    q_s = head_rows(proj_s[:, :hd])
    zz_s = jnp.stack([head_rows(proj_s[:, (1 + br) * hd:(2 + br) * hd]) for br in range(3)], axis=1)
    gl_s = proj_s[:, 4 * hd:4 * hd + 3 * N_HEADS].reshape(db, ds, 3, N_HEADS).transpose(0, 2, 3, 1)
    gl_s = gl_s.reshape(db, 3, N_HEADS * ds, 1)
    new_w = LANES
    group_rows = lambda x: x.reshape(db, -1, HEAD_DIM)
    pad_new = lambda x: jnp.pad(group_rows(x), ((0, 0), (0, (new_w - ds) * N_KV), (0, 0)))
    tq_s = past_len + jnp.arange(ds)
    fd_g = fd.reshape(N_KV, HEADS_PER_GROUP, -1)

    def sample_bias(kpos):
        dd = jnp.clip(tq_s[:, None] - kpos[None, :], 0, None)
        return fd_g[:, :, dd].reshape(N_KV, HEADS_PER_GROUP * ds, -1)

    n_cmp_s = past_len // CMP_STRIDE
    bcmp = sample_bias(jnp.arange(n_cmp_s) * CMP_STRIDE + CMP_BLOCK - 1)
    bsel = sample_bias(jnp.arange(past_len + new_w))
    bwin = sample_bias(jnp.concatenate([past_len - wbuf + jnp.arange(wbuf), past_len + jnp.arange(new_w)]))
    n_sb_s = past_len // SEL_BLOCK
    ci = jnp.arange(n_cmp_s)[:, None] * CMP_STRIDE
    sj = jnp.arange(n_sb_s)[None, :] * SEL_BLOCK
    overlap = ((ci < sj + SEL_BLOCK) & (ci + CMP_BLOCK > sj) & (ci < past_len - CMP_STRIDE)).astype(F32)
    expand_s = (jnp.arange(past_len)[None, :] // SEL_BLOCK == jnp.arange(n_sb_s)[:, None]).astype(BF16)
    o_s = _attn_sample(q_s, zz_s, gl_s, b_g_q[0], kcmp_s, vcmp_s,
                       cache_k_sel.reshape(n_pool, -1, HEAD_DIM), cache_v_sel.reshape(n_pool, -1, HEAD_DIM),
                       page_table, pad_new(ks_s), pad_new(vs_s), pad_new(kw_s), pad_new(vw_s),
                       group_rows(cache_k_win), group_rows(cache_v_win),
                       bcmp, bsel, bwin, overlap, expand_s, ds)
    o_s = o_s.reshape(db, N_HEADS, ds, HEAD_DIM).transpose(0, 2, 1, 3).reshape(db * ds, hd)

    ple1 = (w_proj[1], ple_norm_g[1], w_gate[1])
    y_p = _outproj_ple(o_p, h1_p, p_prompt[1].reshape(s_len, -1), w_out1, *ple1)
    y_s = _outproj_ple(o_s, h1_s, p_sample[1].reshape(db * ds, -1), w_out1, *ple1)

    rows4 = lambda x, b, t: x.reshape(b, t, N_KV, HEAD_DIM)
    wk_p = min(WINDOW, s_len)
    kwin_s = jnp.concatenate([cache_k_win, rows4(kw_s, db, ds)], axis=1)[:, -WINDOW:]
    vwin_s = jnp.concatenate([cache_v_win, rows4(vw_s, db, ds)], axis=1)[:, -WINDOW:]
    return (y_p.reshape(bp, s_len, d), y_s.reshape(db, ds, d),
            conv_p, h_p.reshape(1, bp, d),
            rows4(kc_p, bp, s_len), rows4(vc_p, bp, s_len), rows4(ks_p, bp, s_len), rows4(vs_p, bp, s_len),
            rows4(kw_p, bp, s_len)[:, -wk_p:], rows4(vw_p, bp, s_len)[:, -wk_p:],
            conv_s, h_s.reshape(1, db, d),
            rows4(kc_s, db, ds), rows4(vc_s, db, ds), rows4(ks_s, db, ds), rows4(vs_s, db, ds),
            kwin_s, vwin_s)
```

```python
import functools
import math

import jax
import jax.numpy as jnp
from jax import lax
from jax.experimental import pallas as pl
from jax.experimental.pallas import tpu as pltpu

F32 = jnp.float32
BF16 = jnp.bfloat16

N_RNN_BLOCKS = 8
CONV_W = 4
LRU_C = 8.0
N_HEADS = 16
HEAD_DIM = 128
N_KV = 4
HEADS_PER_GROUP = N_HEADS // N_KV
GROUP_W = HEADS_PER_GROUP * HEAD_DIM
KV_W = N_KV * HEAD_DIM
CMP_BLOCK = 32
CMP_STRIDE = 16
SEL_BLOCK = 64
N_SELECT = 16
WINDOW = 512
N_BUCKETS = 32
MAX_DISTANCE = 4096
EPS = 1e-6
NEG = -1e30
FORCE = 1e9
REMOVED = -3e38
LOG2E = math.log2(math.e)

LANES = 128
SUBLANES = 8
VMEM_LIMIT = 56 * 1024 * 1024

TQ = 128
TK_SEL = 512
CMP_PAGES_PER_STEP = 16


def _cparams(sem):
    return pltpu.CompilerParams(dimension_semantics=sem, vmem_limit_bytes=VMEM_LIMIT)


def _resident(shape, index_map):
    return pl.BlockSpec(shape, index_map, pipeline_mode=pl.Buffered(1))


def _rms_rows(x, g):
    return x * lax.rsqrt(jnp.mean(x * x, axis=-1, keepdims=True) + EPS) * g


def _sigmoid(x):
    return 1.0 / (1.0 + jnp.exp(-x))


def _norm_matmul_kernel(x_ref, g_ref, w_ref, o_ref, xn_ref):
    @pl.when(pl.program_id(1) == 0)
    def _():
        xn_ref[...] = _rms_rows(x_ref[...], g_ref[...]).astype(BF16)

    o_ref[...] = jnp.dot(xn_ref[...], w_ref[...], preferred_element_type=F32)


def _norm_matmul(x, g, w, tn):
    m, d = x.shape
    n = w.shape[1]
    tm = min(m, 1024)
    return pl.pallas_call(
        _norm_matmul_kernel,
        grid=(m // tm, n // tn),
        in_specs=[pl.BlockSpec((tm, d), lambda i, j: (i, 0)),
                  pl.BlockSpec((1, d), lambda i, j: (0, 0)),
                  pl.BlockSpec((d, tn), lambda i, j: (0, j))],
        out_specs=pl.BlockSpec((tm, tn), lambda i, j: (i, j)),
        out_shape=jax.ShapeDtypeStruct((m, n), F32),
        scratch_shapes=[pltpu.VMEM((tm, d), BF16)],
        compiler_params=_cparams(("parallel", "arbitrary")),
        name="norm_matmul",
    )(x, g.reshape(1, d), w)


def _lru_gates(xc, wri_ref, br_ref, bi_ref, sp_ref, n, bw):
    sl = slice(n * bw, (n + 1) * bw)
    ri = jnp.dot(xc.astype(BF16), wri_ref[n], preferred_element_type=F32)
    r = _sigmoid(ri[:, :bw] + br_ref[:, sl])
    i = _sigmoid(ri[:, bw:] + bi_ref[:, sl])
    log_a = (-LRU_C) * r * sp_ref[:, sl]
    a = jnp.exp(log_a)
    b = jnp.sqrt(1.0 - jnp.exp(2.0 * log_a)) * (i * xc)
    return a, b


def _softplus(x):
    return jnp.maximum(x, 0.0) + jnp.log(1.0 + jnp.exp(-jnp.abs(x)))


def _rglru_prompt_kernel(u_ref, z_ref, cinit_ref, hinit_ref, cw_ref, cb_ref, wri_ref, br_ref, bi_ref,
                         lam_ref, g_ref, hlast_ref, ubuf, a_scr, b_scr, hs_scr, hcar, sp_scr):
    tm, d = u_ref.shape
    bw = d // N_RNN_BLOCKS

    @pl.when(pl.program_id(0) == 0)
    def _():
        ubuf[0:SUBLANES, :] = cinit_ref[...]
        hcar[...] = hinit_ref[...]
        sp_scr[...] = _softplus(-lam_ref[...])

    ubuf[SUBLANES:, :] = u_ref[...]
    for n in range(N_RNN_BLOCKS):
        sl = slice(n * bw, (n + 1) * bw)
        up = ubuf[:, sl]
        xc = cb_ref[:, sl] + up * cw_ref[CONV_W - 1:CONV_W, sl]
        for k in range(CONV_W - 1):
            xc = xc + pltpu.roll(up, CONV_W - 1 - k, axis=0) * cw_ref[k:k + 1, sl]
        xc = xc[SUBLANES:]
        a, b = _lru_gates(xc, wri_ref, br_ref, bi_ref, sp_scr, n, bw)
        a_scr[:, sl] = a
        b_scr[:, sl] = b
    ubuf[0:SUBLANES, :] = u_ref[tm - SUBLANES:tm, :]

    def step(t, h):
        h = a_scr[pl.ds(t, 1), :] * h + b_scr[pl.ds(t, 1), :]
        hs_scr[pl.ds(t, 1), :] = h
        return h

    h = lax.fori_loop(0, tm, step, hcar[...], unroll=8)
    hcar[...] = h
    hlast_ref[...] = h
    z = z_ref[...]
    g_ref[...] = (hs_scr[...] * (z * _sigmoid(z))).astype(BF16)


def _rglru_prompt(uz, conv_init, h_init, cw, cb, wri, br, bi, lam):
    t, d2 = uz.shape
    d = d2 // 2
    tm = min(t, 256)
    full = lambda shape: pl.BlockSpec(shape, lambda i: (0,) * len(shape))
    return pl.pallas_call(
        _rglru_prompt_kernel,
        grid=(t // tm,),
        in_specs=[pl.BlockSpec((tm, d), lambda i: (i, 0)),
                  pl.BlockSpec((tm, d), lambda i: (i, 1)),
                  full((SUBLANES, d)), full((1, d)), full((CONV_W, d)), full((1, d)),
                  full(wri.shape), full((1, d)), full((1, d)), full((1, d))],
        out_specs=[pl.BlockSpec((tm, d), lambda i: (i, 0)), full((1, d))],
        out_shape=[jax.ShapeDtypeStruct((t, d), BF16), jax.ShapeDtypeStruct((1, d), F32)],
        scratch_shapes=[pltpu.VMEM((tm + SUBLANES, d), F32), pltpu.VMEM((tm, d), F32),
                        pltpu.VMEM((tm, d), F32), pltpu.VMEM((tm, d), F32),
                        pltpu.VMEM((1, d), F32), pltpu.VMEM((1, d), F32)],
        compiler_params=_cparams(("arbitrary",)),
        name="rglru_prompt",
    )(uz, uz, conv_init, h_init, cw, cb.reshape(1, d), wri, br.reshape(1, d), bi.reshape(1, d),
      lam.reshape(1, d))


def _rglru_sample_kernel(up_ref, z_ref, h0_ref, cw_ref, cb_ref, wri_ref, br_ref, bi_ref, lam_ref,
                         g_ref, hlast_ref, sp_scr):
    n_t = z_ref.shape[0]
    d = z_ref.shape[2]
    bw = d // N_RNN_BLOCKS
    sp_scr[...] = _softplus(-lam_ref[...])
    for n in range(N_RNN_BLOCKS):
        sl = slice(n * bw, (n + 1) * bw)
        h = h0_ref[:, sl]
        for t in range(n_t):
            xc = cb_ref[:, sl]
            for k in range(CONV_W):
                xc = xc + up_ref[t + k, :, sl] * cw_ref[k:k + 1, sl]
            a, b = _lru_gates(xc, wri_ref, br_ref, bi_ref, sp_scr, n, bw)
            h = a * h + b
            z = z_ref[t, :, sl]
            g_ref[t, :, sl] = (h * (z * _sigmoid(z))).astype(BF16)
        hlast_ref[:, sl] = h


def _rglru_sample(up, z, h0, cw, cb, wri, br, bi, lam):
    n_t, nb, d = z.shape
    return pl.pallas_call(
        _rglru_sample_kernel,
        out_shape=[jax.ShapeDtypeStruct((n_t, nb, d), BF16), jax.ShapeDtypeStruct((nb, d), F32)],
        scratch_shapes=[pltpu.VMEM((1, d), F32)],
        compiler_params=pltpu.CompilerParams(vmem_limit_bytes=VMEM_LIMIT),
        name="rglru_sample",
    )(up, z, h0, cw, cb.reshape(1, d), wri, br.reshape(1, d), bi.reshape(1, d), lam.reshape(1, d))


def _outproj_ple_kernel(a_ref, res_ref, p_ref, wo_ref, wp_ref, gn_ref, wg_ref, o_ref, h_scr, hn_scr):
    tm, d = res_ref.shape
    ch = min(d, 512)
    a = a_ref[...]
    ssq = jnp.zeros((tm, 1), F32)
    for c in range(d // ch):
        sl = slice(c * ch, (c + 1) * ch)
        h = res_ref[:, sl] + jnp.dot(a, wo_ref[:, sl], preferred_element_type=F32)
        h_scr[:, sl] = h
        ssq = ssq + jnp.sum(h * h, axis=-1, keepdims=True)
    inv = lax.rsqrt(ssq * (1.0 / d) + EPS)
    for c in range(d // ch):
        sl = slice(c * ch, (c + 1) * ch)
        hn_scr[:, sl] = (h_scr[:, sl] * inv * gn_ref[:, sl]).astype(BF16)
    pb = p_ref[...].astype(BF16)
    hn = hn_scr[...]
    for c in range(d // ch):
        sl = slice(c * ch, (c + 1) * ch)
        gate = _sigmoid(jnp.dot(hn, wg_ref[:, sl], preferred_element_type=F32))
        pp = jnp.dot(pb, wp_ref[:, sl], preferred_element_type=F32)
        o_ref[:, sl] = h_scr[:, sl] + pp * gate


def _outproj_ple(a, res, p, w_out, w_proj, g_norm, w_gate):
    m, d = res.shape
    pd = p.shape[1]
    tm = min(m, 512)
    return pl.pallas_call(
        _outproj_ple_kernel,
        grid=(m // tm,),
        in_specs=[pl.BlockSpec((tm, a.shape[1]), lambda i: (i, 0)),
                  pl.BlockSpec((tm, d), lambda i: (i, 0)),
                  pl.BlockSpec((tm, pd), lambda i: (i, 0)),
                  _resident(w_out.shape, lambda i: (0, 0)),
                  _resident(w_proj.shape, lambda i: (0, 0)),
                  _resident((1, d), lambda i: (0, 0)),
                  _resident(w_gate.shape, lambda i: (0, 0))],
        out_specs=pl.BlockSpec((tm, d), lambda i: (i, 0)),
        out_shape=jax.ShapeDtypeStruct((m, d), F32),
        scratch_shapes=[pltpu.VMEM((tm, d), F32), pltpu.VMEM((tm, d), BF16)],
        compiler_params=_cparams(("parallel",)),
        name="outproj_ple",
    )(a, res, p, w_out, w_proj, g_norm.reshape(1, d), w_gate)


AUG_W = 2 * HEAD_DIM
MASK_OFF = -2.0 ** 60


def _kv_proj_kernel(x_ref, g_ref, w_ref, gsel_ref, gwin_ref,
                    kc_ref, vc_ref, ks_ref, vs_ref, kw_ref, vw_ref, ksa_ref, vsa_ref, kwb_ref, vwa_ref):
    tm = x_ref.shape[0]
    xn = _rms_rows(x_ref[...], g_ref[...]).astype(BF16)
    outs = (kc_ref, vc_ref, ks_ref, vs_ref, kw_ref, vw_ref)
    head_gain = {2: gsel_ref, 4: gwin_ref}
    pos = pl.program_id(0) * tm + lax.broadcasted_iota(jnp.int32, (tm, HEAD_DIM), 0)
    blk = lax.broadcasted_iota(jnp.int32, (tm, HEAD_DIM), 1)
    block_mask = jnp.where(blk == pos // SEL_BLOCK, MASK_OFF, 0.0).astype(BF16)
    ones = jnp.ones((tm, HEAD_DIM), BF16)
    for s in range(6):
        y = jnp.dot(xn, w_ref[:, s * KV_W:(s + 1) * KV_W], preferred_element_type=F32)
        for g in range(N_KV):
            yg = y[:, g * HEAD_DIM:(g + 1) * HEAD_DIM]
            if s in head_gain:
                yg = _rms_rows(yg, head_gain[s][...])
            outs[s][pl.ds(g, tm, stride=N_KV), :] = yg
            lo = slice(g * AUG_W, g * AUG_W + HEAD_DIM)
            hi = slice(g * AUG_W + HEAD_DIM, (g + 1) * AUG_W)
            if s == 2:
                ksa_ref[:, lo] = yg.astype(BF16)
                ksa_ref[:, hi] = block_mask
            elif s == 3:
                vsa_ref[:, lo] = yg.astype(BF16)
                vsa_ref[:, hi] = ones
            elif s == 4:
                kwb_ref[:, g * HEAD_DIM:(g + 1) * HEAD_DIM] = yg.astype(BF16)
            elif s == 5:
                vwa_ref[:, lo] = yg.astype(BF16)
                vwa_ref[:, hi] = ones


def _kv_proj(x, g, w, g_sel, g_win):
    m, d = x.shape
    tm = min(m, 512)
    row = lambda i: (i, 0)
    aug = N_KV * AUG_W
    return pl.pallas_call(
        _kv_proj_kernel,
        grid=(m // tm,),
        in_specs=[pl.BlockSpec((tm, d), row), _resident((1, d), lambda i: (0, 0)),
                  _resident(w.shape, lambda i: (0, 0)),
                  _resident((1, HEAD_DIM), lambda i: (0, 0)), _resident((1, HEAD_DIM), lambda i: (0, 0))],
        out_specs=([pl.BlockSpec((tm * N_KV, HEAD_DIM), row)] * 6
                   + [pl.BlockSpec((tm, aug), row), pl.BlockSpec((tm, aug), row),
                      pl.BlockSpec((tm, KV_W), row), pl.BlockSpec((tm, aug), row)]),
        out_shape=([jax.ShapeDtypeStruct((m * N_KV, HEAD_DIM), F32)] * 6
                   + [jax.ShapeDtypeStruct((m, aug), BF16), jax.ShapeDtypeStruct((m, aug), BF16),
                      jax.ShapeDtypeStruct((m, KV_W), BF16), jax.ShapeDtypeStruct((m, aug), BF16)]),
        compiler_params=_cparams(("parallel",)),
        name="kv_proj",
    )(x, g.reshape(1, d), w, g_sel.reshape(1, HEAD_DIM), g_win.reshape(1, HEAD_DIM))


PIECE_ROWS = CMP_STRIDE * N_KV
PIECE_PITCH = PIECE_ROWS + SUBLANES


def _piece_rows(buf, r, n):
    flat = buf.reshape(buf.shape[0] * PIECE_PITCH, HEAD_DIM)
    return flat[pl.ds(r, n, stride=PIECE_PITCH), :]


def _compress_kernel(pt_ref, kc_hbm, vc_hbm, pe_ref, w1_ref, b1_ref, w2_ref, b2_ref, gk_ref,
                     kcmp_ref, vcmp_ref, kbuf, vbuf, sem, lhs, *, n_quarters, pages_per_step):
    s = pl.program_id(0)
    n_steps = pl.num_programs(0)
    pieces_per_page = kc_hbm.shape[1]
    n_piece = pages_per_step * pieces_per_page
    n_load = n_piece + SUBLANES
    rows_g = n_piece + 2 * SUBLANES
    hidden = w2_ref.shape[1]

    def copies(step, slot):
        b = step // n_quarters
        q = step % n_quarters
        out = []
        for hbm, buf, which in ((kc_hbm, kbuf, 0), (vc_hbm, vbuf, 1)):
            for p in range(pages_per_step):
                pg = pt_ref[b, q * pages_per_step + p]
                dst = buf.at[slot, pl.ds(p * pieces_per_page, pieces_per_page), pl.ds(0, PIECE_ROWS), :]
                out.append(pltpu.make_async_copy(hbm.at[pg], dst, sem.at[slot, which]))
            pg = pt_ref[b, q * pages_per_step + pages_per_step]
            out.append(pltpu.make_async_copy(hbm.at[pg, 0], buf.at[slot, n_piece, pl.ds(0, PIECE_ROWS), :],
                                             sem.at[slot, which]))
        return out

    slot = s % 2

    @pl.when(s == 0)
    def _():
        kbuf[...] = jnp.zeros(kbuf.shape, F32)
        vbuf[...] = jnp.zeros(vbuf.shape, F32)
        lhs[...] = jnp.zeros(lhs.shape, BF16)
        lhs[:, N_KV * rows_g:, :] = pe_ref[...]
        for c in copies(0, 0):
            c.start()

    @pl.when(s + 1 < n_steps)
    def _():
        for c in copies(s + 1, 1 - slot):
            c.start()

    for c in copies(s, slot):
        c.wait()

    for which, buf, out_ref in ((0, kbuf, kcmp_ref), (1, vbuf, vcmp_ref)):
        for g in range(N_KV):
            for pos in range(CMP_STRIDE):
                x = _piece_rows(buf.at[slot], pos * N_KV + g, n_load)
                lhs[which, g * rows_g:g * rows_g + n_load, pos * HEAD_DIM:(pos + 1) * HEAD_DIM] = x.astype(BF16)
        hab = jnp.dot(lhs[which], w1_ref[which], preferred_element_type=F32)
        pe0 = N_KV * rows_g
        bias = (jnp.sum(hab[pe0:pe0 + SUBLANES, :hidden], axis=0, keepdims=True)
                + jnp.sum(hab[pe0 + SUBLANES:pe0 + 2 * SUBLANES, hidden:], axis=0, keepdims=True) + b1_ref[which])
        for g in range(N_KV):
            blk = hab[g * rows_g:(g + 1) * rows_g]
            nxt = pltpu.roll(blk[:, hidden:], rows_g - 1, axis=0)
            hid = blk[:n_piece, :hidden] + nxt[:n_piece] + bias
            hid = hid * _sigmoid(hid)
            y = jnp.dot(hid.astype(BF16), w2_ref[which], preferred_element_type=F32) + b2_ref[which]
            if which == 0:
                y = _rms_rows(y, gk_ref[...])
            out_ref[0, g] = y.astype(BF16)


def _compress(kc_pages, vc_pages, table, pe_rows, w1, b1, w2, b2, g_k):
    nb, n_pages = table.shape[0], table.shape[1] - 1
    ppp = kc_pages.shape[1]
    pps = min(CMP_PAGES_PER_STEP, n_pages)
    n_quarters = n_pages // pps
    n_piece = pps * ppp
    rows_g = n_piece + 2 * SUBLANES
    kdim = CMP_STRIDE * HEAD_DIM
    full = lambda shape: pl.BlockSpec(shape, lambda s, pt: (0,) * len(shape))
    out_spec = pl.BlockSpec((1, N_KV, n_piece, HEAD_DIM), lambda s, pt: (s // n_quarters, 0, s % n_quarters, 0))
    out_sd = jax.ShapeDtypeStruct((nb, N_KV, n_pages * ppp, HEAD_DIM), BF16)
    buf = pltpu.VMEM((2, n_piece + SUBLANES, PIECE_PITCH, HEAD_DIM), F32)
    grid_spec = pltpu.PrefetchScalarGridSpec(
        num_scalar_prefetch=1,
        grid=(nb * n_quarters,),
        in_specs=[pl.BlockSpec(memory_space=pl.ANY), pl.BlockSpec(memory_space=pl.ANY),
                  full(pe_rows.shape), full(w1.shape), full(b1.shape), full(w2.shape), full(b2.shape),
                  full((1, HEAD_DIM))],
        out_specs=[out_spec, out_spec],
        scratch_shapes=[buf, buf, pltpu.SemaphoreType.DMA((2, 2)),
                        pltpu.VMEM((2, N_KV * rows_g + 2 * SUBLANES, kdim), BF16)],
    )
    return pl.pallas_call(
        functools.partial(_compress_kernel, n_quarters=n_quarters, pages_per_step=pps),
        grid_spec=grid_spec,
        out_shape=[out_sd, out_sd],
        compiler_params=_cparams(("arbitrary",)),
        name="compress",
    )(table, kc_pages, vc_pages, pe_rows, w1, b1, w2, b2, g_k.reshape(1, HEAD_DIM))


def _rel_bucket(dist):
    n = jnp.maximum(dist, 0)
    max_exact = N_BUCKETS // 2
    nf = jnp.maximum(n, 1).astype(F32)
    large = max_exact + (jnp.log(nf / max_exact) / math.log(MAX_DISTANCE / max_exact)
                         * (N_BUCKETS - max_exact)).astype(jnp.int32)
    large = jnp.minimum(large, N_BUCKETS - 1)
    return jnp.where(n < max_exact, n, large)


def _bias_by_distance(rel_bias, n_dist):
    onehot = (_rel_bucket(jnp.arange(n_dist))[None, :] == jnp.arange(N_BUCKETS)[:, None]).astype(F32)
    return jnp.dot(rel_bias.astype(F32).T, onehot, precision=lax.Precision.HIGHEST)


def _left_pad(fd, pad, masked=True):
    first = jnp.full((fd.shape[0], pad), NEG, F32) if masked else jnp.broadcast_to(fd[:, :1], (fd.shape[0], pad))
    return jnp.concatenate([first, fd], axis=1)


def _toeplitz_bank(fd, n_rows, bank_k, width):
    fdp = _left_pad(fd, width)
    rev = fdp[:, ::-1]
    top = fdp.shape[1] - 1 - bank_k - width
    return jnp.stack([rev[:, top - i:top - i + width] for i in range(n_rows)], axis=1)


def _cmp_bank(fd, n_qt, n_row_blocks):
    n_m = n_qt + n_row_blocks - 1
    pad = TQ * n_row_blocks + 2 * LANES
    fdp = _left_pad(fd, pad)
    planes = []
    for il in range(SUBLANES):
        base = pad - CMP_STRIDE * il - (CMP_BLOCK - 1) - TQ * (n_row_blocks - 1)
        planes.append(fdp[:, base:base + TQ * n_m].reshape(fd.shape[0], n_m, TQ)[:, ::-1])
    return jnp.stack(planes, axis=2).reshape(fd.shape[0], n_m * SUBLANES, TQ)


def _query_rows_bias(fd, t0, n_tok, n_keys):
    pad = n_keys
    fdp = _left_pad(fd, pad, masked=False)
    rev = fdp[:, ::-1]
    last = fdp.shape[1] - 1
    return jnp.stack([rev[:, last - (t0 + tok + pad):last - (t0 + tok + pad) + n_keys] for tok in range(n_tok)],
                     axis=1)


def _first_constant_distance():
    ratio = MAX_DISTANCE / (N_BUCKETS // 2)
    return int(math.ceil((N_BUCKETS // 2) * ratio ** ((N_BUCKETS // 2 - 1) / (N_BUCKETS // 2)))) + 2


def _topk_rows(score, blk, k_top):
    n_blk = score.shape[0]
    sel = jnp.zeros(score.shape, jnp.bool_)
    for _ in range(k_top):
        m = jnp.max(score, axis=0, keepdims=True)
        idx = jnp.min(jnp.where(score == m, blk, n_blk), axis=0, keepdims=True)
        pick = blk == idx
        sel = sel | pick
        score = jnp.where(pick, REMOVED, score)
    return sel


def _attn_prompt_kernel(q_ref, z0_ref, z1_ref, z2_ref, gl_ref, gq_ref, kcmp_ref, vcmp_ref,
                        ksa_ref, vsa_ref, kw_ref, vwa_ref, rr_ref, sb_ref, wb_ref, o_ref,
                        q_scr, psum_scr, m_scr, acc_scr, s0_scr, s1_scr, *, n_qtiles, bank_k):
    qi = pl.program_id(1)
    q0 = qi * TQ
    n_cmp_rows = kcmp_ref.shape[1]
    n_sb = n_cmp_rows * CMP_STRIDE // SEL_BLOCK
    hg = HEADS_PER_GROUP
    q_gain = HEAD_DIM ** -0.5 * LOG2E

    for h in range(hg):
        qh = _rms_rows(q_ref[:, h * HEAD_DIM:(h + 1) * HEAD_DIM], gq_ref[...]) * q_gain
        q_scr[h * TQ:(h + 1) * TQ, 0:HEAD_DIM] = qh.astype(BF16)
    q_all = q_scr[:, 0:HEAD_DIM]

    t_row = q0 + lax.broadcasted_iota(jnp.int32, (1, TQ), 1)


    s_t = lax.dot_general(kcmp_ref[0], q_all, (((1,), (1,)), ((), ())), preferred_element_type=F32)
    rr_start = pl.multiple_of((n_qtiles - 1 - qi) * SUBLANES, SUBLANES)
    psum = jnp.zeros((n_cmp_rows, TQ), F32)
    o_cmp = []
    for h in range(hg):
        sh = s_t[:, h * TQ:(h + 1) * TQ] + rr_ref[h, pl.ds(rr_start, n_cmp_rows), :]
        mx = jnp.maximum(jnp.max(sh, axis=0, keepdims=True), 0.5 * NEG)
        e = jnp.exp2(sh - mx)
        den = jnp.sum(e, axis=0, keepdims=True)
        p = e * (1.0 / jnp.where(den > 0.0, den, 1.0))
        psum = psum + p
        o_cmp.append(lax.dot_general(p.astype(BF16), vcmp_ref[0], (((0,), (0,)), ((), ())),
                                     preferred_element_type=F32))

    wk = WINDOW + TQ
    w0 = pl.multiple_of(jnp.maximum(q0 - WINDOW, 0), LANES)
    s_w = lax.dot_general(q_all, kw_ref[pl.ds(w0, wk), :], (((1,), (1,)), ((), ())), preferred_element_type=F32)
    vwin = vwa_ref[pl.ds(w0, wk), :]
    wb_start = pl.multiple_of(WINDOW - (q0 - w0), LANES)
    o_win = []
    for h in range(hg):
        sh = s_w[h * TQ:(h + 1) * TQ] + wb_ref[h, :, pl.ds(wb_start, wk)]
        e = jnp.exp2(sh - jnp.max(sh, axis=1, keepdims=True))
        o_aug = jnp.dot(e.astype(BF16), vwin, preferred_element_type=F32)
        o_win.append(o_aug[:, 0:HEAD_DIM] / o_aug[:, HEAD_DIM:AUG_W])

    psum_scr[0:SUBLANES, :] = jnp.zeros((SUBLANES, TQ), F32)
    psum_scr[SUBLANES:, :] = psum
    ratio = SEL_BLOCK // CMP_STRIDE
    imp = psum_scr[pl.ds(SUBLANES - 1, n_sb, stride=ratio), :]
    for k in range(ratio):
        imp = imp + psum_scr[pl.ds(SUBLANES + k, n_sb, stride=ratio), :]
    blk = lax.broadcasted_iota(jnp.int32, (n_sb, 1), 0)
    cur = t_row // SEL_BLOCK
    blk_ok = blk <= cur
    forced = (blk == 0) | (blk == cur) | (blk == cur - 1)
    score = jnp.where(blk_ok, jnp.where(forced, FORCE, imp), -FORCE)
    sel_t = _topk_rows(score, blk, min(N_SELECT, n_sb)) & blk_ok
    not_sel = jnp.where(sel_t, 0.0, 1.0).T
    if n_sb < HEAD_DIM:
        not_sel = jnp.concatenate([not_sel, jnp.zeros((TQ, HEAD_DIM - n_sb), F32)], axis=1)
    for h in range(hg):
        q_scr[h * TQ:(h + 1) * TQ, HEAD_DIM:AUG_W] = not_sel.astype(BF16)
    q_aug = q_scr[...]

    m_scr[...] = jnp.full(m_scr.shape, NEG, F32)
    acc_scr[...] = jnp.zeros(acc_scr.shape, F32)
    n_kv = (q0 + TQ + TK_SEL - 1) // TK_SEL
    rep = TK_SEL // LANES

    def scores(j, s_ref):
        k0 = pl.multiple_of(j * TK_SEL, TK_SEL)
        s_ref[...] = lax.dot_general(q_aug, ksa_ref[pl.ds(k0, TK_SEL), :], (((1,), (1,)), ((), ())),
                                     preferred_element_type=F32)

    def kv_tile(j, s_ref):
        k0 = pl.multiple_of(j * TK_SEL, TK_SEL)
        vt = vsa_ref[pl.ds(k0, TK_SEL), :]
        b_start = pl.multiple_of(jnp.maximum(bank_k - (q0 - k0), 0), LANES)
        for h in range(hg):
            rows = slice(h * TQ, (h + 1) * TQ)
            sh = s_ref[rows] + sb_ref[h, :, pl.ds(b_start, TK_SEL)]
            m_prev = m_scr[rows]
            m_next = jnp.maximum(m_prev, jnp.max(sh, axis=1, keepdims=True))
            alpha = jnp.exp2(m_prev - m_next)
            e = jnp.exp2(sh - jnp.tile(m_next, (1, rep)))
            acc_scr[rows] = (jnp.tile(alpha, (1, AUG_W // LANES)) * acc_scr[rows]
                             + jnp.dot(e.astype(BF16), vt, preferred_element_type=F32))
            m_scr[rows] = m_next

    def run(first, count):
        bufs = (s0_scr, s1_scr)
        for k in range(count):
            scores(first + k + 1, bufs[(k + 1) % 2])
            kv_tile(first + k, bufs[k % 2])

    def kv_quad(i, carry):
        run(4 * i, 4)
        return carry

    scores(0, s0_scr)
    n_before = n_kv - 1
    lax.fori_loop(0, n_before // 4, kv_quad, 0)
    done4 = n_before // 4 * 4

    @pl.when(n_before - done4 >= 2)
    def _():
        run(done4, 2)

    done2 = n_before // 2 * 2

    @pl.when(n_before > done2)
    def _():
        scores(done2 + 1, s1_scr)
        kv_tile(done2, s0_scr)
        kv_tile(done2 + 1, s1_scr)

    @pl.when(n_before == done2)
    def _():
        kv_tile(done2, s0_scr)

    sig_gl = _sigmoid(gl_ref[0])
    z_refs = (z0_ref, z1_ref, z2_ref)
    for h in range(hg):
        rows = slice(h * TQ, (h + 1) * TQ)
        cols = slice(h * HEAD_DIM, (h + 1) * HEAD_DIM)
        o_sel = acc_scr[rows, 0:HEAD_DIM] / acc_scr[rows, HEAD_DIM:AUG_W]
        out = jnp.zeros((TQ, HEAD_DIM), F32)
        for br, o_br in enumerate((o_cmp[h], o_sel, o_win[h])):
            z = z_refs[br][:, cols]
            c = br * hg + h
            out = out + sig_gl[:, c:c + 1] * o_br * (z * _sigmoid(z))
        o_ref[:, cols] = out.astype(BF16)


def _attn_prompt(proj, gl_g, g_q, kcmp, vcmp, ks_aug, vs_aug, kw_b, vw_aug, rr, sb, wb, bank_k):
    s_len = proj.shape[0]
    n_qt = s_len // TQ
    n_cmp_rows = kcmp.shape[2]
    gcols = N_HEADS * HEAD_DIM // GROUP_W
    qz = lambda seg: pl.BlockSpec((TQ, GROUP_W), lambda g, i, seg=seg: (i, seg * gcols + g))
    per_group = lambda shape: _resident((1,) + shape, lambda g, i: (g,) + (0,) * len(shape))
    kv_spec = lambda width: _resident((s_len, width), lambda g, i: (0, g))
    return pl.pallas_call(
        functools.partial(_attn_prompt_kernel, n_qtiles=n_qt, bank_k=bank_k),
        grid=(N_KV, n_qt),
        in_specs=[qz(0), qz(1), qz(2), qz(3),
                  pl.BlockSpec((1, TQ, LANES), lambda g, i: (g, i, 0)),
                  _resident((1, HEAD_DIM), lambda g, i: (0, 0)),
                  per_group((n_cmp_rows, HEAD_DIM)), per_group((n_cmp_rows, HEAD_DIM)),
                  kv_spec(AUG_W), kv_spec(AUG_W), kv_spec(HEAD_DIM), kv_spec(AUG_W),
                  _resident((HEADS_PER_GROUP,) + rr.shape[1:], lambda g, i: (g, 0, 0)),
                  _resident((HEADS_PER_GROUP,) + sb.shape[1:], lambda g, i: (g, 0, 0)),
                  _resident((HEADS_PER_GROUP,) + wb.shape[1:], lambda g, i: (g, 0, 0))],
        out_specs=pl.BlockSpec((TQ, GROUP_W), lambda g, i: (i, g)),
        out_shape=jax.ShapeDtypeStruct((s_len, N_HEADS * HEAD_DIM), BF16),
        scratch_shapes=[pltpu.VMEM((HEADS_PER_GROUP * TQ, AUG_W), BF16),
                        pltpu.VMEM((n_cmp_rows + SUBLANES, TQ), F32),
                        pltpu.VMEM((HEADS_PER_GROUP * TQ, LANES), F32),
                        pltpu.VMEM((HEADS_PER_GROUP * TQ, AUG_W), F32),
                        pltpu.VMEM((HEADS_PER_GROUP * TQ, TK_SEL), F32),
                        pltpu.VMEM((HEADS_PER_GROUP * TQ, TK_SEL), F32)],
        compiler_params=_cparams(("arbitrary", "arbitrary")),
        name="attn_prompt",
    )(proj, proj, proj, proj, gl_g, g_q.reshape(1, HEAD_DIM), kcmp[0], vcmp[0],
      ks_aug, vs_aug, kw_b, vw_aug, rr, sb, wb)


def _attn_sample_kernel(*refs, n_tok, pages_per_chunk, page, past_len):
    pt_ref = refs[0]
    q_ref, z_ref, gl_ref, gq_ref, kcmp_ref, vcmp_ref = refs[1:7]
    kpages = refs[7:7 + pages_per_chunk]
    vpages = refs[7 + pages_per_chunk:7 + 2 * pages_per_chunk]
    (ksn_ref, vsn_ref, kwn_ref, vwn_ref, kwc_ref, vwc_ref, bcmp_ref, bsel_ref, bwin_ref, ov_ref, ex_ref,
     o_ref, sel_scr, m_scr, l_scr, acc_scr, q_scr, ocmp_scr) = refs[7 + 2 * pages_per_chunk:]
    del pt_ref
    c = pl.program_id(1)
    n_chunks = pl.num_programs(1)
    hg = HEADS_PER_GROUP
    nr = hg * n_tok
    scale = HEAD_DIM ** -0.5
    n_cmp_rows = kcmp_ref.shape[2]
    n_sb = ov_ref.shape[1]
    chunk = pages_per_chunk * page
    tok = lax.broadcasted_iota(jnp.int32, (nr, 1), 0) % n_tok

    @pl.when(c == 0)
    def _():
        q_scr[...] = _rms_rows(q_ref[0], gq_ref[...]).astype(BF16)
        m_scr[...] = jnp.full(m_scr.shape, NEG, F32)
        l_scr[...] = jnp.zeros(l_scr.shape, F32)
        acc_scr[...] = jnp.zeros(acc_scr.shape, F32)
        ci = lax.broadcasted_iota(jnp.int32, (1, n_cmp_rows), 1)
        cmp_valid = ci < n_cmp_rows - 1
        for g in range(N_KV):
            rows = slice(g * nr, (g + 1) * nr)
            s = lax.dot_general(q_scr[rows], kcmp_ref[0, g], (((1,), (1,)), ((), ())),
                                preferred_element_type=F32)
            s = jnp.where(cmp_valid, s * scale + bcmp_ref[g], NEG)
            e = jnp.where(cmp_valid, jnp.exp(s - jnp.max(s, axis=1, keepdims=True)), 0.0)
            p = e / jnp.sum(e, axis=1, keepdims=True)
            ocmp_scr[rows] = jnp.dot(p.astype(BF16), vcmp_ref[0, g], preferred_element_type=F32)
            psum = p[0:n_tok]
            for h in range(1, hg):
                psum = psum + p[h * n_tok:(h + 1) * n_tok]
            imp = jnp.dot(psum, ov_ref[...], preferred_element_type=F32, precision=lax.Precision.HIGHEST)
            blk = lax.broadcasted_iota(jnp.int32, (1, n_sb), 1)
            forced = (blk == 0) | (blk == n_sb - 1)
            score = jnp.where(forced, FORCE, imp)
            sel = jnp.zeros(score.shape, jnp.bool_)
            for _ in range(min(N_SELECT, n_sb + 1) - 1):
                mx = jnp.max(score, axis=1, keepdims=True)
                idx = jnp.min(jnp.where(score == mx, blk, n_sb), axis=1, keepdims=True)
                pick = blk == idx
                sel = sel | pick
                score = jnp.where(pick, REMOVED, score)
            sel_scr[rows] = jnp.tile(jnp.where(sel, 1.0, 0.0), (hg, 1)).astype(BF16)

    def online_update(rows, sh, v):
        m_prev = m_scr[rows]
        m_next = jnp.maximum(m_prev, jnp.max(sh, axis=1, keepdims=True))
        alpha = jnp.exp(m_prev - m_next)
        e = jnp.exp(sh - m_next[:, 0:1])
        l_scr[rows] = alpha * l_scr[rows] + jnp.sum(e, axis=1, keepdims=True)
        acc_scr[rows] = alpha * acc_scr[rows] + jnp.dot(e.astype(BF16), v, preferred_element_type=F32)
        m_scr[rows] = m_next

    def group_rows(ref, g):
        return ref[0, pl.ds(g, ref.shape[1] // N_KV, stride=N_KV), :].astype(BF16)

    k0 = pl.multiple_of(c * chunk, chunk)
    for g in range(N_KV):
        rows = slice(g * nr, (g + 1) * nr)
        kt = jnp.concatenate([group_rows(r, g) for r in kpages], axis=0)
        vt = jnp.concatenate([group_rows(r, g) for r in vpages], axis=0)
        s = lax.dot_general(q_scr[rows], kt, (((1,), (1,)), ((), ())), preferred_element_type=F32)
        picked = jnp.dot(sel_scr[rows], ex_ref[:, pl.ds(k0, chunk)], preferred_element_type=F32)
        sh = jnp.where(picked > 0.5, s * scale + bsel_ref[g, :, pl.ds(k0, chunk)], NEG)
        online_update(rows, sh, vt)

    @pl.when(c == n_chunks - 1)
    def _():
        new_w = ksn_ref.shape[1] // N_KV
        jn = lax.broadcasted_iota(jnp.int32, (1, new_w), 1)
        new_valid = jn <= tok
        wbuf = kwc_ref.shape[1] // N_KV
        jw = lax.broadcasted_iota(jnp.int32, (1, wbuf), 1)
        dist_c = (past_len + tok) - (past_len - wbuf + jw)
        win_valid_c = dist_c < WINDOW
        sig_gl = _sigmoid(gl_ref[0])
        for g in range(N_KV):
            rows = slice(g * nr, (g + 1) * nr)
            qg = q_scr[rows]
            s = lax.dot_general(qg, group_rows(ksn_ref, g), (((1,), (1,)), ((), ())),
                                preferred_element_type=F32)
            sh = jnp.where(new_valid, s * scale + bsel_ref[g, :, pl.ds(past_len, new_w)], NEG)
            online_update(rows, sh, group_rows(vsn_ref, g))
            o_sel = acc_scr[rows] / l_scr[rows]
            s_c = lax.dot_general(qg, group_rows(kwc_ref, g), (((1,), (1,)), ((), ())),
                                  preferred_element_type=F32)
            s_n = lax.dot_general(qg, group_rows(kwn_ref, g), (((1,), (1,)), ((), ())),
                                  preferred_element_type=F32)
            s_c = jnp.where(win_valid_c, s_c * scale + bwin_ref[g, :, 0:wbuf], NEG)
            s_n = jnp.where(new_valid, s_n * scale + bwin_ref[g, :, wbuf:wbuf + new_w], NEG)
            mx = jnp.maximum(jnp.max(s_c, axis=1, keepdims=True), jnp.max(s_n, axis=1, keepdims=True))
            e_c = jnp.exp(s_c - mx)
            e_n = jnp.exp(s_n - mx)
            den = jnp.sum(e_c, axis=1, keepdims=True) + jnp.sum(e_n, axis=1, keepdims=True)
            o_win = (jnp.dot(e_c.astype(BF16), group_rows(vwc_ref, g), preferred_element_type=F32)
                     + jnp.dot(e_n.astype(BF16), group_rows(vwn_ref, g), preferred_element_type=F32)) / den
            out = jnp.zeros((nr, HEAD_DIM), F32)
            for br, o_br in enumerate((ocmp_scr[rows], o_sel, o_win)):
                z = z_ref[0, br, rows]
                out = out + sig_gl[br, rows] * o_br * (z * _sigmoid(z))
            o_ref[0, rows] = out.astype(BF16)


def _attn_sample(q_s, z_s, gl_s, g_q, kcmp, vcmp, ksel_pages, vsel_pages, page_table, ks_new, vs_new,
                 kw_new, vw_new, kw_cache, vw_cache, bcmp, bsel, bwin, overlap, expand, n_tok):
    nb = q_s.shape[0]
    n_pages = page_table.shape[1]
    page = ksel_pages.shape[1] // N_KV
    ppc = min(8, n_pages)
    n_chunks = n_pages // ppc
    nrow = q_s.shape[1]
    per_b = lambda shape: pl.BlockSpec((1,) + shape, lambda b, c, pt: (b,) + (0,) * len(shape))
    full = lambda shape: _resident(shape, lambda b, c, pt: (0,) * len(shape))
    page_spec = lambda i: pl.BlockSpec((1, page * N_KV, HEAD_DIM),
                                       lambda b, c, pt, i=i: (pt[b, c * ppc + i], 0, 0))
    in_specs = ([per_b(q_s.shape[1:]), per_b(z_s.shape[1:]), per_b(gl_s.shape[1:]), full((1, HEAD_DIM)),
                 per_b(kcmp.shape[1:]), per_b(vcmp.shape[1:])]
                + [page_spec(i) for i in range(ppc)] + [page_spec(i) for i in range(ppc)]
                + [per_b(ks_new.shape[1:])] * 4 + [per_b(kw_cache.shape[1:])] * 2
                + [full(bcmp.shape), full(bsel.shape), full(bwin.shape), full(overlap.shape), full(expand.shape)])
    grid_spec = pltpu.PrefetchScalarGridSpec(
        num_scalar_prefetch=1,
        grid=(nb, n_chunks),
        in_specs=in_specs,
        out_specs=pl.BlockSpec((1, nrow, HEAD_DIM), lambda b, c, pt: (b, 0, 0)),
        scratch_shapes=[pltpu.VMEM((nrow, expand.shape[0]), BF16),
                        pltpu.VMEM((nrow, LANES), F32), pltpu.VMEM((nrow, LANES), F32),
                        pltpu.VMEM((nrow, HEAD_DIM), F32), pltpu.VMEM((nrow, HEAD_DIM), BF16),
                        pltpu.VMEM((nrow, HEAD_DIM), F32)],
    )
    return pl.pallas_call(
        functools.partial(_attn_sample_kernel, n_tok=n_tok, pages_per_chunk=ppc, page=page,
                          past_len=n_pages * page),
        grid_spec=grid_spec,
        out_shape=jax.ShapeDtypeStruct((nb, nrow, HEAD_DIM), BF16),
        compiler_params=_cparams(("arbitrary", "arbitrary")),
        name="attn_sample",
    )(page_table, q_s, z_s, gl_s, g_q.reshape(1, HEAD_DIM), kcmp, vcmp,
      *([ksel_pages] * ppc), *([vsel_pages] * ppc), ks_new, vs_new, kw_new, vw_new, kw_cache, vw_cache,
      bcmp, bsel, bwin, overlap, expand)


def kernel(x_prompt, x_sample, p_prompt, p_sample, state_conv, state_h, cache_k_cmp, cache_v_cmp, cache_k_sel, cache_v_sel, cache_k_win, cache_v_win, page_table, norm_g, a_w_in, a_conv_w, a_conv_b, a_w_r, a_b_r, a_w_i, a_b_i, a_lam, a_w_out, kv_norm_g, w_kv, g_k_cmp, g_k_sel, g_k_win, cmp_pos, cmp_w1a, cmp_w1b, cmp_b1, cmp_w2, cmp_b2, b_w_in, b_g_q, b_w_out, rel_bias, ple_w_proj, ple_norm_g, ple_w_gate):
    bp, s_len, d = x_prompt.shape
    db, ds, _ = x_sample.shape
    n_pool, page = cache_k_sel.shape[:2]
    n_pages = page_table.shape[1]
    past_len = n_pages * page
    wbuf = cache_k_win.shape[1]
    hd = N_HEADS * HEAD_DIM
    assert bp == 1 and norm_g.shape[0] == 2 and a_w_in.shape[0] == 1 and b_w_in.shape[0] == 1
    assert d == hd and s_len % TK_SEL == 0 and s_len >= WINDOW + TQ and page % SEL_BLOCK == 0
    assert wbuf == WINDOW and ds <= SEL_BLOCK and past_len % TK_SEL == 0 and s_len // SEL_BLOCK <= HEAD_DIM

    w_in0 = a_w_in[0].astype(BF16)
    w_ri = jnp.concatenate([a_w_r[0], a_w_i[0]], axis=-1).astype(BF16)
    w_out0 = a_w_out[0].astype(BF16)
    w_gate = ple_w_gate.astype(BF16)
    w_proj = ple_w_proj.astype(BF16)
    w_kv_b = w_kv.astype(BF16)
    n_in1 = b_w_in.shape[2]
    n_in1_pad = -(-n_in1 // 640) * 640
    w_in1 = jnp.pad(b_w_in[0], ((0, 0), (0, n_in1_pad - n_in1))).astype(BF16)
    w_out1 = b_w_out[0].astype(BF16)
    w1 = jnp.concatenate([cmp_w1a, cmp_w1b], axis=-1).astype(BF16)
    w2 = cmp_w2.astype(BF16)
    pe = cmp_pos.reshape(2, 2, 1, CMP_STRIDE * HEAD_DIM)
    pe_hi = pe.astype(BF16)
    pe_lo = (pe - pe_hi.astype(F32)).astype(BF16)
    pe_rows = jnp.concatenate([pe_hi, pe_lo, jnp.zeros((2, 2, SUBLANES - 2, pe.shape[-1]), BF16)], axis=2)
    pe_rows = pe_rows.reshape(2, 2 * SUBLANES, pe.shape[-1])
    b1 = cmp_b1.reshape(2, 1, -1)
    b2 = cmp_b2.reshape(2, 1, HEAD_DIM)

    fd = _bias_by_distance(rel_bias, past_len + 2 * LANES + s_len)

    xp = x_prompt.reshape(s_len, d)
    xs = x_sample.reshape(db * ds, d)
    uz_p = _norm_matmul(xp, norm_g[0], w_in0, 1024)
    uz_s = _norm_matmul(xs, norm_g[0], w_in0, 1024)
    rg = (a_conv_w[0], a_conv_b[0], w_ri, a_b_r[0], a_b_i[0], a_lam[0])
    gated_p, h_p = _rglru_prompt(uz_p, jnp.zeros((SUBLANES, d), F32), jnp.zeros((1, d), F32), *rg)
    conv_p = uz_p[s_len - (CONV_W - 1):, :d].reshape(1, bp, CONV_W - 1, d)
    u_s = uz_s[:, :d].reshape(db, ds, d).transpose(1, 0, 2)
    z_s = uz_s[:, d:].reshape(db, ds, d).transpose(1, 0, 2)
    up_s = jnp.concatenate([state_conv[0].transpose(1, 0, 2), u_s], axis=0)
    gated_s, h_s = _rglru_sample(up_s, z_s, state_h[0], *rg)
    conv_s = up_s[ds:].transpose(1, 0, 2)[None]
    gated_s = gated_s.transpose(1, 0, 2).reshape(db * ds, d)
    ple0 = (w_proj[0], ple_norm_g[0], w_gate[0])
    h1_p = _outproj_ple(gated_p, xp, p_prompt[0].reshape(s_len, -1), w_out0, *ple0)
    h1_s = _outproj_ple(gated_s, xs, p_sample[0].reshape(db * ds, -1), w_out0, *ple0)

    kv_p = _kv_proj(h1_p, kv_norm_g, w_kv_b, g_k_sel, g_k_win)
    kv_s = _kv_proj(h1_s, kv_norm_g, w_kv_b, g_k_sel, g_k_win)
    kc_p, vc_p, ks_p, vs_p, kw_p, vw_p = kv_p[:6]
    kc_s, vc_s, ks_s, vs_s, kw_s, vw_s = kv_s[:6]

    p_pages = s_len // page
    ident = jnp.minimum(jnp.arange(p_pages + 1, dtype=jnp.int32), p_pages - 1)[None]
    cmp_w = (pe_rows, w1, b1, w2, b2, g_k_cmp)
    pieces = lambda x: x.reshape(-1, page // CMP_STRIDE, PIECE_ROWS, HEAD_DIM)
    kcmp_p, vcmp_p = _compress(pieces(kc_p), pieces(vc_p), ident, *cmp_w)
    table_ext = jnp.concatenate([page_table, page_table[:, -1:]], axis=1)
    kcmp_s, vcmp_s = _compress(pieces(cache_k_cmp), pieces(cache_v_cmp), table_ext, *cmp_w)

    proj_p = _norm_matmul(h1_p, norm_g[1], w_in1, 640)
    proj_s = _norm_matmul(h1_s, norm_g[1], w_in1, 640)

    def gate_logits(proj, n_rows):
        gl = proj[:, 4 * hd:4 * hd + 3 * N_HEADS].reshape(n_rows, 3, N_KV, HEADS_PER_GROUP)
        return gl.transpose(2, 0, 1, 3).reshape(N_KV, n_rows, 3 * HEADS_PER_GROUP)

    n_qt = s_len // TQ
    n_cmp_rows = s_len // CMP_STRIDE
    fd2 = fd * LOG2E
    rr = _cmp_bank(fd2, n_qt, n_cmp_rows // SUBLANES)
    bank_w = max(TK_SEL, WINDOW + TQ)
    bank_k = min(-(-(_first_constant_distance() + bank_w) // LANES) * LANES, s_len)
    sb = _toeplitz_bank(fd2, TQ, bank_k, bank_k + bank_w)
    fd_win = jnp.where(jnp.arange(fd2.shape[1])[None, :] < WINDOW, fd2, NEG)
    wb = _toeplitz_bank(fd_win, TQ, WINDOW, 2 * WINDOW + TQ)
    gl_p = jnp.pad(gate_logits(proj_p, s_len), ((0, 0), (0, 0), (0, LANES - 3 * HEADS_PER_GROUP)))
    o_p = _attn_prompt(proj_p, gl_p, b_g_q[0], kcmp_p, vcmp_p, kv_p[6], kv_p[7], kv_p[8], kv_p[9],
                       rr, sb, wb, bank_k)

    def head_rows(x):
        return x.reshape(db, ds, N_HEADS, HEAD_DIM).transpose(0, 2, 1, 3).reshape(db, N_HEADS * ds, HEAD_DIM)

    q_s = head_rows(proj_s[:, :hd])
    zz_s = jnp.stack([head_rows(proj_s[:, (1 + br) * hd:(2 + br) * hd]) for br in range(3)], axis=1)
    gl_s = proj_s[:, 4 * hd:4 * hd + 3 * N_HEADS].reshape(db, ds, 3, N_HEADS).transpose(0, 2, 3, 1)
    gl_s = gl_s.reshape(db, 3, N_HEADS * ds, 1)
    new_w = LANES
    group_rows = lambda x: x.reshape(db, -1, HEAD_DIM)
    pad_new = lambda x: jnp.pad(group_rows(x), ((0, 0), (0, (new_w - ds) * N_KV), (0, 0)))
    n_cmp_s = past_len // CMP_STRIDE
    bsel = _query_rows_bias(fd, past_len, ds, past_len + new_w).reshape(N_KV, HEADS_PER_GROUP * ds, -1)
    bwin = bsel[:, :, past_len - wbuf:]
    bcmp = bsel[:, :, CMP_BLOCK - 1::CMP_STRIDE][:, :, :n_cmp_s]
    n_sb_s = past_len // SEL_BLOCK
    ci = jnp.arange(n_cmp_s)[:, None] * CMP_STRIDE
    sj = jnp.arange(n_sb_s)[None, :] * SEL_BLOCK
    overlap = ((ci < sj + SEL_BLOCK) & (ci + CMP_BLOCK > sj) & (ci < past_len - CMP_STRIDE)).astype(F32)
    expand_s = (jnp.arange(past_len)[None, :] // SEL_BLOCK == jnp.arange(n_sb_s)[:, None]).astype(BF16)
    o_s = _attn_sample(q_s, zz_s, gl_s, b_g_q[0], kcmp_s, vcmp_s,
                       cache_k_sel.reshape(n_pool, -1, HEAD_DIM), cache_v_sel.reshape(n_pool, -1, HEAD_DIM),
                       page_table, pad_new(ks_s), pad_new(vs_s), pad_new(kw_s), pad_new(vw_s),
                       group_rows(cache_k_win), group_rows(cache_v_win),
                       bcmp, bsel, bwin, overlap, expand_s, ds)
    o_s = o_s.reshape(db, N_HEADS, ds, HEAD_DIM).transpose(0, 2, 1, 3).reshape(db * ds, hd)

    ple1 = (w_proj[1], ple_norm_g[1], w_gate[1])
    y_p = _outproj_ple(o_p, h1_p, p_prompt[1].reshape(s_len, -1), w_out1, *ple1)
    y_s = _outproj_ple(o_s, h1_s, p_sample[1].reshape(db * ds, -1), w_out1, *ple1)

    rows4 = lambda x, b, t: x.reshape(b, t, N_KV, HEAD_DIM)
    wk_p = min(WINDOW, s_len)
    kwin_s = jnp.concatenate([cache_k_win, rows4(kw_s, db, ds)], axis=1)[:, -WINDOW:]
    vwin_s = jnp.concatenate([cache_v_win, rows4(vw_s, db, ds)], axis=1)[:, -WINDOW:]
    return (y_p.reshape(bp, s_len, d), y_s.reshape(db, ds, d),
            conv_p, h_p.reshape(1, bp, d),
            rows4(kc_p, bp, s_len), rows4(vc_p, bp, s_len), rows4(ks_p, bp, s_len), rows4(vs_p, bp, s_len),
            rows4(kw_p, bp, s_len)[:, -wk_p:], rows4(vw_p, bp, s_len)[:, -wk_p:],
            conv_s, h_s.reshape(1, db, d),
            rows4(kc_s, db, ds), rows4(vc_s, db, ds), rows4(ks_s, db, ds), rows4(vs_s, db, ds),
            kwin_s, vwin_s)
```

```python
import functools
import math

import jax
import jax.numpy as jnp
from jax import lax
from jax.experimental import pallas as pl
from jax.experimental.pallas import tpu as pltpu

F32 = jnp.float32
BF16 = jnp.bfloat16

N_RNN_BLOCKS = 8
CONV_W = 4
LRU_C = 8.0
N_HEADS = 16
HEAD_DIM = 128
N_KV = 4
HEADS_PER_GROUP = N_HEADS // N_KV
GROUP_W = HEADS_PER_GROUP * HEAD_DIM
KV_W = N_KV * HEAD_DIM
CMP_BLOCK = 32
CMP_STRIDE = 16
SEL_BLOCK = 64
N_SELECT = 16
WINDOW = 512
N_BUCKETS = 32
MAX_DISTANCE = 4096
EPS = 1e-6
NEG = -1e30
FORCE = 1e9
REMOVED = -3e38
LOG2E = math.log2(math.e)

LANES = 128
SUBLANES = 8
VMEM_LIMIT = 56 * 1024 * 1024

TQ = 128
TK_SEL = 512
CMP_PAGES_PER_STEP = 16


def _cparams(sem):
    return pltpu.CompilerParams(dimension_semantics=sem, vmem_limit_bytes=VMEM_LIMIT)


def _resident(shape, index_map):
    return pl.BlockSpec(shape, index_map, pipeline_mode=pl.Buffered(1))


def _rms_rows(x, g):
    return x * lax.rsqrt(jnp.mean(x * x, axis=-1, keepdims=True) + EPS) * g


def _sigmoid(x):
    return 1.0 / (1.0 + jnp.exp(-x))


def _norm_matmul_kernel(x_ref, g_ref, w_ref, o_ref, xn_ref):
    @pl.when(pl.program_id(1) == 0)
    def _():
        xn_ref[...] = _rms_rows(x_ref[...], g_ref[...]).astype(BF16)

    o_ref[...] = jnp.dot(xn_ref[...], w_ref[...], preferred_element_type=F32)


def _norm_matmul(x, g, w, tn):
    m, d = x.shape
    n = w.shape[1]
    tm = min(m, 1024)
    return pl.pallas_call(
        _norm_matmul_kernel,
        grid=(m // tm, n // tn),
        in_specs=[pl.BlockSpec((tm, d), lambda i, j: (i, 0)),
                  pl.BlockSpec((1, d), lambda i, j: (0, 0)),
                  pl.BlockSpec((d, tn), lambda i, j: (0, j))],
        out_specs=pl.BlockSpec((tm, tn), lambda i, j: (i, j)),
        out_shape=jax.ShapeDtypeStruct((m, n), F32),
        scratch_shapes=[pltpu.VMEM((tm, d), BF16)],
        compiler_params=_cparams(("parallel", "arbitrary")),
        name="norm_matmul",
    )(x, g.reshape(1, d), w)


def _lru_gates(xc, wri_ref, br_ref, bi_ref, sp_ref, n, bw):
    sl = slice(n * bw, (n + 1) * bw)
    ri = jnp.dot(xc.astype(BF16), wri_ref[n], preferred_element_type=F32)
    r = _sigmoid(ri[:, :bw] + br_ref[:, sl])
    i = _sigmoid(ri[:, bw:] + bi_ref[:, sl])
    log_a = (-LRU_C) * r * sp_ref[:, sl]
    a = jnp.exp(log_a)
    b = jnp.sqrt(1.0 - jnp.exp(2.0 * log_a)) * (i * xc)
    return a, b


def _softplus(x):
    return jnp.maximum(x, 0.0) + jnp.log(1.0 + jnp.exp(-jnp.abs(x)))


def _rglru_prompt_kernel(u_ref, z_ref, cinit_ref, hinit_ref, cw_ref, cb_ref, wri_ref, br_ref, bi_ref,
                         lam_ref, g_ref, hlast_ref, ubuf, a_scr, b_scr, hs_scr, hcar, sp_scr):
    tm, d = u_ref.shape
    bw = d // N_RNN_BLOCKS

    @pl.when(pl.program_id(0) == 0)
    def _():
        ubuf[0:SUBLANES, :] = cinit_ref[...]
        hcar[...] = hinit_ref[...]
        sp_scr[...] = _softplus(-lam_ref[...])

    ubuf[SUBLANES:, :] = u_ref[...]
    for n in range(N_RNN_BLOCKS):
        sl = slice(n * bw, (n + 1) * bw)
        up = ubuf[:, sl]
        xc = cb_ref[:, sl] + up * cw_ref[CONV_W - 1:CONV_W, sl]
        for k in range(CONV_W - 1):
            xc = xc + pltpu.roll(up, CONV_W - 1 - k, axis=0) * cw_ref[k:k + 1, sl]
        xc = xc[SUBLANES:]
        a, b = _lru_gates(xc, wri_ref, br_ref, bi_ref, sp_scr, n, bw)
        a_scr[:, sl] = a
        b_scr[:, sl] = b
    ubuf[0:SUBLANES, :] = u_ref[tm - SUBLANES:tm, :]

    def step(t, h):
        h = a_scr[pl.ds(t, 1), :] * h + b_scr[pl.ds(t, 1), :]
        hs_scr[pl.ds(t, 1), :] = h
        return h

    h = lax.fori_loop(0, tm, step, hcar[...], unroll=8)
    hcar[...] = h
    hlast_ref[...] = h
    z = z_ref[...]
    g_ref[...] = (hs_scr[...] * (z * _sigmoid(z))).astype(BF16)


def _rglru_prompt(uz, conv_init, h_init, cw, cb, wri, br, bi, lam):
    t, d2 = uz.shape
    d = d2 // 2
    tm = min(t, 256)
    full = lambda shape: pl.BlockSpec(shape, lambda i: (0,) * len(shape))
    return pl.pallas_call(
        _rglru_prompt_kernel,
        grid=(t // tm,),
        in_specs=[pl.BlockSpec((tm, d), lambda i: (i, 0)),
                  pl.BlockSpec((tm, d), lambda i: (i, 1)),
                  full((SUBLANES, d)), full((1, d)), full((CONV_W, d)), full((1, d)),
                  full(wri.shape), full((1, d)), full((1, d)), full((1, d))],
        out_specs=[pl.BlockSpec((tm, d), lambda i: (i, 0)), full((1, d))],
        out_shape=[jax.ShapeDtypeStruct((t, d), BF16), jax.ShapeDtypeStruct((1, d), F32)],
        scratch_shapes=[pltpu.VMEM((tm + SUBLANES, d), F32), pltpu.VMEM((tm, d), F32),
                        pltpu.VMEM((tm, d), F32), pltpu.VMEM((tm, d), F32),
                        pltpu.VMEM((1, d), F32), pltpu.VMEM((1, d), F32)],
        compiler_params=_cparams(("arbitrary",)),
        name="rglru_prompt",
    )(uz, uz, conv_init, h_init, cw, cb.reshape(1, d), wri, br.reshape(1, d), bi.reshape(1, d),
      lam.reshape(1, d))


def _rglru_sample_kernel(up_ref, z_ref, h0_ref, cw_ref, cb_ref, wri_ref, br_ref, bi_ref, lam_ref,
                         g_ref, hlast_ref, sp_scr):
    n_t = z_ref.shape[0]
    d = z_ref.shape[2]
    bw = d // N_RNN_BLOCKS
    sp_scr[...] = _softplus(-lam_ref[...])
    for n in range(N_RNN_BLOCKS):
        sl = slice(n * bw, (n + 1) * bw)
        h = h0_ref[:, sl]
        for t in range(n_t):
            xc = cb_ref[:, sl]
            for k in range(CONV_W):
                xc = xc + up_ref[t + k, :, sl] * cw_ref[k:k + 1, sl]
            a, b = _lru_gates(xc, wri_ref, br_ref, bi_ref, sp_scr, n, bw)
            h = a * h + b
            z = z_ref[t, :, sl]
            g_ref[t, :, sl] = (h * (z * _sigmoid(z))).astype(BF16)
        hlast_ref[:, sl] = h


def _rglru_sample(up, z, h0, cw, cb, wri, br, bi, lam):
    n_t, nb, d = z.shape
    return pl.pallas_call(
        _rglru_sample_kernel,
        out_shape=[jax.ShapeDtypeStruct((n_t, nb, d), BF16), jax.ShapeDtypeStruct((nb, d), F32)],
        scratch_shapes=[pltpu.VMEM((1, d), F32)],
        compiler_params=pltpu.CompilerParams(vmem_limit_bytes=VMEM_LIMIT),
        name="rglru_sample",
    )(up, z, h0, cw, cb.reshape(1, d), wri, br.reshape(1, d), bi.reshape(1, d), lam.reshape(1, d))


def _outproj_ple_kernel(a_ref, res_ref, p_ref, wo_ref, wp_ref, gn_ref, wg_ref, o_ref, h_scr, hn_scr):
    tm, d = res_ref.shape
    ch = min(d, 512)
    a = a_ref[...]
    ssq = jnp.zeros((tm, 1), F32)
    for c in range(d // ch):
        sl = slice(c * ch, (c + 1) * ch)
        h = res_ref[:, sl] + jnp.dot(a, wo_ref[:, sl], preferred_element_type=F32)
        h_scr[:, sl] = h
        ssq = ssq + jnp.sum(h * h, axis=-1, keepdims=True)
    inv = lax.rsqrt(ssq * (1.0 / d) + EPS)
    for c in range(d // ch):
        sl = slice(c * ch, (c + 1) * ch)
        hn_scr[:, sl] = (h_scr[:, sl] * inv * gn_ref[:, sl]).astype(BF16)
    pb = p_ref[...].astype(BF16)
    hn = hn_scr[...]
    for c in range(d // ch):
        sl = slice(c * ch, (c + 1) * ch)
        gate = _sigmoid(jnp.dot(hn, wg_ref[:, sl], preferred_element_type=F32))
        pp = jnp.dot(pb, wp_ref[:, sl], preferred_element_type=F32)
        o_ref[:, sl] = h_scr[:, sl] + pp * gate


def _outproj_ple(a, res, p, w_out, w_proj, g_norm, w_gate):
    m, d = res.shape
    pd = p.shape[1]
    tm = min(m, 512)
    return pl.pallas_call(
        _outproj_ple_kernel,
        grid=(m // tm,),
        in_specs=[pl.BlockSpec((tm, a.shape[1]), lambda i: (i, 0)),
                  pl.BlockSpec((tm, d), lambda i: (i, 0)),
                  pl.BlockSpec((tm, pd), lambda i: (i, 0)),
                  _resident(w_out.shape, lambda i: (0, 0)),
                  _resident(w_proj.shape, lambda i: (0, 0)),
                  _resident((1, d), lambda i: (0, 0)),
                  _resident(w_gate.shape, lambda i: (0, 0))],
        out_specs=pl.BlockSpec((tm, d), lambda i: (i, 0)),
        out_shape=jax.ShapeDtypeStruct((m, d), F32),
        scratch_shapes=[pltpu.VMEM((tm, d), F32), pltpu.VMEM((tm, d), BF16)],
        compiler_params=_cparams(("parallel",)),
        name="outproj_ple",
    )(a, res, p, w_out, w_proj, g_norm.reshape(1, d), w_gate)


AUG_W = 2 * HEAD_DIM
MASK_OFF = -2.0 ** 60


def _kv_proj_kernel(x_ref, g_ref, w_ref, gsel_ref, gwin_ref,
                    kc_ref, vc_ref, ks_ref, vs_ref, kw_ref, vw_ref, ksa_ref, vsa_ref, kwb_ref, vwa_ref):
    tm = x_ref.shape[0]
    xn = _rms_rows(x_ref[...], g_ref[...]).astype(BF16)
    outs = (kc_ref, vc_ref, ks_ref, vs_ref, kw_ref, vw_ref)
    head_gain = {2: gsel_ref, 4: gwin_ref}
    pos = pl.program_id(0) * tm + lax.broadcasted_iota(jnp.int32, (tm, HEAD_DIM), 0)
    blk = lax.broadcasted_iota(jnp.int32, (tm, HEAD_DIM), 1)
    block_mask = jnp.where(blk == pos // SEL_BLOCK, MASK_OFF, 0.0).astype(BF16)
    ones = jnp.ones((tm, HEAD_DIM), BF16)
    for s in range(6):
        y = jnp.dot(xn, w_ref[:, s * KV_W:(s + 1) * KV_W], preferred_element_type=F32)
        for g in range(N_KV):
            yg = y[:, g * HEAD_DIM:(g + 1) * HEAD_DIM]
            if s in head_gain:
                yg = _rms_rows(yg, head_gain[s][...])
            outs[s][pl.ds(g, tm, stride=N_KV), :] = yg
            lo = slice(g * AUG_W, g * AUG_W + HEAD_DIM)
            hi = slice(g * AUG_W + HEAD_DIM, (g + 1) * AUG_W)
            if s == 2:
                ksa_ref[:, lo] = yg.astype(BF16)
                ksa_ref[:, hi] = block_mask
            elif s == 3:
                vsa_ref[:, lo] = yg.astype(BF16)
                vsa_ref[:, hi] = ones
            elif s == 4:
                kwb_ref[:, g * HEAD_DIM:(g + 1) * HEAD_DIM] = yg.astype(BF16)
            elif s == 5:
                vwa_ref[:, lo] = yg.astype(BF16)
                vwa_ref[:, hi] = ones


def _kv_proj(x, g, w, g_sel, g_win):
    m, d = x.shape
    tm = min(m, 512)
    row = lambda i: (i, 0)
    aug = N_KV * AUG_W
    return pl.pallas_call(
        _kv_proj_kernel,
        grid=(m // tm,),
        in_specs=[pl.BlockSpec((tm, d), row), _resident((1, d), lambda i: (0, 0)),
                  _resident(w.shape, lambda i: (0, 0)),
                  _resident((1, HEAD_DIM), lambda i: (0, 0)), _resident((1, HEAD_DIM), lambda i: (0, 0))],
        out_specs=([pl.BlockSpec((tm * N_KV, HEAD_DIM), row)] * 6
                   + [pl.BlockSpec((tm, aug), row), pl.BlockSpec((tm, aug), row),
                      pl.BlockSpec((tm, KV_W), row), pl.BlockSpec((tm, aug), row)]),
        out_shape=([jax.ShapeDtypeStruct((m * N_KV, HEAD_DIM), F32)] * 6
                   + [jax.ShapeDtypeStruct((m, aug), BF16), jax.ShapeDtypeStruct((m, aug), BF16),
                      jax.ShapeDtypeStruct((m, KV_W), BF16), jax.ShapeDtypeStruct((m, aug), BF16)]),
        compiler_params=_cparams(("parallel",)),
        name="kv_proj",
    )(x, g.reshape(1, d), w, g_sel.reshape(1, HEAD_DIM), g_win.reshape(1, HEAD_DIM))


PIECE_ROWS = CMP_STRIDE * N_KV
PIECE_PITCH = PIECE_ROWS + SUBLANES


def _piece_rows(buf, r, n):
    flat = buf.reshape(buf.shape[0] * PIECE_PITCH, HEAD_DIM)
    return flat[pl.ds(r, n, stride=PIECE_PITCH), :]


def _compress_kernel(pt_ref, kc_hbm, vc_hbm, pe_ref, w1_ref, b1_ref, w2_ref, b2_ref, gk_ref,
                     kcmp_ref, vcmp_ref, kbuf, vbuf, sem, lhs, *, n_quarters, pages_per_step):
    s = pl.program_id(0)
    n_steps = pl.num_programs(0)
    pieces_per_page = kc_hbm.shape[1]
    n_piece = pages_per_step * pieces_per_page
    n_load = n_piece + SUBLANES
    rows_g = n_piece + 2 * SUBLANES
    hidden = w2_ref.shape[1]

    def copies(step, slot):
        b = step // n_quarters
        q = step % n_quarters
        out = []
        for hbm, buf, which in ((kc_hbm, kbuf, 0), (vc_hbm, vbuf, 1)):
            for p in range(pages_per_step):
                pg = pt_ref[b, q * pages_per_step + p]
                dst = buf.at[slot, pl.ds(p * pieces_per_page, pieces_per_page), pl.ds(0, PIECE_ROWS), :]
                out.append(pltpu.make_async_copy(hbm.at[pg], dst, sem.at[slot, which]))
            pg = pt_ref[b, q * pages_per_step + pages_per_step]
            out.append(pltpu.make_async_copy(hbm.at[pg, 0], buf.at[slot, n_piece, pl.ds(0, PIECE_ROWS), :],
                                             sem.at[slot, which]))
        return out

    slot = s % 2

    @pl.when(s == 0)
    def _():
        kbuf[...] = jnp.zeros(kbuf.shape, F32)
        vbuf[...] = jnp.zeros(vbuf.shape, F32)
        lhs[...] = jnp.zeros(lhs.shape, BF16)
        lhs[:, N_KV * rows_g:, :] = pe_ref[...]
        for c in copies(0, 0):
            c.start()

    @pl.when(s + 1 < n_steps)
    def _():
        for c in copies(s + 1, 1 - slot):
            c.start()

    for c in copies(s, slot):
        c.wait()

    for which, buf, out_ref in ((0, kbuf, kcmp_ref), (1, vbuf, vcmp_ref)):
        for g in range(N_KV):
            for pos in range(CMP_STRIDE):
                x = _piece_rows(buf.at[slot], pos * N_KV + g, n_load)
                lhs[which, g * rows_g:g * rows_g + n_load, pos * HEAD_DIM:(pos + 1) * HEAD_DIM] = x.astype(BF16)
        hab = jnp.dot(lhs[which], w1_ref[which], preferred_element_type=F32)
        pe0 = N_KV * rows_g
        bias = (jnp.sum(hab[pe0:pe0 + SUBLANES, :hidden], axis=0, keepdims=True)
                + jnp.sum(hab[pe0 + SUBLANES:pe0 + 2 * SUBLANES, hidden:], axis=0, keepdims=True) + b1_ref[which])
        for g in range(N_KV):
            blk = hab[g * rows_g:(g + 1) * rows_g]
            nxt = pltpu.roll(blk[:, hidden:], rows_g - 1, axis=0)
            hid = blk[:n_piece, :hidden] + nxt[:n_piece] + bias
            hid = hid * _sigmoid(hid)
            y = jnp.dot(hid.astype(BF16), w2_ref[which], preferred_element_type=F32) + b2_ref[which]
            if which == 0:
                y = _rms_rows(y, gk_ref[...])
            out_ref[0, g] = y.astype(BF16)


def _compress(kc_pages, vc_pages, table, pe_rows, w1, b1, w2, b2, g_k):
    nb, n_pages = table.shape[0], table.shape[1] - 1
    ppp = kc_pages.shape[1]
    pps = min(CMP_PAGES_PER_STEP, n_pages)
    n_quarters = n_pages // pps
    n_piece = pps * ppp
    rows_g = n_piece + 2 * SUBLANES
    kdim = CMP_STRIDE * HEAD_DIM
    full = lambda shape: pl.BlockSpec(shape, lambda s, pt: (0,) * len(shape))
    out_spec = pl.BlockSpec((1, N_KV, n_piece, HEAD_DIM), lambda s, pt: (s // n_quarters, 0, s % n_quarters, 0))
    out_sd = jax.ShapeDtypeStruct((nb, N_KV, n_pages * ppp, HEAD_DIM), BF16)
    buf = pltpu.VMEM((2, n_piece + SUBLANES, PIECE_PITCH, HEAD_DIM), F32)
    grid_spec = pltpu.PrefetchScalarGridSpec(
        num_scalar_prefetch=1,
        grid=(nb * n_quarters,),
        in_specs=[pl.BlockSpec(memory_space=pl.ANY), pl.BlockSpec(memory_space=pl.ANY),
                  full(pe_rows.shape), full(w1.shape), full(b1.shape), full(w2.shape), full(b2.shape),
                  full((1, HEAD_DIM))],
        out_specs=[out_spec, out_spec],
        scratch_shapes=[buf, buf, pltpu.SemaphoreType.DMA((2, 2)),
                        pltpu.VMEM((2, N_KV * rows_g + 2 * SUBLANES, kdim), BF16)],
    )
    return pl.pallas_call(
        functools.partial(_compress_kernel, n_quarters=n_quarters, pages_per_step=pps),
        grid_spec=grid_spec,
        out_shape=[out_sd, out_sd],
        compiler_params=_cparams(("arbitrary",)),
        name="compress",
    )(table, kc_pages, vc_pages, pe_rows, w1, b1, w2, b2, g_k.reshape(1, HEAD_DIM))


def _rel_bucket(dist):
    n = jnp.maximum(dist, 0)
    max_exact = N_BUCKETS // 2
    nf = jnp.maximum(n, 1).astype(F32)
    large = max_exact + (jnp.log(nf / max_exact) / math.log(MAX_DISTANCE / max_exact)
                         * (N_BUCKETS - max_exact)).astype(jnp.int32)
    large = jnp.minimum(large, N_BUCKETS - 1)
    return jnp.where(n < max_exact, n, large)


def _bias_table(rel_bias, dist, masked, gain=1.0):
    dist = lax.optimization_barrier(dist)
    onehot = (_rel_bucket(dist)[None, :] == jnp.arange(N_BUCKETS)[:, None]).astype(F32)
    table = jnp.dot(rel_bias.astype(F32).T * gain, onehot, precision=lax.Precision.HIGHEST)
    return jnp.where(dist[None, :] >= 0, table, NEG) if masked else table


def _bias_bank_kernel(gs_ref, gw_ref, g2_ref, sb_ref, wb_ref, rr_ref):
    for g_ref, o_ref in ((gs_ref, sb_ref), (gw_ref, wb_ref)):
        wp = g_ref.shape[2]
        x = jnp.broadcast_to(g_ref[0], (TQ, wp))
        o_ref[0] = pltpu.roll(x, wp - (TQ - 1), 1, stride=1, stride_axis=0)[:, :wp - TQ]

    n_blocks = g2_ref.shape[1]
    for il in range(SUBLANES):
        left = CMP_STRIDE * (SUBLANES - 1 - il)
        y = pltpu.roll(g2_ref[0], (2 * TQ - left) % (2 * TQ), 1)
        rr_ref[0, pl.ds(il, n_blocks, stride=SUBLANES), :] = y[:, :TQ]


def _bias_banks(rel_bias, n_qt, n_row_blocks, bank_k, bank_w):
    nh = rel_bias.shape[1]
    wk = WINDOW + TQ
    gs = _bias_table(rel_bias, bank_k + TQ - 1 - jnp.arange(bank_k + bank_w + TQ), True, LOG2E)
    dw = WINDOW + TQ - 1 - jnp.arange(WINDOW + wk + TQ)
    gw = _bias_table(rel_bias, jnp.where(dw < WINDOW, dw, -1), True, LOG2E)
    n_blocks = -(-(n_qt + n_row_blocks - 1) // SUBLANES) * SUBLANES
    d2 = (TQ * (n_qt - 1 - jnp.arange(n_blocks))[:, None] + jnp.arange(2 * TQ)[None, :]
          - (CMP_BLOCK - 1) - CMP_STRIDE * (SUBLANES - 1))
    g2 = _bias_table(rel_bias, d2.reshape(-1), True, LOG2E).reshape(nh, n_blocks, 2 * TQ)
    spec3 = lambda a: pl.BlockSpec((1,) + a.shape[1:], lambda h: (h, 0, 0))
    outs = [jax.ShapeDtypeStruct((nh, TQ, bank_k + bank_w), F32), jax.ShapeDtypeStruct((nh, TQ, WINDOW + wk), F32),
            jax.ShapeDtypeStruct((nh, n_blocks * SUBLANES, TQ), F32)]
    gs, gw = gs[:, None], gw[:, None]
    return pl.pallas_call(
        _bias_bank_kernel,
        grid=(nh,),
        in_specs=[spec3(gs), spec3(gw), spec3(g2)],
        out_specs=[spec3(o) for o in outs],
        out_shape=outs,
        compiler_params=_cparams(("parallel",)),
        name="bias_banks",
    )(gs, gw, g2)


def _query_rows_bias(rel_bias, t0, n_tok, n_keys):
    dist = (t0 + jnp.arange(n_tok)[:, None] - jnp.arange(n_keys)[None, :]).reshape(-1)
    return _bias_table(rel_bias, dist, False).reshape(-1, n_tok, n_keys)


def _first_constant_distance():
    ratio = MAX_DISTANCE / (N_BUCKETS // 2)
    return int(math.ceil((N_BUCKETS // 2) * ratio ** ((N_BUCKETS // 2 - 1) / (N_BUCKETS // 2)))) + 2


def _topk_rows(score, blk, k_top):
    n_blk = score.shape[0]
    sel = jnp.zeros(score.shape, jnp.bool_)
    for _ in range(k_top):
        m = jnp.max(score, axis=0, keepdims=True)
        idx = jnp.min(jnp.where(score == m, blk, n_blk), axis=0, keepdims=True)
        pick = blk == idx
        sel = sel | pick
        score = jnp.where(pick, REMOVED, score)
    return sel


def _attn_prompt_kernel(q_ref, z0_ref, z1_ref, z2_ref, gl_ref, gq_ref, kcmp_ref, vcmp_ref,
                        ksa_ref, vsa_ref, kw_ref, vwa_ref, rr_ref, sb_ref, wb_ref, o_ref,
                        q_scr, psum_scr, m_scr, acc_scr, s0_scr, s1_scr, *, n_qtiles, bank_k):
    qi = pl.program_id(1)
    q0 = qi * TQ
    n_cmp_rows = kcmp_ref.shape[1]
    n_sb = n_cmp_rows * CMP_STRIDE // SEL_BLOCK
    hg = HEADS_PER_GROUP
    q_gain = HEAD_DIM ** -0.5 * LOG2E

    for h in range(hg):
        qh = _rms_rows(q_ref[:, h * HEAD_DIM:(h + 1) * HEAD_DIM], gq_ref[...]) * q_gain
        q_scr[h * TQ:(h + 1) * TQ, 0:HEAD_DIM] = qh.astype(BF16)
    q_all = q_scr[:, 0:HEAD_DIM]

    t_row = q0 + lax.broadcasted_iota(jnp.int32, (1, TQ), 1)


    s_t = lax.dot_general(kcmp_ref[0], q_all, (((1,), (1,)), ((), ())), preferred_element_type=F32)
    rr_start = pl.multiple_of((n_qtiles - 1 - qi) * SUBLANES, SUBLANES)
    psum = jnp.zeros((n_cmp_rows, TQ), F32)
    o_cmp = []
    for h in range(hg):
        sh = s_t[:, h * TQ:(h + 1) * TQ] + rr_ref[h, pl.ds(rr_start, n_cmp_rows), :]
        mx = jnp.maximum(jnp.max(sh, axis=0, keepdims=True), 0.5 * NEG)
        e = jnp.exp2(sh - mx)
        den = jnp.sum(e, axis=0, keepdims=True)
        p = e * (1.0 / jnp.where(den > 0.0, den, 1.0))
        psum = psum + p
        o_cmp.append(lax.dot_general(p.astype(BF16), vcmp_ref[0], (((0,), (0,)), ((), ())),
                                     preferred_element_type=F32))

    wk = WINDOW + TQ
    w0 = pl.multiple_of(jnp.maximum(q0 - WINDOW, 0), LANES)
    s_w = lax.dot_general(q_all, kw_ref[pl.ds(w0, wk), :], (((1,), (1,)), ((), ())), preferred_element_type=F32)
    vwin = vwa_ref[pl.ds(w0, wk), :]
    wb_start = pl.multiple_of(WINDOW - (q0 - w0), LANES)
    o_win = []
    for h in range(hg):
        sh = s_w[h * TQ:(h + 1) * TQ] + wb_ref[h, :, pl.ds(wb_start, wk)]
        e = jnp.exp2(sh - jnp.max(sh, axis=1, keepdims=True))
        o_aug = jnp.dot(e.astype(BF16), vwin, preferred_element_type=F32)
        o_win.append(o_aug[:, 0:HEAD_DIM] / o_aug[:, HEAD_DIM:AUG_W])

    psum_scr[0:SUBLANES, :] = jnp.zeros((SUBLANES, TQ), F32)
    psum_scr[SUBLANES:, :] = psum
    ratio = SEL_BLOCK // CMP_STRIDE
    imp = psum_scr[pl.ds(SUBLANES - 1, n_sb, stride=ratio), :]
    for k in range(ratio):
        imp = imp + psum_scr[pl.ds(SUBLANES + k, n_sb, stride=ratio), :]
    blk = lax.broadcasted_iota(jnp.int32, (n_sb, 1), 0)
    cur = t_row // SEL_BLOCK
    blk_ok = blk <= cur
    forced = (blk == 0) | (blk == cur) | (blk == cur - 1)
    score = jnp.where(blk_ok, jnp.where(forced, FORCE, imp), -FORCE)
    sel_t = _topk_rows(score, blk, min(N_SELECT, n_sb)) & blk_ok
    not_sel = jnp.where(sel_t, 0.0, 1.0).T
    if n_sb < HEAD_DIM:
        not_sel = jnp.concatenate([not_sel, jnp.zeros((TQ, HEAD_DIM - n_sb), F32)], axis=1)
    for h in range(hg):
        q_scr[h * TQ:(h + 1) * TQ, HEAD_DIM:AUG_W] = not_sel.astype(BF16)
    q_aug = q_scr[...]

    m_scr[...] = jnp.full(m_scr.shape, NEG, F32)
    acc_scr[...] = jnp.zeros(acc_scr.shape, F32)
    n_kv = (q0 + TQ + TK_SEL - 1) // TK_SEL
    rep = TK_SEL // LANES

    def scores(j, s_ref):
        k0 = pl.multiple_of(j * TK_SEL, TK_SEL)
        s_ref[...] = lax.dot_general(q_aug, ksa_ref[pl.ds(k0, TK_SEL), :], (((1,), (1,)), ((), ())),
                                     preferred_element_type=F32)

    def kv_tile(j, s_ref):
        k0 = pl.multiple_of(j * TK_SEL, TK_SEL)
        vt = vsa_ref[pl.ds(k0, TK_SEL), :]
        b_start = pl.multiple_of(jnp.maximum(bank_k - (q0 - k0), 0), LANES)
        for h in range(hg):
            rows = slice(h * TQ, (h + 1) * TQ)
            sh = s_ref[rows] + sb_ref[h, :, pl.ds(b_start, TK_SEL)]
            m_prev = m_scr[rows]
            m_next = jnp.maximum(m_prev, jnp.max(sh, axis=1, keepdims=True))
            alpha = jnp.exp2(m_prev - m_next)
            e = jnp.exp2(sh - jnp.tile(m_next, (1, rep)))
            acc_scr[rows] = (jnp.tile(alpha, (1, AUG_W // LANES)) * acc_scr[rows]
                             + jnp.dot(e.astype(BF16), vt, preferred_element_type=F32))
            m_scr[rows] = m_next

    def run(first, count):
        bufs = (s0_scr, s1_scr)
        for k in range(count):
            scores(first + k + 1, bufs[(k + 1) % 2])
            kv_tile(first + k, bufs[k % 2])

    def kv_quad(i, carry):
        run(4 * i, 4)
        return carry

    scores(0, s0_scr)
    n_before = n_kv - 1
    lax.fori_loop(0, n_before // 4, kv_quad, 0)
    done4 = n_before // 4 * 4

    @pl.when(n_before - done4 >= 2)
    def _():
        run(done4, 2)

    done2 = n_before // 2 * 2

    @pl.when(n_before > done2)
    def _():
        scores(done2 + 1, s1_scr)
        kv_tile(done2, s0_scr)
        kv_tile(done2 + 1, s1_scr)

    @pl.when(n_before == done2)
    def _():
        kv_tile(done2, s0_scr)

    sig_gl = _sigmoid(gl_ref[0])
    z_refs = (z0_ref, z1_ref, z2_ref)
    for h in range(hg):
        rows = slice(h * TQ, (h + 1) * TQ)
        cols = slice(h * HEAD_DIM, (h + 1) * HEAD_DIM)
        o_sel = acc_scr[rows, 0:HEAD_DIM] / acc_scr[rows, HEAD_DIM:AUG_W]
        out = jnp.zeros((TQ, HEAD_DIM), F32)
        for br, o_br in enumerate((o_cmp[h], o_sel, o_win[h])):
            z = z_refs[br][:, cols]
            c = br * hg + h
            out = out + sig_gl[:, c:c + 1] * o_br * (z * _sigmoid(z))
        o_ref[:, cols] = out.astype(BF16)


def _attn_prompt(proj, gl_g, g_q, kcmp, vcmp, ks_aug, vs_aug, kw_b, vw_aug, rr, sb, wb, bank_k):
    s_len = proj.shape[0]
    n_qt = s_len // TQ
    n_cmp_rows = kcmp.shape[2]
    gcols = N_HEADS * HEAD_DIM // GROUP_W
    qz = lambda seg: pl.BlockSpec((TQ, GROUP_W), lambda g, i, seg=seg: (i, seg * gcols + g))
    per_group = lambda shape: _resident((1,) + shape, lambda g, i: (g,) + (0,) * len(shape))
    kv_spec = lambda width: _resident((s_len, width), lambda g, i: (0, g))
    return pl.pallas_call(
        functools.partial(_attn_prompt_kernel, n_qtiles=n_qt, bank_k=bank_k),
        grid=(N_KV, n_qt),
        in_specs=[qz(0), qz(1), qz(2), qz(3),
                  pl.BlockSpec((1, TQ, LANES), lambda g, i: (g, i, 0)),
                  _resident((1, HEAD_DIM), lambda g, i: (0, 0)),
                  per_group((n_cmp_rows, HEAD_DIM)), per_group((n_cmp_rows, HEAD_DIM)),
                  kv_spec(AUG_W), kv_spec(AUG_W), kv_spec(HEAD_DIM), kv_spec(AUG_W),
                  _resident((HEADS_PER_GROUP,) + rr.shape[1:], lambda g, i: (g, 0, 0)),
                  _resident((HEADS_PER_GROUP,) + sb.shape[1:], lambda g, i: (g, 0, 0)),
                  _resident((HEADS_PER_GROUP,) + wb.shape[1:], lambda g, i: (g, 0, 0))],
        out_specs=pl.BlockSpec((TQ, GROUP_W), lambda g, i: (i, g)),
        out_shape=jax.ShapeDtypeStruct((s_len, N_HEADS * HEAD_DIM), BF16),
        scratch_shapes=[pltpu.VMEM((HEADS_PER_GROUP * TQ, AUG_W), BF16),
                        pltpu.VMEM((n_cmp_rows + SUBLANES, TQ), F32),
                        pltpu.VMEM((HEADS_PER_GROUP * TQ, LANES), F32),
                        pltpu.VMEM((HEADS_PER_GROUP * TQ, AUG_W), F32),
                        pltpu.VMEM((HEADS_PER_GROUP * TQ, TK_SEL), F32),
                        pltpu.VMEM((HEADS_PER_GROUP * TQ, TK_SEL), F32)],
        compiler_params=_cparams(("arbitrary", "arbitrary")),
        name="attn_prompt",
    )(proj, proj, proj, proj, gl_g, g_q.reshape(1, HEAD_DIM), kcmp[0], vcmp[0],
      ks_aug, vs_aug, kw_b, vw_aug, rr, sb, wb)


def _attn_sample_kernel(*refs, n_tok, pages_per_chunk, page, past_len):
    pt_ref = refs[0]
    q_ref, z_ref, gl_ref, gq_ref, kcmp_ref, vcmp_ref = refs[1:7]
    kpages = refs[7:7 + pages_per_chunk]
    vpages = refs[7 + pages_per_chunk:7 + 2 * pages_per_chunk]
    (ksn_ref, vsn_ref, kwn_ref, vwn_ref, kwc_ref, vwc_ref, bcmp_ref, bsel_ref, bwin_ref, ov_ref, ex_ref,
     o_ref, sel_scr, m_scr, l_scr, acc_scr, q_scr, ocmp_scr) = refs[7 + 2 * pages_per_chunk:]
    del pt_ref
    c = pl.program_id(1)
    n_chunks = pl.num_programs(1)
    hg = HEADS_PER_GROUP
    nr = hg * n_tok
    scale = HEAD_DIM ** -0.5
    n_cmp_rows = kcmp_ref.shape[2]
    n_sb = ov_ref.shape[0]
    chunk = pages_per_chunk * page
    tok = lax.broadcasted_iota(jnp.int32, (N_KV * nr, 1), 0) % n_tok

    @pl.when(c == 0)
    def _():
        q_scr[...] = _rms_rows(q_ref[0], gq_ref[...]).astype(BF16)
        m_scr[...] = jnp.full(m_scr.shape, NEG, F32)
        l_scr[...] = jnp.zeros(l_scr.shape, F32)
        acc_scr[...] = jnp.zeros(acc_scr.shape, F32)
        ci = lax.broadcasted_iota(jnp.int32, (1, n_cmp_rows), 1)
        cmp_valid = ci < n_cmp_rows - 1
        psums = []
        for g in range(N_KV):
            rows = slice(g * nr, (g + 1) * nr)
            s = lax.dot_general(q_scr[rows], kcmp_ref[0, g], (((1,), (1,)), ((), ())),
                                preferred_element_type=F32)
            s = jnp.where(cmp_valid, s * scale + bcmp_ref[g], NEG)
            e = jnp.where(cmp_valid, jnp.exp(s - jnp.max(s, axis=1, keepdims=True)), 0.0)
            p = e / jnp.sum(e, axis=1, keepdims=True)
            ocmp_scr[rows] = jnp.dot(p.astype(BF16), vcmp_ref[0, g], preferred_element_type=F32)
            psum = p[0:n_tok]
            for h in range(1, hg):
                psum = psum + p[h * n_tok:(h + 1) * n_tok]
            psums.append(psum)
        rest = jnp.concatenate(psums, axis=0)
        imp_t = jnp.zeros((n_sb, N_KV * n_tok), F32)
        for _ in range(3):
            piece = rest.astype(BF16)
            rest = rest - piece.astype(F32)
            imp_t = imp_t + lax.dot_general(ov_ref[...], piece, (((1,), (1,)), ((), ())),
                                            preferred_element_type=F32)
        blk = lax.broadcasted_iota(jnp.int32, (n_sb, 1), 0)
        forced = (blk == 0) | (blk == n_sb - 1)
        score = jnp.where(forced, FORCE, imp_t)
        sel_t = _topk_rows(score, blk, min(N_SELECT, n_sb + 1) - 1)
        sel_f = jnp.where(sel_t, 1.0, 0.0)
        if N_KV * n_tok < LANES:
            sel_f = jnp.concatenate([sel_f, jnp.zeros((n_sb, LANES - N_KV * n_tok), F32)], axis=1)
        sel_f = sel_f.T
        for g in range(N_KV):
            sel_scr[g * nr:(g + 1) * nr] = jnp.tile(sel_f[g * n_tok:(g + 1) * n_tok], (hg, 1)).astype(BF16)

    def group_rows(ref, g):
        return ref[0, pl.ds(g, ref.shape[1] // N_KV, stride=N_KV), :].astype(BF16)

    def all_scores(k_of_group):
        return jnp.concatenate(
            [lax.dot_general(q_scr[g * nr:(g + 1) * nr], k_of_group(g), (((1,), (1,)), ((), ())),
                             preferred_element_type=F32) for g in range(N_KV)], axis=0)

    def all_values(e, v_of_group):
        eb = e.astype(BF16)
        return jnp.concatenate(
            [jnp.dot(eb[g * nr:(g + 1) * nr], v_of_group(g), preferred_element_type=F32) for g in range(N_KV)],
            axis=0)

    def online_update(sh, v_of_group):
        m_prev = m_scr[...]
        m_next = jnp.maximum(m_prev, jnp.max(sh, axis=1, keepdims=True))
        alpha = jnp.exp(m_prev - m_next)
        e = jnp.exp(sh - m_next[:, 0:1])
        l_scr[...] = alpha * l_scr[...] + jnp.sum(e, axis=1, keepdims=True)
        acc_scr[...] = alpha * acc_scr[...] + all_values(e, v_of_group)
        m_scr[...] = m_next

    k0 = pl.multiple_of(c * chunk, chunk)
    s = all_scores(lambda g: jnp.concatenate([group_rows(r, g) for r in kpages], axis=0))
    picked = jnp.dot(sel_scr[...], ex_ref[:, pl.ds(k0, chunk)], preferred_element_type=F32)
    sh = jnp.where(picked > 0.5, s * scale + bsel_ref[:, pl.ds(k0, chunk)], NEG)
    online_update(sh, lambda g: jnp.concatenate([group_rows(r, g) for r in vpages], axis=0))

    @pl.when(c == n_chunks - 1)
    def _():
        new_w = ksn_ref.shape[1] // N_KV
        jn = lax.broadcasted_iota(jnp.int32, (1, new_w), 1)
        new_valid = jn <= tok
        wbuf = kwc_ref.shape[1] // N_KV
        jw = lax.broadcasted_iota(jnp.int32, (1, wbuf), 1)
        dist_c = (past_len + tok) - (past_len - wbuf + jw)
        win_valid_c = dist_c < WINDOW
        s = all_scores(lambda g: group_rows(ksn_ref, g))
        sh = jnp.where(new_valid, s * scale + bsel_ref[:, pl.ds(past_len, new_w)], NEG)
        online_update(sh, lambda g: group_rows(vsn_ref, g))
        o_sel = acc_scr[...] / l_scr[...]
        s_c = all_scores(lambda g: group_rows(kwc_ref, g))
        s_n = all_scores(lambda g: group_rows(kwn_ref, g))
        s_c = jnp.where(win_valid_c, s_c * scale + bwin_ref[:, 0:wbuf], NEG)
        s_n = jnp.where(new_valid, s_n * scale + bwin_ref[:, wbuf:wbuf + new_w], NEG)
        mx = jnp.maximum(jnp.max(s_c, axis=1, keepdims=True), jnp.max(s_n, axis=1, keepdims=True))
        e_c = jnp.exp(s_c - mx)
        e_n = jnp.exp(s_n - mx)
        den = jnp.sum(e_c, axis=1, keepdims=True) + jnp.sum(e_n, axis=1, keepdims=True)
        o_win = (all_values(e_c, lambda g: group_rows(vwc_ref, g))
                 + all_values(e_n, lambda g: group_rows(vwn_ref, g))) / den
        sig_gl = _sigmoid(gl_ref[0])
        out = jnp.zeros((N_KV * nr, HEAD_DIM), F32)
        for br, o_br in enumerate((ocmp_scr[...], o_sel, o_win)):
            z = z_ref[0, br]
            out = out + sig_gl[br] * o_br * (z * _sigmoid(z))
        o_ref[0] = out.astype(BF16)


def _attn_sample(q_s, z_s, gl_s, g_q, kcmp, vcmp, ksel_pages, vsel_pages, page_table, ks_new, vs_new,
                 kw_new, vw_new, kw_cache, vw_cache, bcmp, bsel, bwin, overlap, expand, n_tok):
    nb = q_s.shape[0]
    n_pages = page_table.shape[1]
    page = ksel_pages.shape[1] // N_KV
    ppc = min(8, n_pages)
    n_chunks = n_pages // ppc
    nrow = q_s.shape[1]
    per_b = lambda shape: pl.BlockSpec((1,) + shape, lambda b, c, pt: (b,) + (0,) * len(shape))
    full = lambda shape: _resident(shape, lambda b, c, pt: (0,) * len(shape))
    page_spec = lambda i: pl.BlockSpec((1, page * N_KV, HEAD_DIM),
                                       lambda b, c, pt, i=i: (pt[b, c * ppc + i], 0, 0))
    in_specs = ([per_b(q_s.shape[1:]), per_b(z_s.shape[1:]), per_b(gl_s.shape[1:]), full((1, HEAD_DIM)),
                 per_b(kcmp.shape[1:]), per_b(vcmp.shape[1:])]
                + [page_spec(i) for i in range(ppc)] + [page_spec(i) for i in range(ppc)]
                + [per_b(ks_new.shape[1:])] * 4 + [per_b(kw_cache.shape[1:])] * 2
                + [full(bcmp.shape), full(bsel.shape), full(bwin.shape), full(overlap.shape), full(expand.shape)])
    grid_spec = pltpu.PrefetchScalarGridSpec(
        num_scalar_prefetch=1,
        grid=(nb, n_chunks),
        in_specs=in_specs,
        out_specs=pl.BlockSpec((1, nrow, HEAD_DIM), lambda b, c, pt: (b, 0, 0)),
        scratch_shapes=[pltpu.VMEM((nrow, expand.shape[0]), BF16),
                        pltpu.VMEM((nrow, LANES), F32), pltpu.VMEM((nrow, LANES), F32),
                        pltpu.VMEM((nrow, HEAD_DIM), F32), pltpu.VMEM((nrow, HEAD_DIM), BF16),
                        pltpu.VMEM((nrow, HEAD_DIM), F32)],
    )
    return pl.pallas_call(
        functools.partial(_attn_sample_kernel, n_tok=n_tok, pages_per_chunk=ppc, page=page,
                          past_len=n_pages * page),
        grid_spec=grid_spec,
        out_shape=jax.ShapeDtypeStruct((nb, nrow, HEAD_DIM), BF16),
        compiler_params=_cparams(("arbitrary", "arbitrary")),
        name="attn_sample",
    )(page_table, q_s, z_s, gl_s, g_q.reshape(1, HEAD_DIM), kcmp, vcmp,
      *([ksel_pages] * ppc), *([vsel_pages] * ppc), ks_new, vs_new, kw_new, vw_new, kw_cache, vw_cache,
      bcmp, bsel, bwin, overlap, expand)


def kernel(x_prompt, x_sample, p_prompt, p_sample, state_conv, state_h, cache_k_cmp, cache_v_cmp, cache_k_sel, cache_v_sel, cache_k_win, cache_v_win, page_table, norm_g, a_w_in, a_conv_w, a_conv_b, a_w_r, a_b_r, a_w_i, a_b_i, a_lam, a_w_out, kv_norm_g, w_kv, g_k_cmp, g_k_sel, g_k_win, cmp_pos, cmp_w1a, cmp_w1b, cmp_b1, cmp_w2, cmp_b2, b_w_in, b_g_q, b_w_out, rel_bias, ple_w_proj, ple_norm_g, ple_w_gate):
    bp, s_len, d = x_prompt.shape
    db, ds, _ = x_sample.shape
    n_pool, page = cache_k_sel.shape[:2]
    n_pages = page_table.shape[1]
    past_len = n_pages * page
    wbuf = cache_k_win.shape[1]
    hd = N_HEADS * HEAD_DIM
    assert bp == 1 and norm_g.shape[0] == 2 and a_w_in.shape[0] == 1 and b_w_in.shape[0] == 1
    assert d == hd and s_len % TK_SEL == 0 and s_len >= WINDOW + TQ and page % SEL_BLOCK == 0
    assert wbuf == WINDOW and ds <= SEL_BLOCK and past_len % TK_SEL == 0 and s_len // SEL_BLOCK <= HEAD_DIM

    w_in0 = a_w_in[0].astype(BF16)
    w_ri = jnp.concatenate([a_w_r[0], a_w_i[0]], axis=-1).astype(BF16)
    w_out0 = a_w_out[0].astype(BF16)
    w_gate = ple_w_gate.astype(BF16)
    w_proj = ple_w_proj.astype(BF16)
    w_kv_b = w_kv.astype(BF16)
    n_in1 = b_w_in.shape[2]
    n_in1_pad = -(-n_in1 // 640) * 640
    w_in1 = jnp.pad(b_w_in[0], ((0, 0), (0, n_in1_pad - n_in1))).astype(BF16)
    w_out1 = b_w_out[0].astype(BF16)
    w1 = jnp.concatenate([cmp_w1a, cmp_w1b], axis=-1).astype(BF16)
    w2 = cmp_w2.astype(BF16)
    pe = cmp_pos.reshape(2, 2, 1, CMP_STRIDE * HEAD_DIM)
    pe_hi = pe.astype(BF16)
    pe_lo = (pe - pe_hi.astype(F32)).astype(BF16)
    pe_rows = jnp.concatenate([pe_hi, pe_lo, jnp.zeros((2, 2, SUBLANES - 2, pe.shape[-1]), BF16)], axis=2)
    pe_rows = pe_rows.reshape(2, 2 * SUBLANES, pe.shape[-1])
    b1 = cmp_b1.reshape(2, 1, -1)
    b2 = cmp_b2.reshape(2, 1, HEAD_DIM)

    xp = x_prompt.reshape(s_len, d)
    xs = x_sample.reshape(db * ds, d)
    uz_p = _norm_matmul(xp, norm_g[0], w_in0, 1024)
    uz_s = _norm_matmul(xs, norm_g[0], w_in0, 1024)
    rg = (a_conv_w[0], a_conv_b[0], w_ri, a_b_r[0], a_b_i[0], a_lam[0])
    gated_p, h_p = _rglru_prompt(uz_p, jnp.zeros((SUBLANES, d), F32), jnp.zeros((1, d), F32), *rg)
    conv_p = uz_p[s_len - (CONV_W - 1):, :d].reshape(1, bp, CONV_W - 1, d)
    u_s = uz_s[:, :d].reshape(db, ds, d).transpose(1, 0, 2)
    z_s = uz_s[:, d:].reshape(db, ds, d).transpose(1, 0, 2)
    up_s = jnp.concatenate([state_conv[0].transpose(1, 0, 2), u_s], axis=0)
    gated_s, h_s = _rglru_sample(up_s, z_s, state_h[0], *rg)
    conv_s = up_s[ds:].transpose(1, 0, 2)[None]
    gated_s = gated_s.transpose(1, 0, 2).reshape(db * ds, d)
    ple0 = (w_proj[0], ple_norm_g[0], w_gate[0])
    h1_p = _outproj_ple(gated_p, xp, p_prompt[0].reshape(s_len, -1), w_out0, *ple0)
    h1_s = _outproj_ple(gated_s, xs, p_sample[0].reshape(db * ds, -1), w_out0, *ple0)

    kv_p = _kv_proj(h1_p, kv_norm_g, w_kv_b, g_k_sel, g_k_win)
    kv_s = _kv_proj(h1_s, kv_norm_g, w_kv_b, g_k_sel, g_k_win)
    kc_p, vc_p, ks_p, vs_p, kw_p, vw_p = kv_p[:6]
    kc_s, vc_s, ks_s, vs_s, kw_s, vw_s = kv_s[:6]

    p_pages = s_len // page
    ident = jnp.minimum(jnp.arange(p_pages + 1, dtype=jnp.int32), p_pages - 1)[None]
    cmp_w = (pe_rows, w1, b1, w2, b2, g_k_cmp)
    pieces = lambda x: x.reshape(-1, page // CMP_STRIDE, PIECE_ROWS, HEAD_DIM)
    kcmp_p, vcmp_p = _compress(pieces(kc_p), pieces(vc_p), ident, *cmp_w)
    table_ext = jnp.concatenate([page_table, page_table[:, -1:]], axis=1)
    kcmp_s, vcmp_s = _compress(pieces(cache_k_cmp), pieces(cache_v_cmp), table_ext, *cmp_w)

    proj_p = _norm_matmul(h1_p, norm_g[1], w_in1, 640)
    proj_s = _norm_matmul(h1_s, norm_g[1], w_in1, 640)

    def gate_logits(proj, n_rows):
        gl = proj[:, 4 * hd:4 * hd + 3 * N_HEADS].reshape(n_rows, 3, N_KV, HEADS_PER_GROUP)
        return gl.transpose(2, 0, 1, 3).reshape(N_KV, n_rows, 3 * HEADS_PER_GROUP)

    n_qt = s_len // TQ
    n_cmp_rows = s_len // CMP_STRIDE
    bank_w = max(TK_SEL, WINDOW + TQ)
    bank_k = min(-(-(_first_constant_distance() + bank_w) // LANES) * LANES, s_len)
    sb, wb, rr = _bias_banks(rel_bias, n_qt, n_cmp_rows // SUBLANES, bank_k, bank_w)
    gl_p = jnp.pad(gate_logits(proj_p, s_len), ((0, 0), (0, 0), (0, LANES - 3 * HEADS_PER_GROUP)))
    o_p = _attn_prompt(proj_p, gl_p, b_g_q[0], kcmp_p, vcmp_p, kv_p[6], kv_p[7], kv_p[8], kv_p[9],
                       rr, sb, wb, bank_k)

    def head_rows(x):
        return x.reshape(db, ds, N_HEADS, HEAD_DIM).transpose(0, 2, 1, 3).reshape(db, N_HEADS * ds, HEAD_DIM)

    q_s = head_rows(proj_s[:, :hd])
    zz_s = jnp.stack([head_rows(proj_s[:, (1 + br) * hd:(2 + br) * hd]) for br in range(3)], axis=1)
    gl_s = proj_s[:, 4 * hd:4 * hd + 3 * N_HEADS].reshape(db, ds, 3, N_HEADS).transpose(0, 2, 3, 1)
    gl_s = gl_s.reshape(db, 3, N_HEADS * ds, 1)
    new_w = LANES
    group_rows = lambda x: x.reshape(db, -1, HEAD_DIM)
    pad_new = lambda x: jnp.pad(group_rows(x), ((0, 0), (0, (new_w - ds) * N_KV), (0, 0)))
    n_cmp_s = past_len // CMP_STRIDE
    bsel = _query_rows_bias(rel_bias, past_len, ds, past_len + new_w).reshape(N_HEADS * ds, -1)
    bwin = bsel[:, past_len - wbuf:]
    bcmp = bsel[:, CMP_BLOCK - 1::CMP_STRIDE][:, :n_cmp_s].reshape(N_KV, HEADS_PER_GROUP * ds, n_cmp_s)
    n_sb_s = past_len // SEL_BLOCK
    ci = jnp.arange(n_cmp_s)[:, None] * CMP_STRIDE
    sj = jnp.arange(n_sb_s)[None, :] * SEL_BLOCK
    overlap = ((ci < sj + SEL_BLOCK) & (ci + CMP_BLOCK > sj) & (ci < past_len - CMP_STRIDE)).T.astype(BF16)
    expand_s = (jnp.arange(past_len)[None, :] // SEL_BLOCK == jnp.arange(n_sb_s)[:, None]).astype(BF16)
    o_s = _attn_sample(q_s, zz_s, gl_s, b_g_q[0], kcmp_s, vcmp_s,
                       cache_k_sel.reshape(n_pool, -1, HEAD_DIM), cache_v_sel.reshape(n_pool, -1, HEAD_DIM),
                       page_table, pad_new(ks_s), pad_new(vs_s), pad_new(kw_s), pad_new(vw_s),
                       group_rows(cache_k_win), group_rows(cache_v_win),
                       bcmp, bsel, bwin, overlap, expand_s, ds)
    o_s = o_s.reshape(db, N_HEADS, ds, HEAD_DIM).transpose(0, 2, 1, 3).reshape(db * ds, hd)

    ple1 = (w_proj[1], ple_norm_g[1], w_gate[1])
    y_p = _outproj_ple(o_p, h1_p, p_prompt[1].reshape(s_len, -1), w_out1, *ple1)
    y_s = _outproj_ple(o_s, h1_s, p_sample[1].reshape(db * ds, -1), w_out1, *ple1)

    rows4 = lambda x, b, t: x.reshape(b, t, N_KV, HEAD_DIM)
    wk_p = min(WINDOW, s_len)
    kwin_s = jnp.concatenate([cache_k_win, rows4(kw_s, db, ds)], axis=1)[:, -WINDOW:]
    vwin_s = jnp.concatenate([cache_v_win, rows4(vw_s, db, ds)], axis=1)[:, -WINDOW:]
    return (y_p.reshape(bp, s_len, d), y_s.reshape(db, ds, d),
            conv_p, h_p.reshape(1, bp, d),
            rows4(kc_p, bp, s_len), rows4(vc_p, bp, s_len), rows4(ks_p, bp, s_len), rows4(vs_p, bp, s_len),
            rows4(kw_p, bp, s_len)[:, -wk_p:], rows4(vw_p, bp, s_len)[:, -wk_p:],
            conv_s, h_s.reshape(1, db, d),
            rows4(kc_s, db, ds), rows4(vc_s, db, ds), rows4(ks_s, db, ds), rows4(vs_s, db, ds),
            kwin_s, vwin_s)
```

```python
import functools
import math

import jax
import jax.numpy as jnp
from jax import lax
from jax.experimental import pallas as pl
from jax.experimental.pallas import tpu as pltpu

F32 = jnp.float32
BF16 = jnp.bfloat16

N_RNN_BLOCKS = 8
CONV_W = 4
LRU_C = 8.0
N_HEADS = 16
HEAD_DIM = 128
N_KV = 4
HEADS_PER_GROUP = N_HEADS // N_KV
GROUP_W = HEADS_PER_GROUP * HEAD_DIM
KV_W = N_KV * HEAD_DIM
CMP_BLOCK = 32
CMP_STRIDE = 16
SEL_BLOCK = 64
N_SELECT = 16
WINDOW = 512
N_BUCKETS = 32
MAX_DISTANCE = 4096
EPS = 1e-6
NEG = -1e30
FORCE = 1e9
REMOVED = -3e38
LOG2E = math.log2(math.e)

LANES = 128
SUBLANES = 8
VMEM_LIMIT = 56 * 1024 * 1024

TQ = 128
TK_SEL = 512
CMP_PAGES_PER_STEP = 16


def _cparams(sem):
    return pltpu.CompilerParams(dimension_semantics=sem, vmem_limit_bytes=VMEM_LIMIT)


def _resident(shape, index_map):
    return pl.BlockSpec(shape, index_map, pipeline_mode=pl.Buffered(1))


def _rms_rows(x, g):
    return x * lax.rsqrt(jnp.mean(x * x, axis=-1, keepdims=True) + EPS) * g


def _sigmoid(x):
    return 1.0 / (1.0 + jnp.exp(-x))


def _norm_matmul_kernel(x_ref, g_ref, w_ref, *rest):
    if len(rest) == 4:
        w2_ref, o_ref, o2_ref, xn_ref = rest
    else:
        (o_ref, xn_ref), w2_ref, o2_ref = rest, None, None

    @pl.when(pl.program_id(1) == 0)
    def _():
        xn_ref[...] = _rms_rows(x_ref[...], g_ref[...]).astype(BF16)
        if w2_ref is not None:
            o2_ref[...] = jnp.dot(xn_ref[...], w2_ref[...], preferred_element_type=F32)

    o_ref[...] = jnp.dot(xn_ref[...], w_ref[...], preferred_element_type=F32)


def _norm_matmul(x, g, w, tn, w2=None):
    m, d = x.shape
    n = w.shape[1]
    tm = min(m, 1024)
    in_specs = [pl.BlockSpec((tm, d), lambda i, j: (i, 0)),
                pl.BlockSpec((1, d), lambda i, j: (0, 0)),
                pl.BlockSpec((d, tn), lambda i, j: (0, j))]
    out_specs = [pl.BlockSpec((tm, tn), lambda i, j: (i, j))]
    out_shape = [jax.ShapeDtypeStruct((m, n), F32)]
    args = [x, g.reshape(1, d), w]
    if w2 is not None:
        in_specs.append(_resident(w2.shape, lambda i, j: (0, 0)))
        out_specs.append(pl.BlockSpec((tm, w2.shape[1]), lambda i, j: (i, 0)))
        out_shape.append(jax.ShapeDtypeStruct((m, w2.shape[1]), F32))
        args.append(w2)
    outs = pl.pallas_call(
        _norm_matmul_kernel,
        grid=(m // tm, n // tn),
        in_specs=in_specs,
        out_specs=out_specs,
        out_shape=out_shape,
        scratch_shapes=[pltpu.VMEM((tm, d), BF16)],
        compiler_params=_cparams(("parallel", "arbitrary")),
        name="norm_matmul",
    )(*args)
    return outs if w2 is not None else outs[0]


def _lru_gates(xc, wri_ref, br_ref, bi_ref, sp_ref, n, bw):
    sl = slice(n * bw, (n + 1) * bw)
    ri = jnp.dot(xc.astype(BF16), wri_ref[n], preferred_element_type=F32)
    r = _sigmoid(ri[:, :bw] + br_ref[:, sl])
    i = _sigmoid(ri[:, bw:] + bi_ref[:, sl])
    log_a = (-LRU_C) * r * sp_ref[:, sl]
    a = jnp.exp(log_a)
    b = jnp.sqrt(1.0 - jnp.exp(2.0 * log_a)) * (i * xc)
    return a, b


def _softplus(x):
    return jnp.maximum(x, 0.0) + jnp.log(1.0 + jnp.exp(-jnp.abs(x)))


def _rglru_prompt_kernel(u_ref, z_ref, cinit_ref, hinit_ref, cw_ref, cb_ref, wri_ref, br_ref, bi_ref,
                         lam_ref, g_ref, hlast_ref, ubuf, a_scr, b_scr, hs_scr, hcar, sp_scr):
    tm, d = u_ref.shape
    bw = d // N_RNN_BLOCKS

    @pl.when(pl.program_id(0) == 0)
    def _():
        ubuf[0:SUBLANES, :] = cinit_ref[...]
        hcar[...] = hinit_ref[...]
        sp_scr[...] = _softplus(-lam_ref[...])

    ubuf[SUBLANES:, :] = u_ref[...]
    for n in range(N_RNN_BLOCKS):
        sl = slice(n * bw, (n + 1) * bw)
        up = ubuf[:, sl]
        xc = cb_ref[:, sl] + up * cw_ref[CONV_W - 1:CONV_W, sl]
        for k in range(CONV_W - 1):
            xc = xc + pltpu.roll(up, CONV_W - 1 - k, axis=0) * cw_ref[k:k + 1, sl]
        xc = xc[SUBLANES:]
        a, b = _lru_gates(xc, wri_ref, br_ref, bi_ref, sp_scr, n, bw)
        a_scr[:, sl] = a
        b_scr[:, sl] = b
    ubuf[0:SUBLANES, :] = u_ref[tm - SUBLANES:tm, :]

    def step(t, h):
        h = a_scr[pl.ds(t, 1), :] * h + b_scr[pl.ds(t, 1), :]
        hs_scr[pl.ds(t, 1), :] = h
        return h

    h = lax.fori_loop(0, tm, step, hcar[...], unroll=8)
    hcar[...] = h
    hlast_ref[...] = h
    z = z_ref[...]
    g_ref[...] = (hs_scr[...] * (z * _sigmoid(z))).astype(BF16)


def _rglru_prompt(uz, conv_init, h_init, cw, cb, wri, br, bi, lam):
    t, d2 = uz.shape
    d = d2 // 2
    tm = min(t, 256)
    full = lambda shape: pl.BlockSpec(shape, lambda i: (0,) * len(shape))
    return pl.pallas_call(
        _rglru_prompt_kernel,
        grid=(t // tm,),
        in_specs=[pl.BlockSpec((tm, d), lambda i: (i, 0)),
                  pl.BlockSpec((tm, d), lambda i: (i, 1)),
                  full((SUBLANES, d)), full((1, d)), full((CONV_W, d)), full((1, d)),
                  full(wri.shape), full((1, d)), full((1, d)), full((1, d))],
        out_specs=[pl.BlockSpec((tm, d), lambda i: (i, 0)), full((1, d))],
        out_shape=[jax.ShapeDtypeStruct((t, d), BF16), jax.ShapeDtypeStruct((1, d), F32)],
        scratch_shapes=[pltpu.VMEM((tm + SUBLANES, d), F32), pltpu.VMEM((tm, d), F32),
                        pltpu.VMEM((tm, d), F32), pltpu.VMEM((tm, d), F32),
                        pltpu.VMEM((1, d), F32), pltpu.VMEM((1, d), F32)],
        compiler_params=_cparams(("arbitrary",)),
        name="rglru_prompt",
    )(uz, uz, conv_init, h_init, cw, cb.reshape(1, d), wri, br.reshape(1, d), bi.reshape(1, d),
      lam.reshape(1, d))


def _rglru_sample_kernel(up_ref, z_ref, h0_ref, cw_ref, cb_ref, wri_ref, br_ref, bi_ref, lam_ref,
                         g_ref, hlast_ref, sp_scr):
    n_t = z_ref.shape[0]
    d = z_ref.shape[2]
    bw = d // N_RNN_BLOCKS
    sp_scr[...] = _softplus(-lam_ref[...])
    for n in range(N_RNN_BLOCKS):
        sl = slice(n * bw, (n + 1) * bw)
        h = h0_ref[:, sl]
        for t in range(n_t):
            xc = cb_ref[:, sl]
            for k in range(CONV_W):
                xc = xc + up_ref[t + k, :, sl] * cw_ref[k:k + 1, sl]
            a, b = _lru_gates(xc, wri_ref, br_ref, bi_ref, sp_scr, n, bw)
            h = a * h + b
            z = z_ref[t, :, sl]
            g_ref[t, :, sl] = (h * (z * _sigmoid(z))).astype(BF16)
        hlast_ref[:, sl] = h


def _rglru_sample(up, z, h0, cw, cb, wri, br, bi, lam):
    n_t, nb, d = z.shape
    return pl.pallas_call(
        _rglru_sample_kernel,
        out_shape=[jax.ShapeDtypeStruct((n_t, nb, d), BF16), jax.ShapeDtypeStruct((nb, d), F32)],
        scratch_shapes=[pltpu.VMEM((1, d), F32)],
        compiler_params=pltpu.CompilerParams(vmem_limit_bytes=VMEM_LIMIT),
        name="rglru_sample",
    )(up, z, h0, cw, cb.reshape(1, d), wri, br.reshape(1, d), bi.reshape(1, d), lam.reshape(1, d))


def _outproj_ple_kernel(a_ref, res_ref, p_ref, wo_ref, wp_ref, gn_ref, wg_ref, o_ref, h_scr, hn_scr):
    tm, d = res_ref.shape
    ch = min(d, 512)
    a = a_ref[...]
    ssq = jnp.zeros((tm, 1), F32)
    for c in range(d // ch):
        sl = slice(c * ch, (c + 1) * ch)
        h = res_ref[:, sl] + jnp.dot(a, wo_ref[:, sl], preferred_element_type=F32)
        h_scr[:, sl] = h
        ssq = ssq + jnp.sum(h * h, axis=-1, keepdims=True)
    inv = lax.rsqrt(ssq * (1.0 / d) + EPS)
    for c in range(d // ch):
        sl = slice(c * ch, (c + 1) * ch)
        hn_scr[:, sl] = (h_scr[:, sl] * inv * gn_ref[:, sl]).astype(BF16)
    pb = p_ref[...].astype(BF16)
    hn = hn_scr[...]
    for c in range(d // ch):
        sl = slice(c * ch, (c + 1) * ch)
        gate = _sigmoid(jnp.dot(hn, wg_ref[:, sl], preferred_element_type=F32))
        pp = jnp.dot(pb, wp_ref[:, sl], preferred_element_type=F32)
        o_ref[:, sl] = h_scr[:, sl] + pp * gate


def _outproj_ple(a, res, p, w_out, w_proj, g_norm, w_gate):
    m, d = res.shape
    pd = p.shape[1]
    tm = min(m, 512)
    return pl.pallas_call(
        _outproj_ple_kernel,
        grid=(m // tm,),
        in_specs=[pl.BlockSpec((tm, a.shape[1]), lambda i: (i, 0)),
                  pl.BlockSpec((tm, d), lambda i: (i, 0)),
                  pl.BlockSpec((tm, pd), lambda i: (i, 0)),
                  _resident(w_out.shape, lambda i: (0, 0)),
                  _resident(w_proj.shape, lambda i: (0, 0)),
                  _resident((1, d), lambda i: (0, 0)),
                  _resident(w_gate.shape, lambda i: (0, 0))],
        out_specs=pl.BlockSpec((tm, d), lambda i: (i, 0)),
        out_shape=jax.ShapeDtypeStruct((m, d), F32),
        scratch_shapes=[pltpu.VMEM((tm, d), F32), pltpu.VMEM((tm, d), BF16)],
        compiler_params=_cparams(("parallel",)),
        name="outproj_ple",
    )(a, res, p, w_out, w_proj, g_norm.reshape(1, d), w_gate)


AUG_W = 2 * HEAD_DIM
MASK_OFF = -2.0 ** 60


def _kv_proj_kernel(x_ref, g_ref, w_ref, gsel_ref, gwin_ref,
                    kc_ref, vc_ref, ks_ref, vs_ref, kw_ref, vw_ref, ksa_ref, vsa_ref, kwb_ref, vwa_ref):
    tm = x_ref.shape[0]
    xn = _rms_rows(x_ref[...], g_ref[...]).astype(BF16)
    outs = (kc_ref, vc_ref, ks_ref, vs_ref, kw_ref, vw_ref)
    head_gain = {2: gsel_ref, 4: gwin_ref}
    pos = pl.program_id(0) * tm + lax.broadcasted_iota(jnp.int32, (tm, HEAD_DIM), 0)
    blk = lax.broadcasted_iota(jnp.int32, (tm, HEAD_DIM), 1)
    block_mask = jnp.where(blk == pos // SEL_BLOCK, MASK_OFF, 0.0).astype(BF16)
    ones = jnp.ones((tm, HEAD_DIM), BF16)
    for s in range(6):
        y = jnp.dot(xn, w_ref[:, s * KV_W:(s + 1) * KV_W], preferred_element_type=F32)
        for g in range(N_KV):
            yg = y[:, g * HEAD_DIM:(g + 1) * HEAD_DIM]
            if s in head_gain:
                yg = _rms_rows(yg, head_gain[s][...])
            outs[s][pl.ds(g, tm, stride=N_KV), :] = yg
            lo = slice(g * AUG_W, g * AUG_W + HEAD_DIM)
            hi = slice(g * AUG_W + HEAD_DIM, (g + 1) * AUG_W)
            if s == 2:
                ksa_ref[:, lo] = yg.astype(BF16)
                ksa_ref[:, hi] = block_mask
            elif s == 3:
                vsa_ref[:, lo] = yg.astype(BF16)
                vsa_ref[:, hi] = ones
            elif s == 4:
                kwb_ref[:, g * HEAD_DIM:(g + 1) * HEAD_DIM] = yg.astype(BF16)
            elif s == 5:
                vwa_ref[:, lo] = yg.astype(BF16)
                vwa_ref[:, hi] = ones


def _kv_proj(x, g, w, g_sel, g_win):
    m, d = x.shape
    tm = min(m, 512)
    row = lambda i: (i, 0)
    aug = N_KV * AUG_W
    return pl.pallas_call(
        _kv_proj_kernel,
        grid=(m // tm,),
        in_specs=[pl.BlockSpec((tm, d), row), _resident((1, d), lambda i: (0, 0)),
                  _resident(w.shape, lambda i: (0, 0)),
                  _resident((1, HEAD_DIM), lambda i: (0, 0)), _resident((1, HEAD_DIM), lambda i: (0, 0))],
        out_specs=([pl.BlockSpec((tm * N_KV, HEAD_DIM), row)] * 6
                   + [pl.BlockSpec((tm, aug), row), pl.BlockSpec((tm, aug), row),
                      pl.BlockSpec((tm, KV_W), row), pl.BlockSpec((tm, aug), row)]),
        out_shape=([jax.ShapeDtypeStruct((m * N_KV, HEAD_DIM), F32)] * 6
                   + [jax.ShapeDtypeStruct((m, aug), BF16), jax.ShapeDtypeStruct((m, aug), BF16),
                      jax.ShapeDtypeStruct((m, KV_W), BF16), jax.ShapeDtypeStruct((m, aug), BF16)]),
        compiler_params=_cparams(("parallel",)),
        name="kv_proj",
    )(x, g.reshape(1, d), w, g_sel.reshape(1, HEAD_DIM), g_win.reshape(1, HEAD_DIM))


PIECE_ROWS = CMP_STRIDE * N_KV
PIECE_PITCH = PIECE_ROWS + SUBLANES


def _piece_rows(buf, r, n):
    flat = buf.reshape(buf.shape[0] * PIECE_PITCH, HEAD_DIM)
    return flat[pl.ds(r, n, stride=PIECE_PITCH), :]


def _compress_kernel(pt_ref, kc_hbm, vc_hbm, pe_ref, w1_ref, b1_ref, w2_ref, b2_ref, gk_ref,
                     kcmp_ref, vcmp_ref, kbuf, vbuf, sem, lhs, bias_scr, *, n_quarters, pages_per_step):
    s = pl.program_id(0)
    n_steps = pl.num_programs(0)
    pieces_per_page = kc_hbm.shape[1]
    n_piece = pages_per_step * pieces_per_page
    n_load = n_piece + SUBLANES
    rows_g = n_piece + 2 * SUBLANES
    hidden = w2_ref.shape[1]

    def copies(step, slot):
        b = step // n_quarters
        q = step % n_quarters
        out = []
        for hbm, buf, which in ((kc_hbm, kbuf, 0), (vc_hbm, vbuf, 1)):
            for p in range(pages_per_step):
                pg = pt_ref[b, q * pages_per_step + p]
                dst = buf.at[slot, pl.ds(p * pieces_per_page, pieces_per_page), pl.ds(0, PIECE_ROWS), :]
                out.append(pltpu.make_async_copy(hbm.at[pg], dst, sem.at[slot, which]))
            pg = pt_ref[b, q * pages_per_step + pages_per_step]
            out.append(pltpu.make_async_copy(hbm.at[pg, 0], buf.at[slot, n_piece, pl.ds(0, PIECE_ROWS), :],
                                             sem.at[slot, which]))
        return out

    slot = s % 2

    @pl.when(s == 0)
    def _():
        for c in copies(0, 0):
            c.start()
        kbuf[:, n_piece + 1:] = jnp.zeros((2, SUBLANES - 1) + kbuf.shape[2:], F32)
        vbuf[:, n_piece + 1:] = jnp.zeros((2, SUBLANES - 1) + vbuf.shape[2:], F32)
        lhs[...] = jnp.zeros(lhs.shape, BF16)
        for which in range(2):
            pw = jnp.dot(pe_ref[which], w1_ref[which], preferred_element_type=F32)
            bias_scr[which] = (jnp.sum(pw[:SUBLANES, :hidden], axis=0, keepdims=True)
                               + jnp.sum(pw[SUBLANES:, hidden:], axis=0, keepdims=True) + b1_ref[which])

    @pl.when(s + 1 < n_steps)
    def _():
        for c in copies(s + 1, 1 - slot):
            c.start()

    for c in copies(s, slot):
        c.wait()

    for which, buf, out_ref in ((0, kbuf, kcmp_ref), (1, vbuf, vcmp_ref)):
        for g in range(N_KV):
            for pos in range(CMP_STRIDE):
                x = _piece_rows(buf.at[slot], pos * N_KV + g, n_load)
                lhs[which, g * rows_g:g * rows_g + n_load, pos * HEAD_DIM:(pos + 1) * HEAD_DIM] = x.astype(BF16)
        hab = jnp.dot(lhs[which], w1_ref[which], preferred_element_type=F32)
        for g in range(N_KV):
            blk = hab[g * rows_g:(g + 1) * rows_g]
            nxt = pltpu.roll(blk[:, hidden:], rows_g - 1, axis=0)
            hid = blk[:n_piece, :hidden] + nxt[:n_piece] + bias_scr[which]
            hid = hid * _sigmoid(hid)
            y = jnp.dot(hid.astype(BF16), w2_ref[which], preferred_element_type=F32) + b2_ref[which]
            if which == 0:
                y = _rms_rows(y, gk_ref[...])
            out_ref[0, g] = y.astype(BF16)


def _compress(kc_pages, vc_pages, table, pe_rows, w1, b1, w2, b2, g_k):
    nb, n_pages = table.shape[0], table.shape[1] - 1
    ppp = kc_pages.shape[1]
    pps = min(CMP_PAGES_PER_STEP, n_pages)
    n_quarters = n_pages // pps
    n_piece = pps * ppp
    kdim = CMP_STRIDE * HEAD_DIM
    hidden = w2.shape[1]
    full = lambda shape: pl.BlockSpec(shape, lambda s, pt: (0,) * len(shape))
    out_spec = pl.BlockSpec((1, N_KV, n_piece, HEAD_DIM), lambda s, pt: (s // n_quarters, 0, s % n_quarters, 0))
    out_sd = jax.ShapeDtypeStruct((nb, N_KV, n_pages * ppp, HEAD_DIM), BF16)
    buf = pltpu.VMEM((2, n_piece + SUBLANES, PIECE_PITCH, HEAD_DIM), F32)
    lhs = pltpu.VMEM((2, N_KV * (n_piece + 2 * SUBLANES), kdim), BF16)
    grid_spec = pltpu.PrefetchScalarGridSpec(
        num_scalar_prefetch=1,
        grid=(nb * n_quarters,),
        in_specs=[pl.BlockSpec(memory_space=pl.ANY), pl.BlockSpec(memory_space=pl.ANY),
                  full(pe_rows.shape), full(w1.shape), full(b1.shape), full(w2.shape), full(b2.shape),
                  full((1, HEAD_DIM))],
        out_specs=[out_spec, out_spec],
        scratch_shapes=[buf, buf, pltpu.SemaphoreType.DMA((2, 2)), lhs, pltpu.VMEM((2, 1, hidden), F32)],
    )
    return pl.pallas_call(
        functools.partial(_compress_kernel, n_quarters=n_quarters, pages_per_step=pps),
        grid_spec=grid_spec,
        out_shape=[out_sd, out_sd],
        compiler_params=_cparams(("arbitrary",)),
        name="compress",
    )(table, kc_pages, vc_pages, pe_rows, w1, b1, w2, b2, g_k.reshape(1, HEAD_DIM))


def _rel_bucket(dist):
    n = jnp.maximum(dist, 0)
    max_exact = N_BUCKETS // 2
    nf = jnp.maximum(n, 1).astype(F32)
    large = max_exact + (jnp.log(nf / max_exact) / math.log(MAX_DISTANCE / max_exact)
                         * (N_BUCKETS - max_exact)).astype(jnp.int32)
    large = jnp.minimum(large, N_BUCKETS - 1)
    return jnp.where(n < max_exact, n, large)


def _bias_table(rel_bias, dist, masked, gain=1.0):
    dist = lax.optimization_barrier(dist)
    onehot = (_rel_bucket(dist)[None, :] == jnp.arange(N_BUCKETS)[:, None]).astype(F32)
    table = jnp.dot(rel_bias.astype(F32).T * gain, onehot, precision=lax.Precision.HIGHEST)
    return jnp.where(dist[None, :] >= 0, table, NEG) if masked else table


def _bias_bank_kernel(gs_ref, gw_ref, g2_ref, sb_ref, wb_ref, rr_ref):
    for g_ref, o_ref in ((gs_ref, sb_ref), (gw_ref, wb_ref)):
        wp = g_ref.shape[2]
        x = jnp.broadcast_to(g_ref[0], (TQ, wp))
        o_ref[0] = pltpu.roll(x, wp - (TQ - 1), 1, stride=1, stride_axis=0)[:, :wp - TQ]

    n_blocks = g2_ref.shape[1]
    for il in range(SUBLANES):
        left = CMP_STRIDE * (SUBLANES - 1 - il)
        y = pltpu.roll(g2_ref[0], (2 * TQ - left) % (2 * TQ), 1)
        rr_ref[0, pl.ds(il, n_blocks, stride=SUBLANES), :] = y[:, :TQ]


def _bias_banks(rel_bias, n_qt, n_row_blocks, bank_k, bank_w):
    nh = rel_bias.shape[1]
    wk = WINDOW + TQ
    gs = _bias_table(rel_bias, bank_k + TQ - 1 - jnp.arange(bank_k + bank_w + TQ), True, LOG2E)
    dw = WINDOW + TQ - 1 - jnp.arange(WINDOW + wk + TQ)
    gw = _bias_table(rel_bias, jnp.where(dw < WINDOW, dw, -1), True, LOG2E)
    n_blocks = -(-(n_qt + n_row_blocks - 1) // SUBLANES) * SUBLANES
    d2 = (TQ * (n_qt - 1 - jnp.arange(n_blocks))[:, None] + jnp.arange(2 * TQ)[None, :]
          - (CMP_BLOCK - 1) - CMP_STRIDE * (SUBLANES - 1))
    g2 = _bias_table(rel_bias, d2.reshape(-1), True, LOG2E).reshape(nh, n_blocks, 2 * TQ)
    spec3 = lambda a: pl.BlockSpec((1,) + a.shape[1:], lambda h: (h, 0, 0))
    outs = [jax.ShapeDtypeStruct((nh, TQ, bank_k + bank_w), F32), jax.ShapeDtypeStruct((nh, TQ, WINDOW + wk), F32),
            jax.ShapeDtypeStruct((nh, n_blocks * SUBLANES, TQ), F32)]
    gs, gw = gs[:, None], gw[:, None]
    return pl.pallas_call(
        _bias_bank_kernel,
        grid=(nh,),
        in_specs=[spec3(gs), spec3(gw), spec3(g2)],
        out_specs=[spec3(o) for o in outs],
        out_shape=outs,
        compiler_params=_cparams(("parallel",)),
        name="bias_banks",
    )(gs, gw, g2)


def _query_rows_bias(rel_bias, t0, n_tok, n_keys):
    dist = (t0 + jnp.arange(n_tok)[:, None] - jnp.arange(n_keys)[None, :]).reshape(-1)
    return _bias_table(rel_bias, dist, False).reshape(-1, n_tok, n_keys)


def _first_constant_distance():
    ratio = MAX_DISTANCE / (N_BUCKETS // 2)
    return int(math.ceil((N_BUCKETS // 2) * ratio ** ((N_BUCKETS // 2 - 1) / (N_BUCKETS // 2)))) + 2


def _topk_rows(score, blk, k_top):
    n_blk = score.shape[0]
    for _ in range(k_top):
        m = jnp.max(score, axis=0, keepdims=True)
        idx = jnp.min(jnp.where(score == m, blk, n_blk), axis=0, keepdims=True)
        score = jnp.where(blk == idx, REMOVED, score)
    return score == REMOVED


def _attn_prompt_kernel(q_ref, z0_ref, z1_ref, z2_ref, gl_ref, gq_ref, kcmp_ref, vcmp_ref,
                        ksa_ref, vsa_ref, kw_ref, vwa_ref, rr_ref, sb_ref, wb_ref, o_ref,
                        q_scr, psum_scr, ocmp_scr, owin_scr, m_scr, acc_scr, s0_scr, s1_scr, *, n_qtiles, bank_k):
    qi = pl.program_id(1)
    q0 = qi * TQ
    n_cmp_rows = kcmp_ref.shape[1]
    n_sb = n_cmp_rows * CMP_STRIDE // SEL_BLOCK
    hg = HEADS_PER_GROUP
    q_gain = HEAD_DIM ** -0.5 * LOG2E
    wk = WINDOW + TQ

    def per_tile_branches(t):
        tile = qi + t
        t0 = tile * TQ
        for h in range(hg):
            qh = _rms_rows(q_ref[t * TQ:(t + 1) * TQ, h * HEAD_DIM:(h + 1) * HEAD_DIM], gq_ref[...]) * q_gain
            q_scr[t, h * TQ:(h + 1) * TQ, 0:HEAD_DIM] = qh.astype(BF16)
        q_all = q_scr[t, :, 0:HEAD_DIM]

        s_t = lax.dot_general(kcmp_ref[0], q_all, (((1,), (1,)), ((), ())), preferred_element_type=F32)
        rr_start = pl.multiple_of((n_qtiles - 1 - tile) * SUBLANES, SUBLANES)
        psum = jnp.zeros((n_cmp_rows, TQ), F32)
        for h in range(hg):
            sh = s_t[:, h * TQ:(h + 1) * TQ] + rr_ref[h, pl.ds(rr_start, n_cmp_rows), :]
            mx = jnp.maximum(jnp.max(sh, axis=0, keepdims=True), 0.5 * NEG)
            e = jnp.exp2(sh - mx)
            den = jnp.sum(e, axis=0, keepdims=True)
            p = e * (1.0 / jnp.where(den > 0.0, den, 1.0))
            psum = psum + p
            ocmp_scr[t, h * TQ:(h + 1) * TQ, :] = lax.dot_general(
                p.astype(BF16), vcmp_ref[0], (((0,), (0,)), ((), ())), preferred_element_type=F32)

        w0 = pl.multiple_of(jnp.maximum(t0 - WINDOW, 0), LANES)
        s_w = lax.dot_general(q_all, kw_ref[pl.ds(w0, wk), :], (((1,), (1,)), ((), ())),
                              preferred_element_type=F32)
        vwin = vwa_ref[pl.ds(w0, wk), :]
        wb_start = pl.multiple_of(WINDOW - (t0 - w0), LANES)
        for h in range(hg):
            sh = s_w[h * TQ:(h + 1) * TQ] + wb_ref[h, :, pl.ds(wb_start, wk)]
            e = jnp.exp2(sh - jnp.max(sh, axis=1, keepdims=True))
            o_aug = jnp.dot(e.astype(BF16), vwin, preferred_element_type=F32)
            owin_scr[t, h * TQ:(h + 1) * TQ, :] = o_aug[:, 0:HEAD_DIM] / o_aug[:, HEAD_DIM:AUG_W]

        psum_scr[t, 0:SUBLANES, :] = jnp.zeros((SUBLANES, TQ), F32)
        psum_scr[t, SUBLANES:, :] = psum
        ratio = SEL_BLOCK // CMP_STRIDE
        imp = psum_scr[t, pl.ds(SUBLANES - 1, n_sb, stride=ratio), :]
        for k in range(ratio):
            imp = imp + psum_scr[t, pl.ds(SUBLANES + k, n_sb, stride=ratio), :]
        blk = lax.broadcasted_iota(jnp.int32, (n_sb, 1), 0)
        cur = (t0 + lax.broadcasted_iota(jnp.int32, (1, TQ), 1)) // SEL_BLOCK
        blk_ok = blk <= cur
        forced = (blk == 0) | (blk == cur) | (blk == cur - 1)
        score = jnp.where(blk_ok, jnp.where(forced, FORCE, imp), -FORCE)
        sel_t = _topk_rows(score, blk, min(N_SELECT, n_sb)) & blk_ok
        not_sel = jnp.where(sel_t, 0.0, 1.0).T
        if n_sb < HEAD_DIM:
            not_sel = jnp.concatenate([not_sel, jnp.zeros((TQ, HEAD_DIM - n_sb), F32)], axis=1)
        for h in range(hg):
            q_scr[t, h * TQ:(h + 1) * TQ, HEAD_DIM:AUG_W] = not_sel.astype(BF16)

    per_tile_branches(0)
    half = 0
    q_aug = q_scr[half]

    m_scr[...] = jnp.full(m_scr.shape, NEG, F32)
    acc_scr[...] = jnp.zeros(acc_scr.shape, F32)
    n_kv = (q0 + TQ + TK_SEL - 1) // TK_SEL
    rep = TK_SEL // LANES

    def scores(j, s_ref):
        k0 = pl.multiple_of(j * TK_SEL, TK_SEL)
        s_ref[...] = lax.dot_general(q_aug, ksa_ref[pl.ds(k0, TK_SEL), :], (((1,), (1,)), ((), ())),
                                     preferred_element_type=F32)

    def kv_tile(j, s_ref):
        k0 = pl.multiple_of(j * TK_SEL, TK_SEL)
        vt = vsa_ref[pl.ds(k0, TK_SEL), :]
        b_start = pl.multiple_of(jnp.maximum(bank_k - (q0 - k0), 0), LANES)
        for h in range(hg):
            rows = slice(h * TQ, (h + 1) * TQ)
            sh = s_ref[rows] + sb_ref[h, :, pl.ds(b_start, TK_SEL)]
            m_prev = m_scr[rows]
            m_next = jnp.maximum(m_prev, jnp.max(sh, axis=1, keepdims=True))
            alpha = jnp.exp2(m_prev - m_next)
            e = jnp.exp2(sh - jnp.tile(m_next, (1, rep)))
            acc_scr[rows] = (jnp.tile(alpha, (1, AUG_W // LANES)) * acc_scr[rows]
                             + jnp.dot(e.astype(BF16), vt, preferred_element_type=F32))
            m_scr[rows] = m_next

    def run(first, count):
        bufs = (s0_scr, s1_scr)
        for k in range(count):
            scores(first + k + 1, bufs[(k + 1) % 2])
            kv_tile(first + k, bufs[k % 2])

    def kv_quad(i, carry):
        run(4 * i, 4)
        return carry

    scores(0, s0_scr)
    n_before = n_kv - 1
    lax.fori_loop(0, n_before // 4, kv_quad, 0)
    done4 = n_before // 4 * 4

    @pl.when(n_before - done4 >= 2)
    def _():
        run(done4, 2)

    done2 = n_before // 2 * 2

    @pl.when(n_before > done2)
    def _():
        scores(done2 + 1, s1_scr)
        kv_tile(done2, s0_scr)
        kv_tile(done2 + 1, s1_scr)

    @pl.when(n_before == done2)
    def _():
        kv_tile(done2, s0_scr)

    sig_gl = _sigmoid(gl_ref[...])
    z_refs = (z0_ref, z1_ref, z2_ref)
    for h in range(hg):
        rows = slice(h * TQ, (h + 1) * TQ)
        cols = slice(h * HEAD_DIM, (h + 1) * HEAD_DIM)
        o_sel = acc_scr[rows, 0:HEAD_DIM] / acc_scr[rows, HEAD_DIM:AUG_W]
        out = jnp.zeros((TQ, HEAD_DIM), F32)
        for br, o_br in enumerate((ocmp_scr[half, rows, :], o_sel, owin_scr[half, rows, :])):
            z = z_refs[br][:, cols]
            c = br * hg + h
            out = out + sig_gl[:, c:c + 1] * o_br * (z * _sigmoid(z))
        o_ref[:, cols] = out.astype(BF16)


def _attn_prompt(proj, gl_g, g_q, kcmp, vcmp, ks_aug, vs_aug, kw_b, vw_aug, rr, sb, wb, bank_k):
    s_len = proj.shape[0]
    n_qt = s_len // TQ
    n_cmp_rows = kcmp.shape[2]
    gcols = N_HEADS * HEAD_DIM // GROUP_W
    qz = lambda seg: pl.BlockSpec((TQ, GROUP_W), lambda g, i, seg=seg: (i, seg * gcols + g))
    per_group = lambda shape: _resident((1,) + shape, lambda g, i: (g,) + (0,) * len(shape))
    kv_spec = lambda width: _resident((s_len, width), lambda g, i: (0, g))
    return pl.pallas_call(
        functools.partial(_attn_prompt_kernel, n_qtiles=n_qt, bank_k=bank_k),
        grid=(N_KV, n_qt),
        in_specs=[qz(0), qz(1), qz(2), qz(3),
                  pl.BlockSpec((TQ, LANES), lambda g, i: (i, g)),
                  _resident((1, HEAD_DIM), lambda g, i: (0, 0)),
                  per_group((n_cmp_rows, HEAD_DIM)), per_group((n_cmp_rows, HEAD_DIM)),
                  kv_spec(AUG_W), kv_spec(AUG_W), kv_spec(HEAD_DIM), kv_spec(AUG_W),
                  _resident((HEADS_PER_GROUP,) + rr.shape[1:], lambda g, i: (g, 0, 0)),
                  _resident((HEADS_PER_GROUP,) + sb.shape[1:], lambda g, i: (g, 0, 0)),
                  _resident((HEADS_PER_GROUP,) + wb.shape[1:], lambda g, i: (g, 0, 0))],
        out_specs=pl.BlockSpec((TQ, GROUP_W), lambda g, i: (i, g)),
        out_shape=jax.ShapeDtypeStruct((s_len, N_HEADS * HEAD_DIM), BF16),
        scratch_shapes=[pltpu.VMEM((1, HEADS_PER_GROUP * TQ, AUG_W), BF16),
                        pltpu.VMEM((1, n_cmp_rows + SUBLANES, TQ), F32),
                        pltpu.VMEM((1, HEADS_PER_GROUP * TQ, HEAD_DIM), F32),
                        pltpu.VMEM((1, HEADS_PER_GROUP * TQ, HEAD_DIM), F32),
                        pltpu.VMEM((HEADS_PER_GROUP * TQ, LANES), F32),
                        pltpu.VMEM((HEADS_PER_GROUP * TQ, AUG_W), F32),
                        pltpu.VMEM((HEADS_PER_GROUP * TQ, TK_SEL), F32),
                        pltpu.VMEM((HEADS_PER_GROUP * TQ, TK_SEL), F32)],
        compiler_params=_cparams(("arbitrary", "arbitrary")),
        name="attn_prompt",
    )(proj, proj, proj, proj, gl_g, g_q.reshape(1, HEAD_DIM), kcmp[0], vcmp[0],
      ks_aug, vs_aug, kw_b, vw_aug, rr, sb, wb)


def _attn_sample_kernel(*refs, n_tok, pages_per_chunk, page, past_len):
    pt_ref = refs[0]
    q_ref, z_ref, gl_ref, gq_ref, kcmp_ref, vcmp_ref = refs[1:7]
    kpages = refs[7:7 + pages_per_chunk]
    vpages = refs[7 + pages_per_chunk:7 + 2 * pages_per_chunk]
    (ksn_ref, vsn_ref, kwn_ref, vwn_ref, kwc_ref, vwc_ref, bcmp_ref, bsel_ref, bwin_ref, ov_ref, ex_ref,
     o_ref, sel_scr, m_scr, l_scr, acc_scr, q_scr, ocmp_scr) = refs[7 + 2 * pages_per_chunk:]
    del pt_ref
    c = pl.program_id(1)
    n_chunks = pl.num_programs(1)
    hg = HEADS_PER_GROUP
    nr = hg * n_tok
    scale = HEAD_DIM ** -0.5
    n_cmp_rows = kcmp_ref.shape[2]
    n_sb = ov_ref.shape[0]
    chunk = pages_per_chunk * page
    tok = lax.broadcasted_iota(jnp.int32, (N_KV * nr, 1), 0) % n_tok

    @pl.when(c == 0)
    def _():
        q_scr[...] = _rms_rows(q_ref[0], gq_ref[...]).astype(BF16)
        m_scr[...] = jnp.full(m_scr.shape, NEG, F32)
        l_scr[...] = jnp.zeros(l_scr.shape, F32)
        acc_scr[...] = jnp.zeros(acc_scr.shape, F32)
        ci = lax.broadcasted_iota(jnp.int32, (1, n_cmp_rows), 1)
        cmp_valid = ci < n_cmp_rows - 1
        s = jnp.concatenate(
            [lax.dot_general(q_scr[g * nr:(g + 1) * nr], kcmp_ref[0, g], (((1,), (1,)), ((), ())),
                             preferred_element_type=F32) for g in range(N_KV)], axis=0)
        s = jnp.where(cmp_valid, s * scale + bcmp_ref[...], NEG)
        e = jnp.where(cmp_valid, jnp.exp(s - jnp.max(s, axis=1, keepdims=True)), 0.0)
        p = e / jnp.sum(e, axis=1, keepdims=True)
        pb = p.astype(BF16)
        psums = []
        for g in range(N_KV):
            rows = slice(g * nr, (g + 1) * nr)
            ocmp_scr[rows] = jnp.dot(pb[rows], vcmp_ref[0, g], preferred_element_type=F32)
            psum = p[g * nr:g * nr + n_tok]
            for h in range(1, hg):
                psum = psum + p[g * nr + h * n_tok:g * nr + (h + 1) * n_tok]
            psums.append(psum)
        rest = jnp.concatenate(psums, axis=0)
        imp_t = jnp.zeros((n_sb, N_KV * n_tok), F32)
        for _ in range(3):
            piece = rest.astype(BF16)
            rest = rest - piece.astype(F32)
            imp_t = imp_t + lax.dot_general(ov_ref[...], piece, (((1,), (1,)), ((), ())),
                                            preferred_element_type=F32)
        blk = lax.broadcasted_iota(jnp.int32, (n_sb, 1), 0)
        forced = (blk == 0) | (blk == n_sb - 1)
        score = jnp.where(forced, FORCE, imp_t)
        sel_t = _topk_rows(score, blk, min(N_SELECT, n_sb + 1) - 1)
        sel_f = jnp.where(sel_t, 1.0, 0.0)
        if N_KV * n_tok < LANES:
            sel_f = jnp.concatenate([sel_f, jnp.zeros((n_sb, LANES - N_KV * n_tok), F32)], axis=1)
        sel_f = sel_f.T
        for g in range(N_KV):
            sel_scr[g * nr:(g + 1) * nr] = jnp.tile(sel_f[g * n_tok:(g + 1) * n_tok], (hg, 1)).astype(BF16)

    def group_rows(ref, g):
        return ref[0, pl.ds(g, ref.shape[1] // N_KV, stride=N_KV), :].astype(BF16)

    def all_scores(k_of_group):
        return jnp.concatenate(
            [lax.dot_general(q_scr[g * nr:(g + 1) * nr], k_of_group(g), (((1,), (1,)), ((), ())),
                             preferred_element_type=F32) for g in range(N_KV)], axis=0)

    def all_values(e, v_of_group):
        eb = e.astype(BF16)
        return jnp.concatenate(
            [jnp.dot(eb[g * nr:(g + 1) * nr], v_of_group(g), preferred_element_type=F32) for g in range(N_KV)],
            axis=0)

    def online_update(sh, v_of_group):
        m_prev = m_scr[...]
        m_next = jnp.maximum(m_prev, jnp.max(sh, axis=1, keepdims=True))
        alpha = jnp.exp(m_prev - m_next)
        e = jnp.exp(sh - m_next[:, 0:1])
        l_scr[...] = alpha * l_scr[...] + jnp.sum(e, axis=1, keepdims=True)
        acc_scr[...] = alpha * acc_scr[...] + all_values(e, v_of_group)
        m_scr[...] = m_next

    k0 = pl.multiple_of(c * chunk, chunk)
    s = all_scores(lambda g: jnp.concatenate([group_rows(r, g) for r in kpages], axis=0))
    picked = jnp.dot(sel_scr[...], ex_ref[:, pl.ds(k0, chunk)], preferred_element_type=F32)
    sh = jnp.where(picked > 0.5, s * scale + bsel_ref[:, pl.ds(k0, chunk)], NEG)
    online_update(sh, lambda g: jnp.concatenate([group_rows(r, g) for r in vpages], axis=0))

    @pl.when(c == n_chunks - 1)
    def _():
        new_w = ksn_ref.shape[1] // N_KV
        jn = lax.broadcasted_iota(jnp.int32, (1, new_w), 1)
        new_valid = jn <= tok
        wbuf = kwc_ref.shape[1] // N_KV
        jw = lax.broadcasted_iota(jnp.int32, (1, wbuf), 1)
        dist_c = (past_len + tok) - (past_len - wbuf + jw)
        win_valid_c = dist_c < WINDOW
        s = all_scores(lambda g: group_rows(ksn_ref, g))
        sh = jnp.where(new_valid, s * scale + bsel_ref[:, pl.ds(past_len, new_w)], NEG)
        online_update(sh, lambda g: group_rows(vsn_ref, g))
        o_sel = acc_scr[...] / l_scr[...]
        s_c = all_scores(lambda g: group_rows(kwc_ref, g))
        s_n = all_scores(lambda g: group_rows(kwn_ref, g))
        s_c = jnp.where(win_valid_c, s_c * scale + bwin_ref[:, 0:wbuf], NEG)
        s_n = jnp.where(new_valid, s_n * scale + bwin_ref[:, wbuf:wbuf + new_w], NEG)
        mx = jnp.maximum(jnp.max(s_c, axis=1, keepdims=True), jnp.max(s_n, axis=1, keepdims=True))
        e_c = jnp.exp(s_c - mx)
        e_n = jnp.exp(s_n - mx)
        den = jnp.sum(e_c, axis=1, keepdims=True) + jnp.sum(e_n, axis=1, keepdims=True)
        o_win = (all_values(e_c, lambda g: group_rows(vwc_ref, g))
                 + all_values(e_n, lambda g: group_rows(vwn_ref, g))) / den
        sig_gl = _sigmoid(gl_ref[0])
        out = jnp.zeros((N_KV * nr, HEAD_DIM), F32)
        for br, o_br in enumerate((ocmp_scr[...], o_sel, o_win)):
            z = z_ref[0, br]
            out = out + sig_gl[br] * o_br * (z * _sigmoid(z))
        o_ref[0] = out.astype(BF16)


def _attn_sample(q_s, z_s, gl_s, g_q, kcmp, vcmp, ksel_pages, vsel_pages, page_table, ks_new, vs_new,
                 kw_new, vw_new, kw_cache, vw_cache, bcmp, bsel, bwin, overlap, expand, n_tok):
    nb = q_s.shape[0]
    n_pages = page_table.shape[1]
    page = ksel_pages.shape[1] // N_KV
    ppc = min(8, n_pages)
    n_chunks = n_pages // ppc
    nrow = q_s.shape[1]
    per_b = lambda shape: pl.BlockSpec((1,) + shape, lambda b, c, pt: (b,) + (0,) * len(shape))
    full = lambda shape: _resident(shape, lambda b, c, pt: (0,) * len(shape))
    page_spec = lambda i: pl.BlockSpec((1, page * N_KV, HEAD_DIM),
                                       lambda b, c, pt, i=i: (pt[b, c * ppc + i], 0, 0))
    in_specs = ([per_b(q_s.shape[1:]), per_b(z_s.shape[1:]), per_b(gl_s.shape[1:]), full((1, HEAD_DIM)),
                 per_b(kcmp.shape[1:]), per_b(vcmp.shape[1:])]
                + [page_spec(i) for i in range(ppc)] + [page_spec(i) for i in range(ppc)]
                + [per_b(ks_new.shape[1:])] * 4 + [per_b(kw_cache.shape[1:])] * 2
                + [full(bcmp.shape), full(bsel.shape), full(bwin.shape), full(overlap.shape), full(expand.shape)])
    grid_spec = pltpu.PrefetchScalarGridSpec(
        num_scalar_prefetch=1,
        grid=(nb, n_chunks),
        in_specs=in_specs,
        out_specs=pl.BlockSpec((1, nrow, HEAD_DIM), lambda b, c, pt: (b, 0, 0)),
        scratch_shapes=[pltpu.VMEM((nrow, expand.shape[0]), BF16),
                        pltpu.VMEM((nrow, LANES), F32), pltpu.VMEM((nrow, LANES), F32),
                        pltpu.VMEM((nrow, HEAD_DIM), F32), pltpu.VMEM((nrow, HEAD_DIM), BF16),
                        pltpu.VMEM((nrow, HEAD_DIM), F32)],
    )
    return pl.pallas_call(
        functools.partial(_attn_sample_kernel, n_tok=n_tok, pages_per_chunk=ppc, page=page,
                          past_len=n_pages * page),
        grid_spec=grid_spec,
        out_shape=jax.ShapeDtypeStruct((nb, nrow, HEAD_DIM), BF16),
        compiler_params=_cparams(("arbitrary", "arbitrary")),
        name="attn_sample",
    )(page_table, q_s, z_s, gl_s, g_q.reshape(1, HEAD_DIM), kcmp, vcmp,
      *([ksel_pages] * ppc), *([vsel_pages] * ppc), ks_new, vs_new, kw_new, vw_new, kw_cache, vw_cache,
      bcmp, bsel, bwin, overlap, expand)


def kernel(x_prompt, x_sample, p_prompt, p_sample, state_conv, state_h, cache_k_cmp, cache_v_cmp, cache_k_sel, cache_v_sel, cache_k_win, cache_v_win, page_table, norm_g, a_w_in, a_conv_w, a_conv_b, a_w_r, a_b_r, a_w_i, a_b_i, a_lam, a_w_out, kv_norm_g, w_kv, g_k_cmp, g_k_sel, g_k_win, cmp_pos, cmp_w1a, cmp_w1b, cmp_b1, cmp_w2, cmp_b2, b_w_in, b_g_q, b_w_out, rel_bias, ple_w_proj, ple_norm_g, ple_w_gate):
    bp, s_len, d = x_prompt.shape
    db, ds, _ = x_sample.shape
    n_pool, page = cache_k_sel.shape[:2]
    n_pages = page_table.shape[1]
    past_len = n_pages * page
    wbuf = cache_k_win.shape[1]
    hd = N_HEADS * HEAD_DIM
    assert bp == 1 and norm_g.shape[0] == 2 and a_w_in.shape[0] == 1 and b_w_in.shape[0] == 1
    assert d == hd and s_len % TK_SEL == 0 and s_len >= WINDOW + TQ and page % SEL_BLOCK == 0
    assert wbuf == WINDOW and ds <= SEL_BLOCK and past_len % TK_SEL == 0 and s_len // SEL_BLOCK <= HEAD_DIM

    w_in0 = a_w_in[0].astype(BF16)
    w_ri = jnp.concatenate([a_w_r[0], a_w_i[0]], axis=-1).astype(BF16)
    w_out0 = a_w_out[0].astype(BF16)
    w_gate = ple_w_gate.astype(BF16)
    w_proj = ple_w_proj.astype(BF16)
    w_kv_b = w_kv.astype(BF16)
    w_in1 = b_w_in[0][:, :4 * hd].astype(BF16)
    w_gl = b_w_in[0][:, 4 * hd:].reshape(d, 3, N_KV, HEADS_PER_GROUP).transpose(0, 2, 1, 3)
    w_gl = jnp.pad(w_gl.reshape(d, N_KV, 3 * HEADS_PER_GROUP), ((0, 0), (0, 0), (0, LANES - 3 * HEADS_PER_GROUP)))
    w_gl = w_gl.reshape(d, N_KV * LANES).astype(BF16)
    w_out1 = b_w_out[0].astype(BF16)
    w1 = jnp.concatenate([cmp_w1a, cmp_w1b], axis=-1).astype(BF16)
    w2 = cmp_w2.astype(BF16)
    pe = cmp_pos.reshape(2, 2, 1, CMP_STRIDE * HEAD_DIM)
    pe_hi = pe.astype(BF16)
    pe_lo = (pe - pe_hi.astype(F32)).astype(BF16)
    pe_rows = jnp.concatenate([pe_hi, pe_lo, jnp.zeros((2, 2, SUBLANES - 2, pe.shape[-1]), BF16)], axis=2)
    pe_rows = pe_rows.reshape(2, 2 * SUBLANES, pe.shape[-1])
    b1 = cmp_b1.reshape(2, 1, -1)
    b2 = cmp_b2.reshape(2, 1, HEAD_DIM)

    xp = x_prompt.reshape(s_len, d)
    xs = x_sample.reshape(db * ds, d)
    uz_p = _norm_matmul(xp, norm_g[0], w_in0, 1024)
    uz_s = _norm_matmul(xs, norm_g[0], w_in0, 1024)
    rg = (a_conv_w[0], a_conv_b[0], w_ri, a_b_r[0], a_b_i[0], a_lam[0])
    gated_p, h_p = _rglru_prompt(uz_p, jnp.zeros((SUBLANES, d), F32), jnp.zeros((1, d), F32), *rg)
    conv_p = uz_p[s_len - (CONV_W - 1):, :d].reshape(1, bp, CONV_W - 1, d)
    u_s = uz_s[:, :d].reshape(db, ds, d).transpose(1, 0, 2)
    z_s = uz_s[:, d:].reshape(db, ds, d).transpose(1, 0, 2)
    up_s = jnp.concatenate([state_conv[0].transpose(1, 0, 2), u_s], axis=0)
    gated_s, h_s = _rglru_sample(up_s, z_s, state_h[0], *rg)
    conv_s = up_s[ds:].transpose(1, 0, 2)[None]
    gated_s = gated_s.transpose(1, 0, 2).reshape(db * ds, d)
    ple0 = (w_proj[0], ple_norm_g[0], w_gate[0])
    h1_p = _outproj_ple(gated_p, xp, p_prompt[0].reshape(s_len, -1), w_out0, *ple0)
    h1_s = _outproj_ple(gated_s, xs, p_sample[0].reshape(db * ds, -1), w_out0, *ple0)

    kv_p = _kv_proj(h1_p, kv_norm_g, w_kv_b, g_k_sel, g_k_win)
    kv_s = _kv_proj(h1_s, kv_norm_g, w_kv_b, g_k_sel, g_k_win)
    kc_p, vc_p, ks_p, vs_p, kw_p, vw_p = kv_p[:6]
    kc_s, vc_s, ks_s, vs_s, kw_s, vw_s = kv_s[:6]

    p_pages = s_len // page
    ident = jnp.minimum(jnp.arange(p_pages + 1, dtype=jnp.int32), p_pages - 1)[None]
    cmp_w = (pe_rows, w1, b1, w2, b2, g_k_cmp)
    pieces = lambda x: x.reshape(-1, page // CMP_STRIDE, PIECE_ROWS, HEAD_DIM)
    kcmp_p, vcmp_p = _compress(pieces(kc_p), pieces(vc_p), ident, *cmp_w)
    table_ext = jnp.concatenate([page_table, page_table[:, -1:]], axis=1)
    kcmp_s, vcmp_s = _compress(pieces(cache_k_cmp), pieces(cache_v_cmp), table_ext, *cmp_w)

    proj_p, gl_p = _norm_matmul(h1_p, norm_g[1], w_in1, 1024, w_gl)
    proj_s, gl_s = _norm_matmul(h1_s, norm_g[1], w_in1, 1024, w_gl)

    n_qt = s_len // TQ
    n_cmp_rows = s_len // CMP_STRIDE
    bank_w = max(TK_SEL, WINDOW + TQ)
    bank_k = min(-(-(_first_constant_distance() + bank_w) // LANES) * LANES, s_len)
    sb, wb, rr = _bias_banks(rel_bias, n_qt, n_cmp_rows // SUBLANES, bank_k, bank_w)
    o_p = _attn_prompt(proj_p, gl_p, b_g_q[0], kcmp_p, vcmp_p, kv_p[6], kv_p[7], kv_p[8], kv_p[9],
                       rr, sb, wb, bank_k)

    def head_rows(x):
        return x.reshape(db, ds, N_HEADS, HEAD_DIM).transpose(0, 2, 1, 3).reshape(db, N_HEADS * ds, HEAD_DIM)

    q_s = head_rows(proj_s[:, :hd])
    zz_s = jnp.stack([head_rows(proj_s[:, (1 + br) * hd:(2 + br) * hd]) for br in range(3)], axis=1)
    gl_s = gl_s.reshape(db, ds, N_KV, LANES)[..., :3 * HEADS_PER_GROUP].reshape(db, ds, N_KV, 3, HEADS_PER_GROUP)
    gl_s = gl_s.transpose(0, 3, 2, 4, 1).reshape(db, 3, N_HEADS * ds, 1)
    new_w = LANES
    group_rows = lambda x: x.reshape(db, -1, HEAD_DIM)
    pad_new = lambda x: jnp.pad(group_rows(x), ((0, 0), (0, (new_w - ds) * N_KV), (0, 0)))
    n_cmp_s = past_len // CMP_STRIDE
    bsel = _query_rows_bias(rel_bias, past_len, ds, past_len + new_w).reshape(N_HEADS * ds, -1)
    bwin = bsel[:, past_len - wbuf:]
    bcmp = bsel[:, CMP_BLOCK - 1::CMP_STRIDE][:, :n_cmp_s]
    n_sb_s = past_len // SEL_BLOCK
    ci = jnp.arange(n_cmp_s)[:, None] * CMP_STRIDE
    sj = jnp.arange(n_sb_s)[None, :] * SEL_BLOCK
    overlap = ((ci < sj + SEL_BLOCK) & (ci + CMP_BLOCK > sj) & (ci < past_len - CMP_STRIDE)).T.astype(BF16)
    expand_s = (jnp.arange(past_len)[None, :] // SEL_BLOCK == jnp.arange(n_sb_s)[:, None]).astype(BF16)
    o_s = _attn_sample(q_s, zz_s, gl_s, b_g_q[0], kcmp_s, vcmp_s,
                       cache_k_sel.reshape(n_pool, -1, HEAD_DIM), cache_v_sel.reshape(n_pool, -1, HEAD_DIM),
                       page_table, pad_new(ks_s), pad_new(vs_s), pad_new(kw_s), pad_new(vw_s),
                       group_rows(cache_k_win), group_rows(cache_v_win),
                       bcmp, bsel, bwin, overlap, expand_s, ds)
    o_s = o_s.reshape(db, N_HEADS, ds, HEAD_DIM).transpose(0, 2, 1, 3).reshape(db * ds, hd)

    ple1 = (w_proj[1], ple_norm_g[1], w_gate[1])
    y_p = _outproj_ple(o_p, h1_p, p_prompt[1].reshape(s_len, -1), w_out1, *ple1)
    y_s = _outproj_ple(o_s, h1_s, p_sample[1].reshape(db * ds, -1), w_out1, *ple1)

    rows4 = lambda x, b, t: x.reshape(b, t, N_KV, HEAD_DIM)
    wk_p = min(WINDOW, s_len)
    kwin_s = jnp.concatenate([cache_k_win, rows4(kw_s, db, ds)], axis=1)[:, -WINDOW:]
    vwin_s = jnp.concatenate([cache_v_win, rows4(vw_s, db, ds)], axis=1)[:, -WINDOW:]
    return (y_p.reshape(bp, s_len, d), y_s.reshape(db, ds, d),
            conv_p, h_p.reshape(1, bp, d),
            rows4(kc_p, bp, s_len), rows4(vc_p, bp, s_len), rows4(ks_p, bp, s_len), rows4(vs_p, bp, s_len),
            rows4(kw_p, bp, s_len)[:, -wk_p:], rows4(vw_p, bp, s_len)[:, -wk_p:],
            conv_s, h_s.reshape(1, db, d),
            rows4(kc_s, db, ds), rows4(vc_s, db, ds), rows4(ks_s, db, ds), rows4(vs_s, db, ds),
            kwin_s, vwin_s)
```

```python
import functools
import math

import jax
import jax.numpy as jnp
from jax import lax
from jax.experimental import pallas as pl
from jax.experimental.pallas import tpu as pltpu

F32 = jnp.float32
BF16 = jnp.bfloat16

N_RNN_BLOCKS = 8
CONV_W = 4
LRU_C = 8.0
N_HEADS = 16
HEAD_DIM = 128
N_KV = 4
HEADS_PER_GROUP = N_HEADS // N_KV
GROUP_W = HEADS_PER_GROUP * HEAD_DIM
KV_W = N_KV * HEAD_DIM
CMP_BLOCK = 32
CMP_STRIDE = 16
SEL_BLOCK = 64
N_SELECT = 16
WINDOW = 512
N_BUCKETS = 32
MAX_DISTANCE = 4096
EPS = 1e-6
NEG = -1e30
FORCE = 1e9
REMOVED = -2.0 ** 126
LOG2E = math.log2(math.e)
MAX_SCORE_BOUND = 60.0

LANES = 128
SUBLANES = 8
VMEM_LIMIT = 56 * 1024 * 1024

TQ = 128
TK_SEL = 512
KV_RUN = 8
CMP_PAGES_PER_STEP = 16


def _cparams(sem):
    return pltpu.CompilerParams(dimension_semantics=sem, vmem_limit_bytes=VMEM_LIMIT)


def _resident(shape, index_map):
    return pl.BlockSpec(shape, index_map, pipeline_mode=pl.Buffered(1))


def _rms_rows(x, g):
    return x * lax.rsqrt(jnp.mean(x * x, axis=-1, keepdims=True) + EPS) * g


def _sigmoid(x):
    return 1.0 / (1.0 + jnp.exp(-x))


def _norm_matmul_kernel(x_ref, g_ref, w_ref, *rest):
    if len(rest) == 4:
        w2_ref, o_ref, o2_ref, xn_ref = rest
    else:
        (o_ref, xn_ref), w2_ref, o2_ref = rest, None, None

    @pl.when(pl.program_id(1) == 0)
    def _():
        xn_ref[...] = _rms_rows(x_ref[...], g_ref[...]).astype(BF16)
        if w2_ref is not None:
            o2_ref[...] = jnp.dot(xn_ref[...], w2_ref[...], preferred_element_type=F32)

    o_ref[...] = jnp.dot(xn_ref[...], w_ref[...], preferred_element_type=F32)


def _norm_matmul(x, g, w, tn, w2=None, n=None):
    m, d = x.shape
    n = w.shape[1] if n is None else n
    tm = min(m, 1024)
    in_specs = [pl.BlockSpec((tm, d), lambda i, j: (i, 0)),
                pl.BlockSpec((1, d), lambda i, j: (0, 0)),
                pl.BlockSpec((d, tn), lambda i, j: (0, j))]
    out_specs = [pl.BlockSpec((tm, tn), lambda i, j: (i, j))]
    out_shape = [jax.ShapeDtypeStruct((m, n), F32)]
    args = [x, g.reshape(1, d), w]
    if w2 is not None:
        in_specs.append(_resident(w2.shape, lambda i, j: (0, 0)))
        out_specs.append(pl.BlockSpec((tm, w2.shape[1]), lambda i, j: (i, 0)))
        out_shape.append(jax.ShapeDtypeStruct((m, w2.shape[1]), F32))
        args.append(w2)
    outs = pl.pallas_call(
        _norm_matmul_kernel,
        grid=(m // tm, n // tn),
        in_specs=in_specs,
        out_specs=out_specs,
        out_shape=out_shape,
        scratch_shapes=[pltpu.VMEM((tm, d), BF16)],
        compiler_params=_cparams(("parallel", "arbitrary")),
        name="norm_matmul",
    )(*args)
    return outs if w2 is not None else outs[0]


def _lru_gates(xc, wri_ref, br_ref, bi_ref, sp_ref, n, bw):
    sl = slice(n * bw, (n + 1) * bw)
    ri = jnp.dot(xc.astype(BF16), wri_ref[n], preferred_element_type=F32)
    r = _sigmoid(ri[:, :bw] + br_ref[:, sl])
    i = _sigmoid(ri[:, bw:] + bi_ref[:, sl])
    log_a = (-LRU_C) * r * sp_ref[:, sl]
    a = jnp.exp(log_a)
    b = jnp.sqrt(1.0 - a * a) * (i * xc)
    return a, b


def _softplus(x):
    return jnp.maximum(x, 0.0) + jnp.log(1.0 + jnp.exp(-jnp.abs(x)))


def _rglru_prompt_kernel(u_ref, z_ref, cinit_ref, hinit_ref, cw_ref, cb_ref, wri_ref, br_ref, bi_ref,
                         lam_ref, g_ref, hlast_ref, ubuf, a_scr, b_scr, hs_scr, hcar, sp_scr):
    tm, d = u_ref.shape
    bw = d // N_RNN_BLOCKS

    @pl.when(pl.program_id(0) == 0)
    def _():
        ubuf[0:SUBLANES, :] = cinit_ref[...]
        hcar[...] = hinit_ref[...]
        sp_scr[...] = _softplus(-lam_ref[...])

    ubuf[SUBLANES:, :] = u_ref[...]
    for n in range(N_RNN_BLOCKS):
        sl = slice(n * bw, (n + 1) * bw)
        up = ubuf[:, sl]
        xc = cb_ref[:, sl] + up * cw_ref[CONV_W - 1:CONV_W, sl]
        for k in range(CONV_W - 1):
            xc = xc + pltpu.roll(up, CONV_W - 1 - k, axis=0) * cw_ref[k:k + 1, sl]
        xc = xc[SUBLANES:]
        a, b = _lru_gates(xc, wri_ref, br_ref, bi_ref, sp_scr, n, bw)
        a_scr[:, sl] = a
        b_scr[:, sl] = b
    ubuf[0:SUBLANES, :] = u_ref[tm - SUBLANES:tm, :]

    def step(t, h):
        h = a_scr[pl.ds(t, 1), :] * h + b_scr[pl.ds(t, 1), :]
        hs_scr[pl.ds(t, 1), :] = h
        return h

    h = lax.fori_loop(0, tm, step, hcar[...], unroll=8)
    hcar[...] = h
    hlast_ref[...] = h
    z = z_ref[...]
    g_ref[...] = (hs_scr[...] * (z * _sigmoid(z))).astype(BF16)


def _rglru_prompt(uz, conv_init, h_init, cw, cb, wri, br, bi, lam):
    t, d2 = uz.shape
    d = d2 // 2
    tm = min(t, 256)
    full = lambda shape: pl.BlockSpec(shape, lambda i: (0,) * len(shape))
    return pl.pallas_call(
        _rglru_prompt_kernel,
        grid=(t // tm,),
        in_specs=[pl.BlockSpec((tm, d), lambda i: (i, 0)),
                  pl.BlockSpec((tm, d), lambda i: (i, 1)),
                  full((SUBLANES, d)), full((1, d)), full((CONV_W, d)), full((1, d)),
                  full(wri.shape), full((1, d)), full((1, d)), full((1, d))],
        out_specs=[pl.BlockSpec((tm, d), lambda i: (i, 0)), full((1, d))],
        out_shape=[jax.ShapeDtypeStruct((t, d), BF16), jax.ShapeDtypeStruct((1, d), F32)],
        scratch_shapes=[pltpu.VMEM((tm + SUBLANES, d), F32), pltpu.VMEM((tm, d), F32),
                        pltpu.VMEM((tm, d), F32), pltpu.VMEM((tm, d), F32),
                        pltpu.VMEM((1, d), F32), pltpu.VMEM((1, d), F32)],
        compiler_params=_cparams(("arbitrary",)),
        name="rglru_prompt",
    )(uz, uz, conv_init, h_init, cw, cb.reshape(1, d), wri, br.reshape(1, d), bi.reshape(1, d),
      lam.reshape(1, d))


def _rglru_sample_kernel(up_ref, z_ref, h0_ref, cw_ref, cb_ref, wri_ref, br_ref, bi_ref, lam_ref,
                         g_ref, hlast_ref, sp_scr):
    n_t = z_ref.shape[0]
    d = z_ref.shape[2]
    bw = d // N_RNN_BLOCKS
    sp_scr[...] = _softplus(-lam_ref[...])
    for n in range(N_RNN_BLOCKS):
        sl = slice(n * bw, (n + 1) * bw)
        h = h0_ref[:, sl]
        for t in range(n_t):
            xc = cb_ref[:, sl]
            for k in range(CONV_W):
                xc = xc + up_ref[t + k, :, sl] * cw_ref[k:k + 1, sl]
            a, b = _lru_gates(xc, wri_ref, br_ref, bi_ref, sp_scr, n, bw)
            h = a * h + b
            z = z_ref[t, :, sl]
            g_ref[t, :, sl] = (h * (z * _sigmoid(z))).astype(BF16)
        hlast_ref[:, sl] = h


def _rglru_sample(up, z, h0, cw, cb, wri, br, bi, lam):
    n_t, nb, d = z.shape
    return pl.pallas_call(
        _rglru_sample_kernel,
        out_shape=[jax.ShapeDtypeStruct((n_t, nb, d), BF16), jax.ShapeDtypeStruct((nb, d), F32)],
        scratch_shapes=[pltpu.VMEM((1, d), F32)],
        compiler_params=pltpu.CompilerParams(vmem_limit_bytes=VMEM_LIMIT),
        name="rglru_sample",
    )(up, z, h0, cw, cb.reshape(1, d), wri, br.reshape(1, d), bi.reshape(1, d), lam.reshape(1, d))


def _outproj_ple_kernel(a_ref, res_ref, p_ref, wo_ref, wp_ref, gn_ref, wg_ref, o_ref, h_scr, hn_scr):
    tm, d = res_ref.shape
    ch = min(d, 512)
    a = a_ref[...]
    ssq = jnp.zeros((tm, 1), F32)
    for c in range(d // ch):
        sl = slice(c * ch, (c + 1) * ch)
        h = res_ref[:, sl] + jnp.dot(a, wo_ref[:, sl], preferred_element_type=F32)
        h_scr[:, sl] = h
        ssq = ssq + jnp.sum(h * h, axis=-1, keepdims=True)
    inv = lax.rsqrt(ssq * (1.0 / d) + EPS)
    for c in range(d // ch):
        sl = slice(c * ch, (c + 1) * ch)
        hn_scr[:, sl] = (h_scr[:, sl] * inv * gn_ref[:, sl]).astype(BF16)
    pb = p_ref[...].astype(BF16)
    hn = hn_scr[...]
    for c in range(d // ch):
        sl = slice(c * ch, (c + 1) * ch)
        gate = _sigmoid(jnp.dot(hn, wg_ref[:, sl], preferred_element_type=F32))
        pp = jnp.dot(pb, wp_ref[:, sl], preferred_element_type=F32)
        o_ref[:, sl] = h_scr[:, sl] + pp * gate


def _outproj_ple(a, res, p, w_out, w_proj, g_norm, w_gate):
    m, d = res.shape
    pd = p.shape[1]
    tm = min(m, 512)
    return pl.pallas_call(
        _outproj_ple_kernel,
        grid=(m // tm,),
        in_specs=[pl.BlockSpec((tm, a.shape[1]), lambda i: (i, 0)),
                  pl.BlockSpec((tm, d), lambda i: (i, 0)),
                  pl.BlockSpec((tm, pd), lambda i: (i, 0)),
                  _resident(w_out.shape, lambda i: (0, 0)),
                  _resident(w_proj.shape, lambda i: (0, 0)),
                  _resident((1, d), lambda i: (0, 0)),
                  _resident(w_gate.shape, lambda i: (0, 0))],
        out_specs=pl.BlockSpec((tm, d), lambda i: (i, 0)),
        out_shape=jax.ShapeDtypeStruct((m, d), F32),
        scratch_shapes=[pltpu.VMEM((tm, d), F32), pltpu.VMEM((tm, d), BF16)],
        compiler_params=_cparams(("parallel",)),
        name="outproj_ple",
    )(a, res, p, w_out, w_proj, g_norm.reshape(1, d), w_gate)


AUG_W = 2 * HEAD_DIM
MASK_OFF = -2.0 ** 60


def _kv_proj_kernel(x_ref, g_ref, w_ref, gsel_ref, gwin_ref,
                    kc_ref, vc_ref, ks_ref, vs_ref, kw_ref, vw_ref, ksa_ref, vsa_ref, kwb_ref, vwa_ref):
    tm = x_ref.shape[0]
    xn = _rms_rows(x_ref[...], g_ref[...]).astype(BF16)
    outs = (kc_ref, vc_ref, ks_ref, vs_ref, kw_ref, vw_ref)
    head_gain = {2: gsel_ref, 4: gwin_ref}
    pos = pl.program_id(0) * tm + lax.broadcasted_iota(jnp.int32, (tm, HEAD_DIM), 0)
    blk = lax.broadcasted_iota(jnp.int32, (tm, HEAD_DIM), 1)
    block_mask = jnp.where(blk == pos // SEL_BLOCK, MASK_OFF, 0.0).astype(BF16)
    ones = jnp.ones((tm, HEAD_DIM), BF16)
    for s in range(6):
        y = jnp.dot(xn, w_ref[:, s * KV_W:(s + 1) * KV_W], preferred_element_type=F32)
        for g in range(N_KV):
            yg = y[:, g * HEAD_DIM:(g + 1) * HEAD_DIM]
            if s in head_gain:
                yg = _rms_rows(yg, head_gain[s][...])
            outs[s][pl.ds(g, tm, stride=N_KV), :] = yg
            lo = slice(g * AUG_W, g * AUG_W + HEAD_DIM)
            hi = slice(g * AUG_W + HEAD_DIM, (g + 1) * AUG_W)
            if s == 2:
                ksa_ref[:, lo] = yg.astype(BF16)
                ksa_ref[:, hi] = block_mask
            elif s == 3:
                vsa_ref[:, lo] = yg.astype(BF16)
                vsa_ref[:, hi] = ones
            elif s == 4:
                kwb_ref[:, g * HEAD_DIM:(g + 1) * HEAD_DIM] = yg.astype(BF16)
            elif s == 5:
                vwa_ref[:, lo] = yg.astype(BF16)
                vwa_ref[:, hi] = ones


def _kv_proj(x, g, w, g_sel, g_win):
    m, d = x.shape
    tm = min(m, 512)
    row = lambda i: (i, 0)
    aug = N_KV * AUG_W
    return pl.pallas_call(
        _kv_proj_kernel,
        grid=(m // tm,),
        in_specs=[pl.BlockSpec((tm, d), row), _resident((1, d), lambda i: (0, 0)),
                  _resident(w.shape, lambda i: (0, 0)),
                  _resident((1, HEAD_DIM), lambda i: (0, 0)), _resident((1, HEAD_DIM), lambda i: (0, 0))],
        out_specs=([pl.BlockSpec((tm * N_KV, HEAD_DIM), row)] * 6
                   + [pl.BlockSpec((tm, aug), row), pl.BlockSpec((tm, aug), row),
                      pl.BlockSpec((tm, KV_W), row), pl.BlockSpec((tm, aug), row)]),
        out_shape=([jax.ShapeDtypeStruct((m * N_KV, HEAD_DIM), F32)] * 6
                   + [jax.ShapeDtypeStruct((m, aug), BF16), jax.ShapeDtypeStruct((m, aug), BF16),
                      jax.ShapeDtypeStruct((m, KV_W), BF16), jax.ShapeDtypeStruct((m, aug), BF16)]),
        compiler_params=_cparams(("parallel",)),
        name="kv_proj",
    )(x, g.reshape(1, d), w, g_sel.reshape(1, HEAD_DIM), g_win.reshape(1, HEAD_DIM))


PIECE_ROWS = CMP_STRIDE * N_KV
PIECE_PITCH = PIECE_ROWS + SUBLANES


def _piece_rows(buf, r, n):
    flat = buf.reshape(buf.shape[0] * PIECE_PITCH, HEAD_DIM)
    return flat[pl.ds(r, n, stride=PIECE_PITCH), :]


def _compress_kernel(pt_ref, kc_hbm, vc_hbm, pe_ref, w1_ref, b1_ref, w2_ref, b2_ref, gk_ref,
                     kcmp_ref, vcmp_ref, kbuf, vbuf, sem, lhs, bias_scr, *, n_quarters, pages_per_step):
    s = pl.program_id(0)
    n_steps = pl.num_programs(0)
    pieces_per_page = kc_hbm.shape[1]
    n_piece = pages_per_step * pieces_per_page
    n_load = n_piece + SUBLANES
    rows_g = n_piece + 2 * SUBLANES
    hidden = w2_ref.shape[1]

    def copies(step, slot):
        b = step // n_quarters
        q = step % n_quarters
        out = []
        for hbm, buf, which in ((kc_hbm, kbuf, 0), (vc_hbm, vbuf, 1)):
            for p in range(pages_per_step):
                pg = pt_ref[b, q * pages_per_step + p]
                dst = buf.at[slot, pl.ds(p * pieces_per_page, pieces_per_page), pl.ds(0, PIECE_ROWS), :]
                out.append(pltpu.make_async_copy(hbm.at[pg], dst, sem.at[slot, which]))
            pg = pt_ref[b, q * pages_per_step + pages_per_step]
            out.append(pltpu.make_async_copy(hbm.at[pg, 0], buf.at[slot, n_piece, pl.ds(0, PIECE_ROWS), :],
                                             sem.at[slot, which]))
        return out

    slot = s % 2

    @pl.when(s == 0)
    def _():
        for c in copies(0, 0):
            c.start()
        kbuf[:, n_piece + 1:] = jnp.zeros((2, SUBLANES - 1) + kbuf.shape[2:], F32)
        vbuf[:, n_piece + 1:] = jnp.zeros((2, SUBLANES - 1) + vbuf.shape[2:], F32)
        lhs[...] = jnp.zeros(lhs.shape, BF16)
        for which in range(2):
            pw = jnp.dot(pe_ref[which], w1_ref[which], preferred_element_type=F32)
            bias_scr[which] = (jnp.sum(pw[:SUBLANES, :hidden], axis=0, keepdims=True)
                               + jnp.sum(pw[SUBLANES:, hidden:], axis=0, keepdims=True) + b1_ref[which])

    @pl.when(s + 1 < n_steps)
    def _():
        for c in copies(s + 1, 1 - slot):
            c.start()

    for c in copies(s, slot):
        c.wait()

    for which, buf, out_ref in ((0, kbuf, kcmp_ref), (1, vbuf, vcmp_ref)):
        for g in range(N_KV):
            for pos in range(CMP_STRIDE):
                x = _piece_rows(buf.at[slot], pos * N_KV + g, n_load)
                lhs[which, g * rows_g:g * rows_g + n_load, pos * HEAD_DIM:(pos + 1) * HEAD_DIM] = x.astype(BF16)
        hab = jnp.dot(lhs[which], w1_ref[which], preferred_element_type=F32)
        for g in range(N_KV):
            blk = hab[g * rows_g:(g + 1) * rows_g]
            nxt = pltpu.roll(blk[:, hidden:], rows_g - 1, axis=0)
            hid = blk[:n_piece, :hidden] + nxt[:n_piece] + bias_scr[which]
            hid = hid * _sigmoid(hid)
            y = jnp.dot(hid.astype(BF16), w2_ref[which], preferred_element_type=F32) + b2_ref[which]
            if which == 0:
                y = _rms_rows(y, gk_ref[...])
            out_ref[0, g] = y.astype(BF16)


def _compress(kc_pages, vc_pages, table, pe_rows, w1, b1, w2, b2, g_k):
    nb, n_pages = table.shape[0], table.shape[1] - 1
    ppp = kc_pages.shape[1]
    pps = min(CMP_PAGES_PER_STEP, n_pages)
    n_quarters = n_pages // pps
    n_piece = pps * ppp
    kdim = CMP_STRIDE * HEAD_DIM
    hidden = w2.shape[1]
    full = lambda shape: pl.BlockSpec(shape, lambda s, pt: (0,) * len(shape))
    out_spec = pl.BlockSpec((1, N_KV, n_piece, HEAD_DIM), lambda s, pt: (s // n_quarters, 0, s % n_quarters, 0))
    out_sd = jax.ShapeDtypeStruct((nb, N_KV, n_pages * ppp, HEAD_DIM), BF16)
    buf = pltpu.VMEM((2, n_piece + SUBLANES, PIECE_PITCH, HEAD_DIM), F32)
    lhs = pltpu.VMEM((2, N_KV * (n_piece + 2 * SUBLANES), kdim), BF16)
    grid_spec = pltpu.PrefetchScalarGridSpec(
        num_scalar_prefetch=1,
        grid=(nb * n_quarters,),
        in_specs=[pl.BlockSpec(memory_space=pl.ANY), pl.BlockSpec(memory_space=pl.ANY),
                  full(pe_rows.shape), full(w1.shape), full(b1.shape), full(w2.shape), full(b2.shape),
                  full((1, HEAD_DIM))],
        out_specs=[out_spec, out_spec],
        scratch_shapes=[buf, buf, pltpu.SemaphoreType.DMA((2, 2)), lhs, pltpu.VMEM((2, 1, hidden), F32)],
    )
    return pl.pallas_call(
        functools.partial(_compress_kernel, n_quarters=n_quarters, pages_per_step=pps),
        grid_spec=grid_spec,
        out_shape=[out_sd, out_sd],
        compiler_params=_cparams(("arbitrary",)),
        name="compress",
    )(table, kc_pages, vc_pages, pe_rows, w1, b1, w2, b2, g_k.reshape(1, HEAD_DIM))


def _rel_bucket(dist):
    n = jnp.maximum(dist, 0)
    max_exact = N_BUCKETS // 2
    nf = jnp.maximum(n, 1).astype(F32)
    large = max_exact + (jnp.log(nf / max_exact) / math.log(MAX_DISTANCE / max_exact)
                         * (N_BUCKETS - max_exact)).astype(jnp.int32)
    large = jnp.minimum(large, N_BUCKETS - 1)
    return jnp.where(n < max_exact, n, large) & (N_BUCKETS - 1)


def _bias_table(rel_bias, dist, masked, gain=1.0):
    dist = lax.optimization_barrier(dist)
    onehot = (_rel_bucket(dist)[None, :] == jnp.arange(N_BUCKETS)[:, None]).astype(F32)
    table = jnp.dot(rel_bias.astype(F32).T * gain, onehot, precision=lax.Precision.HIGHEST)
    return jnp.where(dist[None, :] >= 0, table, NEG) if masked else table


def _bias_bank_kernel(gs_ref, gw_ref, g2_ref, sb_ref, wb_ref, rr_ref):
    for g_ref, o_ref in ((gs_ref, sb_ref), (gw_ref, wb_ref)):
        wp = g_ref.shape[2]
        x = jnp.broadcast_to(g_ref[0], (TQ, wp))
        o_ref[0] = pltpu.roll(x, wp - (TQ - 1), 1, stride=1, stride_axis=0)[:, :wp - TQ]

    n_blocks = g2_ref.shape[1]
    for il in range(SUBLANES):
        left = CMP_STRIDE * (SUBLANES - 1 - il)
        y = pltpu.roll(g2_ref[0], (2 * TQ - left) % (2 * TQ), 1)
        rr_ref[0, pl.ds(il, n_blocks, stride=SUBLANES), :] = y[:, :TQ]


def _bias_banks(rel_bias, n_qt, n_row_blocks, bank_k, bank_w, shift):
    nh = rel_bias.shape[1]
    wk = WINDOW + TQ
    gs = _bias_table(rel_bias, bank_k + TQ - 1 - jnp.arange(bank_k + bank_w + TQ), True, LOG2E)
    dw = WINDOW + TQ - 1 - jnp.arange(WINDOW + wk + TQ)
    gw = _bias_table(rel_bias, jnp.where(dw < WINDOW, dw, -1), True, LOG2E)
    n_blocks = -(-(n_qt + n_row_blocks - 1) // SUBLANES) * SUBLANES
    d2 = (TQ * (n_qt - 1 - jnp.arange(n_blocks))[:, None] + jnp.arange(2 * TQ)[None, :]
          - (CMP_BLOCK - 1) - CMP_STRIDE * (SUBLANES - 1))
    g2 = _bias_table(rel_bias, d2.reshape(-1), True, LOG2E).reshape(nh, n_blocks, 2 * TQ)
    spec3 = lambda a: pl.BlockSpec((1,) + a.shape[1:], lambda h: (h, 0, 0))
    outs = [jax.ShapeDtypeStruct((nh, TQ, bank_k + bank_w), F32), jax.ShapeDtypeStruct((nh, TQ, WINDOW + wk), F32),
            jax.ShapeDtypeStruct((nh, n_blocks * SUBLANES, TQ), F32)]
    gs, gw, g2 = gs[:, None] - shift, gw[:, None] - shift, g2 - shift
    return pl.pallas_call(
        _bias_bank_kernel,
        grid=(nh,),
        in_specs=[spec3(gs), spec3(gw), spec3(g2)],
        out_specs=[spec3(o) for o in outs],
        out_shape=outs,
        compiler_params=_cparams(("parallel",)),
        name="bias_banks",
    )(gs, gw, g2)


def _query_rows_bias(rel_bias, t0, n_tok, n_keys):
    dist = (t0 + jnp.arange(n_tok)[:, None] - jnp.arange(n_keys)[None, :]).reshape(-1)
    return _bias_table(rel_bias, dist, False).reshape(-1, n_tok, n_keys)


def _first_constant_distance():
    ratio = MAX_DISTANCE / (N_BUCKETS // 2)
    return int(math.ceil((N_BUCKETS // 2) * ratio ** ((N_BUCKETS // 2 - 1) / (N_BUCKETS // 2)))) + 2


def _topk_rows(score, blk, k_top):
    n_blk = score.shape[0]
    for _ in range(k_top):
        m = jnp.max(score, axis=0, keepdims=True)
        idx = jnp.min(jnp.where(score == m, blk, n_blk), axis=0, keepdims=True)
        score = jnp.where(blk == idx, REMOVED, score)
    return score == REMOVED


def _attn_prompt_kernel(q_ref, z0_ref, z1_ref, z2_ref, gl_ref, gq_ref, kcmp_ref, vcmp_ref,
                        ksa_ref, vsa_ref, kw_ref, vwa_ref, rr_ref, sb_ref, wb_ref, bounded_ref, o_ref,
                        q_scr, psum_scr, ocmp_scr, owin_scr, m_scr, acc_scr, s0_scr, s1_scr, *, n_qtiles, bank_k):
    qi = pl.program_id(1)
    q0 = qi * TQ
    n_cmp_rows = kcmp_ref.shape[1]
    n_sb = n_cmp_rows * CMP_STRIDE // SEL_BLOCK
    hg = HEADS_PER_GROUP
    q_gain = HEAD_DIM ** -0.5 * LOG2E
    wk = WINDOW + TQ

    def per_tile_branches(t, bounded):
        tile = qi + t
        t0 = tile * TQ
        for h in range(hg):
            qh = _rms_rows(q_ref[t * TQ:(t + 1) * TQ, h * HEAD_DIM:(h + 1) * HEAD_DIM], gq_ref[...]) * q_gain
            q_scr[t, h * TQ:(h + 1) * TQ, 0:HEAD_DIM] = qh.astype(BF16)
        q_all = q_scr[t, :, 0:HEAD_DIM]

        s_t = lax.dot_general(kcmp_ref[0], q_all, (((1,), (1,)), ((), ())), preferred_element_type=F32)
        rr_start = pl.multiple_of((n_qtiles - 1 - tile) * SUBLANES, SUBLANES)
        psum = jnp.zeros((n_cmp_rows, TQ), F32)
        for h in range(hg):
            sh = s_t[:, h * TQ:(h + 1) * TQ] + rr_ref[h, pl.ds(rr_start, n_cmp_rows), :]
            if bounded:
                e = jnp.exp2(sh)
            else:
                e = jnp.exp2(sh - jnp.maximum(jnp.max(sh, axis=0, keepdims=True), 0.5 * NEG))
            den = jnp.sum(e, axis=0, keepdims=True)
            p = e * (1.0 / jnp.where(den > 0.0, den, 1.0))
            psum = psum + p
            ocmp_scr[t, h * TQ:(h + 1) * TQ, :] = lax.dot_general(
                p.astype(BF16), vcmp_ref[0], (((0,), (0,)), ((), ())), preferred_element_type=F32)

        w0 = pl.multiple_of(jnp.maximum(t0 - WINDOW, 0), LANES)
        s_w = lax.dot_general(q_all, kw_ref[pl.ds(w0, wk), :], (((1,), (1,)), ((), ())),
                              preferred_element_type=F32)
        vwin = vwa_ref[pl.ds(w0, wk), :]
        wb_start = pl.multiple_of(WINDOW - (t0 - w0), LANES)
        for h in range(hg):
            sh = s_w[h * TQ:(h + 1) * TQ] + wb_ref[h, :, pl.ds(wb_start, wk)]
            e = jnp.exp2(sh if bounded else sh - jnp.max(sh, axis=1, keepdims=True))
            o_aug = jnp.dot(e.astype(BF16), vwin, preferred_element_type=F32)
            owin_scr[t, h * TQ:(h + 1) * TQ, :] = o_aug[:, 0:HEAD_DIM] / o_aug[:, HEAD_DIM:AUG_W]

        psum_scr[t, 0:SUBLANES, :] = jnp.zeros((SUBLANES, TQ), F32)
        psum_scr[t, SUBLANES:, :] = psum
        ratio = SEL_BLOCK // CMP_STRIDE
        imp = psum_scr[t, pl.ds(SUBLANES - 1, n_sb, stride=ratio), :]
        for k in range(ratio):
            imp = imp + psum_scr[t, pl.ds(SUBLANES + k, n_sb, stride=ratio), :]
        blk = lax.broadcasted_iota(jnp.int32, (n_sb, 1), 0)
        cur = (t0 + lax.broadcasted_iota(jnp.int32, (1, TQ), 1)) // SEL_BLOCK
        blk_ok = blk <= cur
        forced = (blk == 0) | (blk == cur) | (blk == cur - 1)
        score = jnp.where(blk_ok, jnp.where(forced, FORCE, imp), -FORCE)
        sel_t = _topk_rows(score, blk, min(N_SELECT, n_sb)) & blk_ok
        not_sel = jnp.where(sel_t, 0.0, 1.0).T
        if n_sb < HEAD_DIM:
            not_sel = jnp.concatenate([not_sel, jnp.zeros((TQ, HEAD_DIM - n_sb), F32)], axis=1)
        for h in range(hg):
            q_scr[t, h * TQ:(h + 1) * TQ, HEAD_DIM:AUG_W] = not_sel.astype(BF16)

    half = 0
    n_kv = (q0 + TQ + TK_SEL - 1) // TK_SEL
    rep = TK_SEL // LANES

    def scores(j, s_ref):
        k0 = pl.multiple_of(j * TK_SEL, TK_SEL)
        s_ref[...] = lax.dot_general(q_scr[half], ksa_ref[pl.ds(k0, TK_SEL), :], (((1,), (1,)), ((), ())),
                                     preferred_element_type=F32)

    def kv_tile(j, s_ref, bounded):
        k0 = pl.multiple_of(j * TK_SEL, TK_SEL)
        vt = vsa_ref[pl.ds(k0, TK_SEL), :]
        b_start = pl.multiple_of(jnp.maximum(bank_k - (q0 - k0), 0), LANES)
        for h in range(hg):
            rows = slice(h * TQ, (h + 1) * TQ)
            sh = s_ref[rows] + sb_ref[h, :, pl.ds(b_start, TK_SEL)]
            if bounded:
                acc_scr[rows] += jnp.dot(jnp.exp2(sh).astype(BF16), vt, preferred_element_type=F32)
            else:
                m_prev = m_scr[rows]
                m_next = jnp.maximum(m_prev, jnp.max(sh, axis=1, keepdims=True))
                alpha = jnp.exp2(m_prev - m_next)
                e = jnp.exp2(sh - jnp.tile(m_next, (1, rep)))
                acc_scr[rows] = (jnp.tile(alpha, (1, AUG_W // LANES)) * acc_scr[rows]
                                 + jnp.dot(e.astype(BF16), vt, preferred_element_type=F32))
                m_scr[rows] = m_next

    def selected_branch(bounded):
        m_scr[...] = jnp.full(m_scr.shape, NEG, F32)
        acc_scr[...] = jnp.zeros(acc_scr.shape, F32)

        def run(first, count):
            bufs = (s0_scr, s1_scr)
            for k in range(count):
                scores(first + k + 1, bufs[(k + 1) % 2])
                kv_tile(first + k, bufs[k % 2], bounded)

        def kv_run(i, carry):
            run(KV_RUN * i, KV_RUN)
            return carry

        scores(0, s0_scr)
        n_before = n_kv - 1
        lax.fori_loop(0, n_before // KV_RUN, kv_run, 0)
        size = KV_RUN // 2
        while size >= 2:
            start = n_before // (2 * size) * (2 * size)

            @pl.when(n_before - start >= size)
            def _(start=start, size=size):
                run(start, size)

            size //= 2

        done2 = n_before // 2 * 2

        @pl.when(n_before > done2)
        def _():
            scores(done2 + 1, s1_scr)
            kv_tile(done2, s0_scr, bounded)
            kv_tile(done2 + 1, s1_scr, bounded)

        @pl.when(n_before == done2)
        def _():
            kv_tile(done2, s0_scr, bounded)

    for bounded in (True, False):
        @pl.when(bounded_ref[0] == int(bounded))
        def _(bounded=bounded):
            per_tile_branches(0, bounded)
            selected_branch(bounded)

    sig_gl = _sigmoid(gl_ref[...])
    z_refs = (z0_ref, z1_ref, z2_ref)
    for h in range(hg):
        rows = slice(h * TQ, (h + 1) * TQ)
        cols = slice(h * HEAD_DIM, (h + 1) * HEAD_DIM)
        o_sel = acc_scr[rows, 0:HEAD_DIM] / acc_scr[rows, HEAD_DIM:AUG_W]
        out = jnp.zeros((TQ, HEAD_DIM), F32)
        for br, o_br in enumerate((ocmp_scr[half, rows, :], o_sel, owin_scr[half, rows, :])):
            z = z_refs[br][:, cols]
            c = br * hg + h
            out = out + sig_gl[:, c:c + 1] * o_br * (z * _sigmoid(z))
        o_ref[:, cols] = out.astype(BF16)


def _attn_prompt(proj, gl_g, g_q, kcmp, vcmp, ks_aug, vs_aug, kw_b, vw_aug, rr, sb, wb, bounded, bank_k):
    s_len = proj.shape[0]
    n_qt = s_len // TQ
    n_cmp_rows = kcmp.shape[2]
    gcols = N_HEADS * HEAD_DIM // GROUP_W
    qz = lambda seg: pl.BlockSpec((TQ, GROUP_W), lambda g, i, seg=seg: (i, seg * gcols + g))
    per_group = lambda shape: _resident((1,) + shape, lambda g, i: (g,) + (0,) * len(shape))
    kv_spec = lambda width: _resident((s_len, width), lambda g, i: (0, g))
    return pl.pallas_call(
        functools.partial(_attn_prompt_kernel, n_qtiles=n_qt, bank_k=bank_k),
        grid=(N_KV, n_qt),
        in_specs=[qz(0), qz(1), qz(2), qz(3),
                  pl.BlockSpec((TQ, LANES), lambda g, i: (i, g)),
                  _resident((1, HEAD_DIM), lambda g, i: (0, 0)),
                  per_group((n_cmp_rows, HEAD_DIM)), per_group((n_cmp_rows, HEAD_DIM)),
                  kv_spec(AUG_W), kv_spec(AUG_W), kv_spec(HEAD_DIM), kv_spec(AUG_W),
                  _resident((HEADS_PER_GROUP,) + rr.shape[1:], lambda g, i: (g, 0, 0)),
                  _resident((HEADS_PER_GROUP,) + sb.shape[1:], lambda g, i: (g, 0, 0)),
                  _resident((HEADS_PER_GROUP,) + wb.shape[1:], lambda g, i: (g, 0, 0)),
                  pl.BlockSpec(memory_space=pltpu.SMEM)],
        out_specs=pl.BlockSpec((TQ, GROUP_W), lambda g, i: (i, g)),
        out_shape=jax.ShapeDtypeStruct((s_len, N_HEADS * HEAD_DIM), BF16),
        scratch_shapes=[pltpu.VMEM((1, HEADS_PER_GROUP * TQ, AUG_W), BF16),
                        pltpu.VMEM((1, n_cmp_rows + SUBLANES, TQ), F32),
                        pltpu.VMEM((1, HEADS_PER_GROUP * TQ, HEAD_DIM), F32),
                        pltpu.VMEM((1, HEADS_PER_GROUP * TQ, HEAD_DIM), F32),
                        pltpu.VMEM((HEADS_PER_GROUP * TQ, LANES), F32),
                        pltpu.VMEM((HEADS_PER_GROUP * TQ, AUG_W), F32),
                        pltpu.VMEM((HEADS_PER_GROUP * TQ, TK_SEL), F32),
                        pltpu.VMEM((HEADS_PER_GROUP * TQ, TK_SEL), F32)],
        compiler_params=_cparams(("arbitrary", "arbitrary")),
        name="attn_prompt",
    )(proj, proj, proj, proj, gl_g, g_q.reshape(1, HEAD_DIM), kcmp[0], vcmp[0],
      ks_aug, vs_aug, kw_b, vw_aug, rr, sb, wb, bounded)


def _attn_sample_kernel(*refs, n_tok, pages_per_chunk, page, past_len):
    pt_ref = refs[0]
    q_ref, z_ref, gl_ref, gq_ref, kcmp_ref, vcmp_ref = refs[1:7]
    kpages = refs[7:7 + pages_per_chunk]
    vpages = refs[7 + pages_per_chunk:7 + 2 * pages_per_chunk]
    (ksn_ref, vsn_ref, kwn_ref, vwn_ref, kwc_ref, vwc_ref, bcmp_ref, bsel_ref, bwin_ref, ov_ref, ex_ref,
     o_ref, sel_scr, m_scr, l_scr, acc_scr, q_scr, ocmp_scr) = refs[7 + 2 * pages_per_chunk:]
    del pt_ref
    c = pl.program_id(1)
    n_chunks = pl.num_programs(1)
    hg = HEADS_PER_GROUP
    nr = hg * n_tok
    scale = HEAD_DIM ** -0.5
    n_cmp_rows = kcmp_ref.shape[2]
    n_sb = ov_ref.shape[0]
    chunk = pages_per_chunk * page
    tok = lax.broadcasted_iota(jnp.int32, (N_KV * nr, 1), 0) % n_tok

    @pl.when(c == 0)
    def _():
        q_scr[...] = _rms_rows(q_ref[0], gq_ref[...]).astype(BF16)
        m_scr[...] = jnp.full(m_scr.shape, NEG, F32)
        l_scr[...] = jnp.zeros(l_scr.shape, F32)
        acc_scr[...] = jnp.zeros(acc_scr.shape, F32)
        ci = lax.broadcasted_iota(jnp.int32, (1, n_cmp_rows), 1)
        cmp_valid = ci < n_cmp_rows - 1
        s = jnp.concatenate(
            [lax.dot_general(q_scr[g * nr:(g + 1) * nr], kcmp_ref[0, g], (((1,), (1,)), ((), ())),
                             preferred_element_type=F32) for g in range(N_KV)], axis=0)
        s = jnp.where(cmp_valid, s * scale + bcmp_ref[...], NEG)
        e = jnp.where(cmp_valid, jnp.exp(s - jnp.max(s, axis=1, keepdims=True)), 0.0)
        p = e / jnp.sum(e, axis=1, keepdims=True)
        pb = p.astype(BF16)
        psums = []
        for g in range(N_KV):
            rows = slice(g * nr, (g + 1) * nr)
            ocmp_scr[rows] = jnp.dot(pb[rows], vcmp_ref[0, g], preferred_element_type=F32)
            psum = p[g * nr:g * nr + n_tok]
            for h in range(1, hg):
                psum = psum + p[g * nr + h * n_tok:g * nr + (h + 1) * n_tok]
            psums.append(psum)
        rest = jnp.concatenate(psums, axis=0)
        imp_t = jnp.zeros((n_sb, N_KV * n_tok), F32)
        for _ in range(3):
            piece = rest.astype(BF16)
            rest = rest - piece.astype(F32)
            imp_t = imp_t + lax.dot_general(ov_ref[...], piece, (((1,), (1,)), ((), ())),
                                            preferred_element_type=F32)
        blk = lax.broadcasted_iota(jnp.int32, (n_sb, 1), 0)
        forced = (blk == 0) | (blk == n_sb - 1)
        score = jnp.where(forced, FORCE, imp_t)
        sel_t = _topk_rows(score, blk, min(N_SELECT, n_sb + 1) - 1)
        sel_f = jnp.where(sel_t, 1.0, 0.0)
        if N_KV * n_tok < LANES:
            sel_f = jnp.concatenate([sel_f, jnp.zeros((n_sb, LANES - N_KV * n_tok), F32)], axis=1)
        sel_f = sel_f.T
        for g in range(N_KV):
            sel_scr[g * nr:(g + 1) * nr] = jnp.tile(sel_f[g * n_tok:(g + 1) * n_tok], (hg, 1)).astype(BF16)

    def group_rows(ref, g):
        return ref[0, pl.ds(g, ref.shape[1] // N_KV, stride=N_KV), :].astype(BF16)

    def all_scores(k_of_group):
        return jnp.concatenate(
            [lax.dot_general(q_scr[g * nr:(g + 1) * nr], k_of_group(g), (((1,), (1,)), ((), ())),
                             preferred_element_type=F32) for g in range(N_KV)], axis=0)

    def all_values(e, v_of_group):
        eb = e.astype(BF16)
        return jnp.concatenate(
            [jnp.dot(eb[g * nr:(g + 1) * nr], v_of_group(g), preferred_element_type=F32) for g in range(N_KV)],
            axis=0)

    def online_update(sh, v_of_group):
        m_prev = m_scr[...]
        m_next = jnp.maximum(m_prev, jnp.max(sh, axis=1, keepdims=True))
        alpha = jnp.exp(m_prev - m_next)
        e = jnp.exp(sh - m_next[:, 0:1])
        l_scr[...] = alpha * l_scr[...] + jnp.sum(e, axis=1, keepdims=True)
        acc_scr[...] = alpha * acc_scr[...] + all_values(e, v_of_group)
        m_scr[...] = m_next

    k0 = pl.multiple_of(c * chunk, chunk)
    s = all_scores(lambda g: jnp.concatenate([group_rows(r, g) for r in kpages], axis=0))
    picked = jnp.dot(sel_scr[...], ex_ref[:, pl.ds(k0, chunk)], preferred_element_type=F32)
    sh = jnp.where(picked > 0.5, s * scale + bsel_ref[:, pl.ds(k0, chunk)], NEG)
    online_update(sh, lambda g: jnp.concatenate([group_rows(r, g) for r in vpages], axis=0))

    @pl.when(c == n_chunks - 1)
    def _():
        new_w = ksn_ref.shape[1] // N_KV
        jn = lax.broadcasted_iota(jnp.int32, (1, new_w), 1)
        new_valid = jn <= tok
        wbuf = kwc_ref.shape[1] // N_KV
        jw = lax.broadcasted_iota(jnp.int32, (1, wbuf), 1)
        dist_c = (past_len + tok) - (past_len - wbuf + jw)
        win_valid_c = dist_c < WINDOW
        s = all_scores(lambda g: group_rows(ksn_ref, g))
        sh = jnp.where(new_valid, s * scale + bsel_ref[:, pl.ds(past_len, new_w)], NEG)
        online_update(sh, lambda g: group_rows(vsn_ref, g))
        o_sel = acc_scr[...] / l_scr[...]
        s_c = all_scores(lambda g: group_rows(kwc_ref, g))
        s_n = all_scores(lambda g: group_rows(kwn_ref, g))
        s_c = jnp.where(win_valid_c, s_c * scale + bwin_ref[:, 0:wbuf], NEG)
        s_n = jnp.where(new_valid, s_n * scale + bwin_ref[:, wbuf:wbuf + new_w], NEG)
        mx = jnp.maximum(jnp.max(s_c, axis=1, keepdims=True), jnp.max(s_n, axis=1, keepdims=True))
        e_c = jnp.exp(s_c - mx)
        e_n = jnp.exp(s_n - mx)
        den = jnp.sum(e_c, axis=1, keepdims=True) + jnp.sum(e_n, axis=1, keepdims=True)
        o_win = (all_values(e_c, lambda g: group_rows(vwc_ref, g))
                 + all_values(e_n, lambda g: group_rows(vwn_ref, g))) / den
        sig_gl = _sigmoid(gl_ref[0])
        out = jnp.zeros((N_KV * nr, HEAD_DIM), F32)
        for br, o_br in enumerate((ocmp_scr[...], o_sel, o_win)):
            z = z_ref[0, br]
            out = out + sig_gl[br] * o_br * (z * _sigmoid(z))
        o_ref[0] = out.astype(BF16)


def _attn_sample(q_s, z_s, gl_s, g_q, kcmp, vcmp, ksel_pages, vsel_pages, page_table, ks_new, vs_new,
                 kw_new, vw_new, kw_cache, vw_cache, bcmp, bsel, bwin, overlap, expand, n_tok):
    nb = q_s.shape[0]
    n_pages = page_table.shape[1]
    page = ksel_pages.shape[1] // N_KV
    ppc = min(8, n_pages)
    n_chunks = n_pages // ppc
    nrow = q_s.shape[1]
    per_b = lambda shape: pl.BlockSpec((1,) + shape, lambda b, c, pt: (b,) + (0,) * len(shape))
    full = lambda shape: _resident(shape, lambda b, c, pt: (0,) * len(shape))
    page_spec = lambda i: pl.BlockSpec((1, page * N_KV, HEAD_DIM),
                                       lambda b, c, pt, i=i: (pt[b, c * ppc + i], 0, 0))
    in_specs = ([per_b(q_s.shape[1:]), per_b(z_s.shape[1:]), per_b(gl_s.shape[1:]), full((1, HEAD_DIM)),
                 per_b(kcmp.shape[1:]), per_b(vcmp.shape[1:])]
                + [page_spec(i) for i in range(ppc)] + [page_spec(i) for i in range(ppc)]
                + [per_b(ks_new.shape[1:])] * 4 + [per_b(kw_cache.shape[1:])] * 2
                + [full(bcmp.shape), full(bsel.shape), full(bwin.shape), full(overlap.shape), full(expand.shape)])
    grid_spec = pltpu.PrefetchScalarGridSpec(
        num_scalar_prefetch=1,
        grid=(nb, n_chunks),
        in_specs=in_specs,
        out_specs=pl.BlockSpec((1, nrow, HEAD_DIM), lambda b, c, pt: (b, 0, 0)),
        scratch_shapes=[pltpu.VMEM((nrow, expand.shape[0]), BF16),
                        pltpu.VMEM((nrow, LANES), F32), pltpu.VMEM((nrow, LANES), F32),
                        pltpu.VMEM((nrow, HEAD_DIM), F32), pltpu.VMEM((nrow, HEAD_DIM), BF16),
                        pltpu.VMEM((nrow, HEAD_DIM), F32)],
    )
    return pl.pallas_call(
        functools.partial(_attn_sample_kernel, n_tok=n_tok, pages_per_chunk=ppc, page=page,
                          past_len=n_pages * page),
        grid_spec=grid_spec,
        out_shape=jax.ShapeDtypeStruct((nb, nrow, HEAD_DIM), BF16),
        compiler_params=_cparams(("arbitrary", "arbitrary")),
        name="attn_sample",
    )(page_table, q_s, z_s, gl_s, g_q.reshape(1, HEAD_DIM), kcmp, vcmp,
      *([ksel_pages] * ppc), *([vsel_pages] * ppc), ks_new, vs_new, kw_new, vw_new, kw_cache, vw_cache,
      bcmp, bsel, bwin, overlap, expand)


def kernel(x_prompt, x_sample, p_prompt, p_sample, state_conv, state_h, cache_k_cmp, cache_v_cmp, cache_k_sel, cache_v_sel, cache_k_win, cache_v_win, page_table, norm_g, a_w_in, a_conv_w, a_conv_b, a_w_r, a_b_r, a_w_i, a_b_i, a_lam, a_w_out, kv_norm_g, w_kv, g_k_cmp, g_k_sel, g_k_win, cmp_pos, cmp_w1a, cmp_w1b, cmp_b1, cmp_w2, cmp_b2, b_w_in, b_g_q, b_w_out, rel_bias, ple_w_proj, ple_norm_g, ple_w_gate):
    bp, s_len, d = x_prompt.shape
    db, ds, _ = x_sample.shape
    n_pool, page = cache_k_sel.shape[:2]
    n_pages = page_table.shape[1]
    past_len = n_pages * page
    wbuf = cache_k_win.shape[1]
    hd = N_HEADS * HEAD_DIM
    assert bp == 1 and norm_g.shape[0] == 2 and a_w_in.shape[0] == 1 and b_w_in.shape[0] == 1
    assert d == hd and s_len % TK_SEL == 0 and s_len >= WINDOW + TQ and page % SEL_BLOCK == 0
    assert wbuf == WINDOW and ds <= SEL_BLOCK and past_len % TK_SEL == 0 and s_len // SEL_BLOCK <= HEAD_DIM

    w_in0 = a_w_in[0].astype(BF16)
    w_ri = jnp.concatenate([a_w_r[0], a_w_i[0]], axis=-1).astype(BF16)
    w_out0 = a_w_out[0].astype(BF16)
    w_gate = ple_w_gate.astype(BF16)
    w_proj = ple_w_proj.astype(BF16)
    w_kv_b = w_kv.astype(BF16)
    w_in1 = b_w_in[0].astype(BF16)
    w_gl = b_w_in[0][:, 4 * hd:].reshape(d, 3, N_KV, HEADS_PER_GROUP).transpose(0, 2, 1, 3)
    w_gl = jnp.pad(w_gl.reshape(d, N_KV, 3 * HEADS_PER_GROUP), ((0, 0), (0, 0), (0, LANES - 3 * HEADS_PER_GROUP)))
    w_gl = w_gl.reshape(d, N_KV * LANES).astype(BF16)
    w_out1 = b_w_out[0].astype(BF16)
    w1 = jnp.concatenate([cmp_w1a, cmp_w1b], axis=-1).astype(BF16)
    w2 = cmp_w2.astype(BF16)
    pe = cmp_pos.reshape(2, 2, 1, CMP_STRIDE * HEAD_DIM)
    pe_hi = pe.astype(BF16)
    pe_lo = (pe - pe_hi.astype(F32)).astype(BF16)
    pe_rows = jnp.concatenate([pe_hi, pe_lo, jnp.zeros((2, 2, SUBLANES - 2, pe.shape[-1]), BF16)], axis=2)
    pe_rows = pe_rows.reshape(2, 2 * SUBLANES, pe.shape[-1])
    b1 = cmp_b1.reshape(2, 1, -1)
    b2 = cmp_b2.reshape(2, 1, HEAD_DIM)

    xp = x_prompt.reshape(s_len, d)
    xs = x_sample.reshape(db * ds, d)
    uz_p = _norm_matmul(xp, norm_g[0], w_in0, 1024)
    uz_s = _norm_matmul(xs, norm_g[0], w_in0, 1024)
    rg = (a_conv_w[0], a_conv_b[0], w_ri, a_b_r[0], a_b_i[0], a_lam[0])
    gated_p, h_p = _rglru_prompt(uz_p, jnp.zeros((SUBLANES, d), F32), jnp.zeros((1, d), F32), *rg)
    conv_p = uz_p[s_len - (CONV_W - 1):, :d].reshape(1, bp, CONV_W - 1, d)
    u_s = uz_s[:, :d].reshape(db, ds, d).transpose(1, 0, 2)
    z_s = uz_s[:, d:].reshape(db, ds, d).transpose(1, 0, 2)
    up_s = jnp.concatenate([state_conv[0].transpose(1, 0, 2), u_s], axis=0)
    gated_s, h_s = _rglru_sample(up_s, z_s, state_h[0], *rg)
    conv_s = up_s[ds:].transpose(1, 0, 2)[None]
    gated_s = gated_s.transpose(1, 0, 2).reshape(db * ds, d)
    ple0 = (w_proj[0], ple_norm_g[0], w_gate[0])
    h1_p = _outproj_ple(gated_p, xp, p_prompt[0].reshape(s_len, -1), w_out0, *ple0)
    h1_s = _outproj_ple(gated_s, xs, p_sample[0].reshape(db * ds, -1), w_out0, *ple0)

    kv_p = _kv_proj(h1_p, kv_norm_g, w_kv_b, g_k_sel, g_k_win)
    kv_s = _kv_proj(h1_s, kv_norm_g, w_kv_b, g_k_sel, g_k_win)
    kc_p, vc_p, ks_p, vs_p, kw_p, vw_p = kv_p[:6]
    kc_s, vc_s, ks_s, vs_s, kw_s, vw_s = kv_s[:6]

    p_pages = s_len // page
    ident = jnp.minimum(jnp.arange(p_pages + 1, dtype=jnp.int32), p_pages - 1)[None]
    cmp_w = (pe_rows, w1, b1, w2, b2, g_k_cmp)
    pieces = lambda x: x.reshape(-1, page // CMP_STRIDE, PIECE_ROWS, HEAD_DIM)
    kcmp_p, vcmp_p = _compress(pieces(kc_p), pieces(vc_p), ident, *cmp_w)
    table_ext = jnp.concatenate([page_table, page_table[:, -1:]], axis=1)
    kcmp_s, vcmp_s = _compress(pieces(cache_k_cmp), pieces(cache_v_cmp), table_ext, *cmp_w)

    proj_p, gl_p = _norm_matmul(h1_p, norm_g[1], w_in1, 1024, w_gl, n=4 * hd)
    proj_s, gl_s = _norm_matmul(h1_s, norm_g[1], w_in1, 1024, w_gl, n=4 * hd)

    n_qt = s_len // TQ
    n_cmp_rows = s_len // CMP_STRIDE
    bank_w = max(TK_SEL, WINDOW + TQ)
    bank_k = min(-(-(_first_constant_distance() + bank_w) // LANES) * LANES, s_len)
    gmax = lambda g: jnp.max(jnp.abs(g))
    k_gain = jnp.maximum(jnp.maximum(gmax(g_k_sel), gmax(g_k_win)), gmax(g_k_cmp))
    score_bound = (1.02 * HEAD_DIM * HEAD_DIM ** -0.5 * gmax(b_g_q[0]) * k_gain + gmax(rel_bias)) * LOG2E
    bounded = score_bound <= MAX_SCORE_BOUND
    shift = jnp.where(bounded, score_bound, 0.0)
    sb, wb, rr = _bias_banks(rel_bias, n_qt, n_cmp_rows // SUBLANES, bank_k, bank_w, shift)
    o_p = _attn_prompt(proj_p, gl_p, b_g_q[0], kcmp_p, vcmp_p, kv_p[6], kv_p[7], kv_p[8], kv_p[9],
                       rr, sb, wb, bounded.astype(jnp.int32).reshape(1), bank_k)

    def head_rows(x):
        return x.reshape(db, ds, N_HEADS, HEAD_DIM).transpose(0, 2, 1, 3).reshape(db, N_HEADS * ds, HEAD_DIM)

    q_s = head_rows(proj_s[:, :hd])
    zz_s = jnp.stack([head_rows(proj_s[:, (1 + br) * hd:(2 + br) * hd]) for br in range(3)], axis=1)
    gl_s = gl_s.reshape(db, ds, N_KV, LANES)[..., :3 * HEADS_PER_GROUP].reshape(db, ds, N_KV, 3, HEADS_PER_GROUP)
    gl_s = gl_s.transpose(0, 3, 2, 4, 1).reshape(db, 3, N_HEADS * ds, 1)
    new_w = LANES
    group_rows = lambda x: x.reshape(db, -1, HEAD_DIM)
    pad_new = lambda x: jnp.pad(group_rows(x), ((0, 0), (0, (new_w - ds) * N_KV), (0, 0)))
    n_cmp_s = past_len // CMP_STRIDE
    bsel = _query_rows_bias(rel_bias, past_len, ds, past_len + new_w).reshape(N_HEADS * ds, -1)
    bwin = bsel[:, past_len - wbuf:]
    bcmp = bsel[:, CMP_BLOCK - 1::CMP_STRIDE][:, :n_cmp_s]
    n_sb_s = past_len // SEL_BLOCK
    ci = jnp.arange(n_cmp_s)[:, None] * CMP_STRIDE
    sj = jnp.arange(n_sb_s)[None, :] * SEL_BLOCK
    overlap = ((ci < sj + SEL_BLOCK) & (ci + CMP_BLOCK > sj) & (ci < past_len - CMP_STRIDE)).T.astype(BF16)
    expand_s = (jnp.arange(past_len)[None, :] // SEL_BLOCK == jnp.arange(n_sb_s)[:, None]).astype(BF16)
    o_s = _attn_sample(q_s, zz_s, gl_s, b_g_q[0], kcmp_s, vcmp_s,
                       cache_k_sel.reshape(n_pool, -1, HEAD_DIM), cache_v_sel.reshape(n_pool, -1, HEAD_DIM),
                       page_table, pad_new(ks_s), pad_new(vs_s), pad_new(kw_s), pad_new(vw_s),
                       group_rows(cache_k_win), group_rows(cache_v_win),
                       bcmp, bsel, bwin, overlap, expand_s, ds)
    o_s = o_s.reshape(db, N_HEADS, ds, HEAD_DIM).transpose(0, 2, 1, 3).reshape(db * ds, hd)

    ple1 = (w_proj[1], ple_norm_g[1], w_gate[1])
    y_p = _outproj_ple(o_p, h1_p, p_prompt[1].reshape(s_len, -1), w_out1, *ple1)
    y_s = _outproj_ple(o_s, h1_s, p_sample[1].reshape(db * ds, -1), w_out1, *ple1)

    rows4 = lambda x, b, t: x.reshape(b, t, N_KV, HEAD_DIM)
    wk_p = min(WINDOW, s_len)
    kwin_s = jnp.concatenate([cache_k_win, rows4(kw_s, db, ds)], axis=1)[:, -WINDOW:]
    vwin_s = jnp.concatenate([cache_v_win, rows4(vw_s, db, ds)], axis=1)[:, -WINDOW:]
    return (y_p.reshape(bp, s_len, d), y_s.reshape(db, ds, d),
            conv_p, h_p.reshape(1, bp, d),
            rows4(kc_p, bp, s_len), rows4(vc_p, bp, s_len), rows4(ks_p, bp, s_len), rows4(vs_p, bp, s_len),
            rows4(kw_p, bp, s_len)[:, -wk_p:], rows4(vw_p, bp, s_len)[:, -wk_p:],
            conv_s, h_s.reshape(1, db, d),
            rows4(kc_s, db, ds), rows4(vc_s, db, ds), rows4(ks_s, db, ds), rows4(vs_s, db, ds),
            kwin_s, vwin_s)
```

```python
import functools
import math

import jax
import jax.numpy as jnp
from jax import lax
from jax.experimental import pallas as pl
from jax.experimental.pallas import tpu as pltpu

F32 = jnp.float32
BF16 = jnp.bfloat16

N_RNN_BLOCKS = 8
CONV_W = 4
LRU_C = 8.0
N_HEADS = 16
HEAD_DIM = 128
N_KV = 4
HEADS_PER_GROUP = N_HEADS // N_KV
GROUP_W = HEADS_PER_GROUP * HEAD_DIM
KV_W = N_KV * HEAD_DIM
CMP_BLOCK = 32
CMP_STRIDE = 16
SEL_BLOCK = 64
N_SELECT = 16
WINDOW = 512
N_BUCKETS = 32
MAX_DISTANCE = 4096
EPS = 1e-6
NEG = -1e30
FORCE = 1e9
REMOVED = -2.0 ** 126
LOG2E = math.log2(math.e)
MAX_SCORE_BOUND = 60.0

LANES = 128
SUBLANES = 8
VMEM_LIMIT = 56 * 1024 * 1024

TQ = 128
TK_SEL = 512
KV_RUN = 8
CMP_PAGES_PER_STEP = 16
SAMPLE_PAGES_PER_STEP = 16


def _cparams(sem):
    return pltpu.CompilerParams(dimension_semantics=sem, vmem_limit_bytes=VMEM_LIMIT)


def _resident(shape, index_map):
    return pl.BlockSpec(shape, index_map, pipeline_mode=pl.Buffered(1))


def _rms_rows(x, g):
    return x * lax.rsqrt(jnp.mean(x * x, axis=-1, keepdims=True) + EPS) * g


def _sigmoid(x):
    return 1.0 / (1.0 + jnp.exp(-x))


def _norm_matmul_kernel(x_ref, g_ref, w_ref, *rest):
    if len(rest) == 4:
        w2_ref, o_ref, o2_ref, xn_ref = rest
    else:
        (o_ref, xn_ref), w2_ref, o2_ref = rest, None, None

    @pl.when(pl.program_id(1) == 0)
    def _():
        xn_ref[...] = _rms_rows(x_ref[...], g_ref[...]).astype(BF16)
        if w2_ref is not None:
            o2_ref[...] = jnp.dot(xn_ref[...], w2_ref[...], preferred_element_type=F32)

    o_ref[...] = jnp.dot(xn_ref[...], w_ref[...], preferred_element_type=F32)


def _norm_matmul(x, g, w, tn, w2=None, n=None):
    m, d = x.shape
    n = w.shape[1] if n is None else n
    tm = min(m, 1024)
    in_specs = [pl.BlockSpec((tm, d), lambda i, j: (i, 0)),
                pl.BlockSpec((1, d), lambda i, j: (0, 0)),
                pl.BlockSpec((d, tn), lambda i, j: (0, j))]
    out_specs = [pl.BlockSpec((tm, tn), lambda i, j: (i, j))]
    out_shape = [jax.ShapeDtypeStruct((m, n), F32)]
    args = [x, g.reshape(1, d), w]
    if w2 is not None:
        in_specs.append(_resident(w2.shape, lambda i, j: (0, 0)))
        out_specs.append(pl.BlockSpec((tm, w2.shape[1]), lambda i, j: (i, 0)))
        out_shape.append(jax.ShapeDtypeStruct((m, w2.shape[1]), F32))
        args.append(w2)
    outs = pl.pallas_call(
        _norm_matmul_kernel,
        grid=(m // tm, n // tn),
        in_specs=in_specs,
        out_specs=out_specs,
        out_shape=out_shape,
        scratch_shapes=[pltpu.VMEM((tm, d), BF16)],
        compiler_params=_cparams(("parallel", "arbitrary")),
        name="norm_matmul",
    )(*args)
    return outs if w2 is not None else outs[0]


def _lru_gates(xc, wri_ref, br_ref, bi_ref, sp_ref, n, bw):
    sl = slice(n * bw, (n + 1) * bw)
    ri = jnp.dot(xc.astype(BF16), wri_ref[n], preferred_element_type=F32)
    r = _sigmoid(ri[:, :bw] + br_ref[:, sl])
    i = _sigmoid(ri[:, bw:] + bi_ref[:, sl])
    log_a = (-LRU_C) * r * sp_ref[:, sl]
    a = jnp.exp(log_a)
    b = jnp.sqrt(1.0 - a * a) * (i * xc)
    return a, b


def _softplus(x):
    return jnp.maximum(x, 0.0) + jnp.log(1.0 + jnp.exp(-jnp.abs(x)))


def _rglru_prompt_kernel(u_ref, z_ref, cinit_ref, hinit_ref, cw_ref, cb_ref, wri_ref, br_ref, bi_ref,
                         lam_ref, g_ref, hlast_ref, ubuf, a_scr, b_scr, hs_scr, hcar, sp_scr):
    tm, d = u_ref.shape
    bw = d // N_RNN_BLOCKS

    @pl.when(pl.program_id(0) == 0)
    def _():
        ubuf[0:SUBLANES, :] = cinit_ref[...]
        hcar[...] = hinit_ref[...]
        sp_scr[...] = _softplus(-lam_ref[...])

    ubuf[SUBLANES:, :] = u_ref[...]
    for n in range(N_RNN_BLOCKS):
        sl = slice(n * bw, (n + 1) * bw)
        up = ubuf[:, sl]
        xc = cb_ref[:, sl] + up * cw_ref[CONV_W - 1:CONV_W, sl]
        for k in range(CONV_W - 1):
            xc = xc + pltpu.roll(up, CONV_W - 1 - k, axis=0) * cw_ref[k:k + 1, sl]
        xc = xc[SUBLANES:]
        a, b = _lru_gates(xc, wri_ref, br_ref, bi_ref, sp_scr, n, bw)
        a_scr[:, sl] = a
        b_scr[:, sl] = b
    ubuf[0:SUBLANES, :] = u_ref[tm - SUBLANES:tm, :]

    def step(t, h):
        h = a_scr[pl.ds(t, 1), :] * h + b_scr[pl.ds(t, 1), :]
        hs_scr[pl.ds(t, 1), :] = h
        return h

    h = lax.fori_loop(0, tm, step, hcar[...], unroll=8)
    hcar[...] = h
    hlast_ref[...] = h
    z = z_ref[...]
    g_ref[...] = (hs_scr[...] * (z * _sigmoid(z))).astype(BF16)


def _rglru_prompt(uz, conv_init, h_init, cw, cb, wri, br, bi, lam):
    t, d2 = uz.shape
    d = d2 // 2
    tm = min(t, 256)
    full = lambda shape: pl.BlockSpec(shape, lambda i: (0,) * len(shape))
    return pl.pallas_call(
        _rglru_prompt_kernel,
        grid=(t // tm,),
        in_specs=[pl.BlockSpec((tm, d), lambda i: (i, 0)),
                  pl.BlockSpec((tm, d), lambda i: (i, 1)),
                  full((SUBLANES, d)), full((1, d)), full((CONV_W, d)), full((1, d)),
                  full(wri.shape), full((1, d)), full((1, d)), full((1, d))],
        out_specs=[pl.BlockSpec((tm, d), lambda i: (i, 0)), full((1, d))],
        out_shape=[jax.ShapeDtypeStruct((t, d), BF16), jax.ShapeDtypeStruct((1, d), F32)],
        scratch_shapes=[pltpu.VMEM((tm + SUBLANES, d), F32), pltpu.VMEM((tm, d), F32),
                        pltpu.VMEM((tm, d), F32), pltpu.VMEM((tm, d), F32),
                        pltpu.VMEM((1, d), F32), pltpu.VMEM((1, d), F32)],
        compiler_params=_cparams(("arbitrary",)),
        name="rglru_prompt",
    )(uz, uz, conv_init, h_init, cw, cb.reshape(1, d), wri, br.reshape(1, d), bi.reshape(1, d),
      lam.reshape(1, d))


def _rglru_sample_kernel(up_ref, z_ref, h0_ref, cw_ref, cb_ref, wri_ref, br_ref, bi_ref, lam_ref,
                         g_ref, hlast_ref, sp_scr):
    n_t = z_ref.shape[0]
    d = z_ref.shape[2]
    bw = d // N_RNN_BLOCKS
    sp_scr[...] = _softplus(-lam_ref[...])
    for n in range(N_RNN_BLOCKS):
        sl = slice(n * bw, (n + 1) * bw)
        h = h0_ref[:, sl]
        for t in range(n_t):
            xc = cb_ref[:, sl]
            for k in range(CONV_W):
                xc = xc + up_ref[t + k, :, sl] * cw_ref[k:k + 1, sl]
            a, b = _lru_gates(xc, wri_ref, br_ref, bi_ref, sp_scr, n, bw)
            h = a * h + b
            z = z_ref[t, :, sl]
            g_ref[t, :, sl] = (h * (z * _sigmoid(z))).astype(BF16)
        hlast_ref[:, sl] = h


def _rglru_sample(up, z, h0, cw, cb, wri, br, bi, lam):
    n_t, nb, d = z.shape
    return pl.pallas_call(
        _rglru_sample_kernel,
        out_shape=[jax.ShapeDtypeStruct((n_t, nb, d), BF16), jax.ShapeDtypeStruct((nb, d), F32)],
        scratch_shapes=[pltpu.VMEM((1, d), F32)],
        compiler_params=pltpu.CompilerParams(vmem_limit_bytes=VMEM_LIMIT),
        name="rglru_sample",
    )(up, z, h0, cw, cb.reshape(1, d), wri, br.reshape(1, d), bi.reshape(1, d), lam.reshape(1, d))


def _outproj_ple_kernel(a_ref, res_ref, p_ref, wo_ref, wp_ref, gn_ref, wg_ref, o_ref, h_scr, hn_scr):
    tm, d = res_ref.shape
    ch = min(d, 512)
    a = a_ref[...]
    ssq = jnp.zeros((tm, 1), F32)
    for c in range(d // ch):
        sl = slice(c * ch, (c + 1) * ch)
        h = res_ref[:, sl] + jnp.dot(a, wo_ref[:, sl], preferred_element_type=F32)
        h_scr[:, sl] = h
        ssq = ssq + jnp.sum(h * h, axis=-1, keepdims=True)
    inv = lax.rsqrt(ssq * (1.0 / d) + EPS)
    for c in range(d // ch):
        sl = slice(c * ch, (c + 1) * ch)
        hn_scr[:, sl] = (h_scr[:, sl] * inv * gn_ref[:, sl]).astype(BF16)
    pb = p_ref[...].astype(BF16)
    hn = hn_scr[...]
    for c in range(d // ch):
        sl = slice(c * ch, (c + 1) * ch)
        gate = _sigmoid(jnp.dot(hn, wg_ref[:, sl], preferred_element_type=F32))
        pp = jnp.dot(pb, wp_ref[:, sl], preferred_element_type=F32)
        o_ref[:, sl] = h_scr[:, sl] + pp * gate


def _outproj_ple(a, res, p, w_out, w_proj, g_norm, w_gate):
    m, d = res.shape
    pd = p.shape[1]
    tm = min(m, 512)
    return pl.pallas_call(
        _outproj_ple_kernel,
        grid=(m // tm,),
        in_specs=[pl.BlockSpec((tm, a.shape[1]), lambda i: (i, 0)),
                  pl.BlockSpec((tm, d), lambda i: (i, 0)),
                  pl.BlockSpec((tm, pd), lambda i: (i, 0)),
                  _resident(w_out.shape, lambda i: (0, 0)),
                  _resident(w_proj.shape, lambda i: (0, 0)),
                  _resident((1, d), lambda i: (0, 0)),
                  _resident(w_gate.shape, lambda i: (0, 0))],
        out_specs=pl.BlockSpec((tm, d), lambda i: (i, 0)),
        out_shape=jax.ShapeDtypeStruct((m, d), F32),
        scratch_shapes=[pltpu.VMEM((tm, d), F32), pltpu.VMEM((tm, d), BF16)],
        compiler_params=_cparams(("parallel",)),
        name="outproj_ple",
    )(a, res, p, w_out, w_proj, g_norm.reshape(1, d), w_gate)


AUG_W = 2 * HEAD_DIM
MASK_OFF = -2.0 ** 60


def _kv_proj_kernel(x_ref, g_ref, w_ref, gsel_ref, gwin_ref,
                    kc_ref, vc_ref, ks_ref, vs_ref, kw_ref, vw_ref, ksa_ref, vsa_ref, kwb_ref, vwa_ref):
    tm = x_ref.shape[0]
    xn = _rms_rows(x_ref[...], g_ref[...]).astype(BF16)
    outs = (kc_ref, vc_ref, ks_ref, vs_ref, kw_ref, vw_ref)
    head_gain = {2: gsel_ref, 4: gwin_ref}
    pos = pl.program_id(0) * tm + lax.broadcasted_iota(jnp.int32, (tm, HEAD_DIM), 0)
    blk = lax.broadcasted_iota(jnp.int32, (tm, HEAD_DIM), 1)
    block_mask = jnp.where(blk == pos // SEL_BLOCK, MASK_OFF, 0.0).astype(BF16)
    ones = jnp.ones((tm, HEAD_DIM), BF16)
    for s in range(6):
        y = jnp.dot(xn, w_ref[:, s * KV_W:(s + 1) * KV_W], preferred_element_type=F32)
        for g in range(N_KV):
            yg = y[:, g * HEAD_DIM:(g + 1) * HEAD_DIM]
            if s in head_gain:
                yg = _rms_rows(yg, head_gain[s][...])
            outs[s][pl.ds(g, tm, stride=N_KV), :] = yg
            lo = slice(g * AUG_W, g * AUG_W + HEAD_DIM)
            hi = slice(g * AUG_W + HEAD_DIM, (g + 1) * AUG_W)
            if s == 2:
                ksa_ref[:, lo] = yg.astype(BF16)
                ksa_ref[:, hi] = block_mask
            elif s == 3:
                vsa_ref[:, lo] = yg.astype(BF16)
                vsa_ref[:, hi] = ones
            elif s == 4:
                kwb_ref[:, g * HEAD_DIM:(g + 1) * HEAD_DIM] = yg.astype(BF16)
            elif s == 5:
                vwa_ref[:, lo] = yg.astype(BF16)
                vwa_ref[:, hi] = ones


def _kv_proj(x, g, w, g_sel, g_win):
    m, d = x.shape
    tm = min(m, 512)
    row = lambda i: (i, 0)
    aug = N_KV * AUG_W
    return pl.pallas_call(
        _kv_proj_kernel,
        grid=(m // tm,),
        in_specs=[pl.BlockSpec((tm, d), row), _resident((1, d), lambda i: (0, 0)),
                  _resident(w.shape, lambda i: (0, 0)),
                  _resident((1, HEAD_DIM), lambda i: (0, 0)), _resident((1, HEAD_DIM), lambda i: (0, 0))],
        out_specs=([pl.BlockSpec((tm * N_KV, HEAD_DIM), row)] * 6
                   + [pl.BlockSpec((tm, aug), row), pl.BlockSpec((tm, aug), row),
                      pl.BlockSpec((tm, KV_W), row), pl.BlockSpec((tm, aug), row)]),
        out_shape=([jax.ShapeDtypeStruct((m * N_KV, HEAD_DIM), F32)] * 6
                   + [jax.ShapeDtypeStruct((m, aug), BF16), jax.ShapeDtypeStruct((m, aug), BF16),
                      jax.ShapeDtypeStruct((m, KV_W), BF16), jax.ShapeDtypeStruct((m, aug), BF16)]),
        compiler_params=_cparams(("parallel",)),
        name="kv_proj",
    )(x, g.reshape(1, d), w, g_sel.reshape(1, HEAD_DIM), g_win.reshape(1, HEAD_DIM))


PIECE_ROWS = CMP_STRIDE * N_KV
PIECE_PITCH = PIECE_ROWS + SUBLANES


def _piece_rows(buf, r, n):
    flat = buf.reshape(buf.shape[0] * PIECE_PITCH, HEAD_DIM)
    return flat[pl.ds(r, n, stride=PIECE_PITCH), :]


def _compress_kernel(pt_ref, kc_hbm, vc_hbm, pe_ref, w1_ref, b1_ref, w2_ref, b2_ref, gk_ref,
                     kcmp_ref, vcmp_ref, kbuf, vbuf, sem, lhs, bias_scr, *, n_quarters, pages_per_step):
    s = pl.program_id(0)
    n_steps = pl.num_programs(0)
    pieces_per_page = kc_hbm.shape[1]
    n_piece = pages_per_step * pieces_per_page
    n_load = n_piece + SUBLANES
    rows_g = n_piece + 2 * SUBLANES
    hidden = w2_ref.shape[1]

    def copies(step, slot):
        b = step // n_quarters
        q = step % n_quarters
        out = []
        for hbm, buf, which in ((kc_hbm, kbuf, 0), (vc_hbm, vbuf, 1)):
            for p in range(pages_per_step):
                pg = pt_ref[b, q * pages_per_step + p]
                dst = buf.at[slot, pl.ds(p * pieces_per_page, pieces_per_page), pl.ds(0, PIECE_ROWS), :]
                out.append(pltpu.make_async_copy(hbm.at[pg], dst, sem.at[slot, which]))
            pg = pt_ref[b, q * pages_per_step + pages_per_step]
            out.append(pltpu.make_async_copy(hbm.at[pg, 0], buf.at[slot, n_piece, pl.ds(0, PIECE_ROWS), :],
                                             sem.at[slot, which]))
        return out

    slot = s % 2

    @pl.when(s == 0)
    def _():
        for c in copies(0, 0):
            c.start()
        kbuf[:, n_piece + 1:] = jnp.zeros((2, SUBLANES - 1) + kbuf.shape[2:], F32)
        vbuf[:, n_piece + 1:] = jnp.zeros((2, SUBLANES - 1) + vbuf.shape[2:], F32)
        lhs[...] = jnp.zeros(lhs.shape, BF16)
        for which in range(2):
            pw = jnp.dot(pe_ref[which], w1_ref[which], preferred_element_type=F32)
            bias_scr[which] = (jnp.sum(pw[:SUBLANES, :hidden], axis=0, keepdims=True)
                               + jnp.sum(pw[SUBLANES:, hidden:], axis=0, keepdims=True) + b1_ref[which])

    @pl.when(s + 1 < n_steps)
    def _():
        for c in copies(s + 1, 1 - slot):
            c.start()

    for c in copies(s, slot):
        c.wait()

    for which, buf, out_ref in ((0, kbuf, kcmp_ref), (1, vbuf, vcmp_ref)):
        for g in range(N_KV):
            for pos in range(CMP_STRIDE):
                x = _piece_rows(buf.at[slot], pos * N_KV + g, n_load)
                lhs[which, g * rows_g:g * rows_g + n_load, pos * HEAD_DIM:(pos + 1) * HEAD_DIM] = x.astype(BF16)
        hab = jnp.dot(lhs[which], w1_ref[which], preferred_element_type=F32)
        for g in range(N_KV):
            blk = hab[g * rows_g:(g + 1) * rows_g]
            nxt = pltpu.roll(blk[:, hidden:], rows_g - 1, axis=0)
            hid = blk[:n_piece, :hidden] + nxt[:n_piece] + bias_scr[which]
            hid = hid * _sigmoid(hid)
            y = jnp.dot(hid.astype(BF16), w2_ref[which], preferred_element_type=F32) + b2_ref[which]
            if which == 0:
                y = _rms_rows(y, gk_ref[...])
            out_ref[0, g] = y.astype(BF16)


def _compress(kc_pages, vc_pages, table, pe_rows, w1, b1, w2, b2, g_k):
    nb, n_pages = table.shape[0], table.shape[1] - 1
    ppp = kc_pages.shape[1]
    pps = min(CMP_PAGES_PER_STEP, n_pages)
    n_quarters = n_pages // pps
    n_piece = pps * ppp
    kdim = CMP_STRIDE * HEAD_DIM
    hidden = w2.shape[1]
    full = lambda shape: pl.BlockSpec(shape, lambda s, pt: (0,) * len(shape))
    out_spec = pl.BlockSpec((1, N_KV, n_piece, HEAD_DIM), lambda s, pt: (s // n_quarters, 0, s % n_quarters, 0))
    out_sd = jax.ShapeDtypeStruct((nb, N_KV, n_pages * ppp, HEAD_DIM), BF16)
    buf = pltpu.VMEM((2, n_piece + SUBLANES, PIECE_PITCH, HEAD_DIM), F32)
    lhs = pltpu.VMEM((2, N_KV * (n_piece + 2 * SUBLANES), kdim), BF16)
    grid_spec = pltpu.PrefetchScalarGridSpec(
        num_scalar_prefetch=1,
        grid=(nb * n_quarters,),
        in_specs=[pl.BlockSpec(memory_space=pl.ANY), pl.BlockSpec(memory_space=pl.ANY),
                  full(pe_rows.shape), full(w1.shape), full(b1.shape), full(w2.shape), full(b2.shape),
                  full((1, HEAD_DIM))],
        out_specs=[out_spec, out_spec],
        scratch_shapes=[buf, buf, pltpu.SemaphoreType.DMA((2, 2)), lhs, pltpu.VMEM((2, 1, hidden), F32)],
    )
    return pl.pallas_call(
        functools.partial(_compress_kernel, n_quarters=n_quarters, pages_per_step=pps),
        grid_spec=grid_spec,
        out_shape=[out_sd, out_sd],
        compiler_params=_cparams(("arbitrary",)),
        name="compress",
    )(table, kc_pages, vc_pages, pe_rows, w1, b1, w2, b2, g_k.reshape(1, HEAD_DIM))


def _rel_bucket(dist):
    n = jnp.maximum(dist, 0)
    max_exact = N_BUCKETS // 2
    nf = jnp.maximum(n, 1).astype(F32)
    large = max_exact + (jnp.log(nf / max_exact) / math.log(MAX_DISTANCE / max_exact)
                         * (N_BUCKETS - max_exact)).astype(jnp.int32)
    large = jnp.minimum(large, N_BUCKETS - 1)
    return jnp.where(n < max_exact, n, large) & (N_BUCKETS - 1)


def _bias_table(rel_bias, dist, masked, gain=1.0):
    dist = lax.optimization_barrier(dist)
    onehot = (_rel_bucket(dist)[None, :] == jnp.arange(N_BUCKETS)[:, None]).astype(F32)
    table = jnp.dot(rel_bias.astype(F32).T * gain, onehot, precision=lax.Precision.HIGHEST)
    return jnp.where(dist[None, :] >= 0, table, NEG) if masked else table


def _bias_bank_kernel(gs_ref, gw_ref, g2_ref, sb_ref, wb_ref, rr_ref):
    for g_ref, o_ref in ((gs_ref, sb_ref), (gw_ref, wb_ref)):
        wp = g_ref.shape[2]
        x = jnp.broadcast_to(g_ref[0], (TQ, wp))
        o_ref[0] = pltpu.roll(x, wp - (TQ - 1), 1, stride=1, stride_axis=0)[:, :wp - TQ]

    n_blocks = g2_ref.shape[1]
    for il in range(SUBLANES):
        left = CMP_STRIDE * (SUBLANES - 1 - il)
        y = pltpu.roll(g2_ref[0], (2 * TQ - left) % (2 * TQ), 1)
        rr_ref[0, pl.ds(il, n_blocks, stride=SUBLANES), :] = y[:, :TQ]


def _bias_banks(rel_bias, n_qt, n_row_blocks, bank_k, bank_w, shift):
    nh = rel_bias.shape[1]
    wk = WINDOW + TQ
    gs = _bias_table(rel_bias, bank_k + TQ - 1 - jnp.arange(bank_k + bank_w + TQ), True, LOG2E)
    dw = WINDOW + TQ - 1 - jnp.arange(WINDOW + wk + TQ)
    gw = _bias_table(rel_bias, jnp.where(dw < WINDOW, dw, -1), True, LOG2E)
    n_blocks = -(-(n_qt + n_row_blocks - 1) // SUBLANES) * SUBLANES
    d2 = (TQ * (n_qt - 1 - jnp.arange(n_blocks))[:, None] + jnp.arange(2 * TQ)[None, :]
          - (CMP_BLOCK - 1) - CMP_STRIDE * (SUBLANES - 1))
    g2 = _bias_table(rel_bias, d2.reshape(-1), True, LOG2E).reshape(nh, n_blocks, 2 * TQ)
    spec3 = lambda a: pl.BlockSpec((1,) + a.shape[1:], lambda h: (h, 0, 0))
    outs = [jax.ShapeDtypeStruct((nh, TQ, bank_k + bank_w), F32), jax.ShapeDtypeStruct((nh, TQ, WINDOW + wk), F32),
            jax.ShapeDtypeStruct((nh, n_blocks * SUBLANES, TQ), F32)]
    gs, gw, g2 = gs[:, None] - shift, gw[:, None] - shift, g2 - shift
    return pl.pallas_call(
        _bias_bank_kernel,
        grid=(nh,),
        in_specs=[spec3(gs), spec3(gw), spec3(g2)],
        out_specs=[spec3(o) for o in outs],
        out_shape=outs,
        compiler_params=_cparams(("parallel",)),
        name="bias_banks",
    )(gs, gw, g2)


def _query_rows_bias(rel_bias, t0, n_tok, n_keys):
    dist = (t0 + jnp.arange(n_tok)[:, None] - jnp.arange(n_keys)[None, :]).reshape(-1)
    return _bias_table(rel_bias, dist, False).reshape(-1, n_tok, n_keys)


def _first_constant_distance():
    ratio = MAX_DISTANCE / (N_BUCKETS // 2)
    return int(math.ceil((N_BUCKETS // 2) * ratio ** ((N_BUCKETS // 2 - 1) / (N_BUCKETS // 2)))) + 2


def _topk_rows(score, blk, k_top):
    n_blk = score.shape[0]
    for _ in range(k_top):
        m = jnp.max(score, axis=0, keepdims=True)
        idx = jnp.min(jnp.where(score == m, blk, n_blk), axis=0, keepdims=True)
        score = jnp.where(blk == idx, REMOVED, score)
    return score == REMOVED


def _attn_prompt_kernel(q_ref, z0_ref, z1_ref, z2_ref, gl_ref, gq_ref, kcmp_ref, vcmp_ref,
                        ksa_ref, vsa_ref, kw_ref, vwa_ref, rr_ref, sb_ref, wb_ref, bounded_ref, o_ref,
                        q_scr, psum_scr, ocmp_scr, owin_scr, m_scr, acc_scr, s0_scr, s1_scr, *, n_qtiles, bank_k):
    qi = pl.program_id(1)
    q0 = qi * TQ
    n_cmp_rows = kcmp_ref.shape[1]
    n_sb = n_cmp_rows * CMP_STRIDE // SEL_BLOCK
    hg = HEADS_PER_GROUP
    q_gain = HEAD_DIM ** -0.5 * LOG2E
    wk = WINDOW + TQ

    def per_tile_branches(t, bounded):
        tile = qi + t
        t0 = tile * TQ
        for h in range(hg):
            qh = _rms_rows(q_ref[t * TQ:(t + 1) * TQ, h * HEAD_DIM:(h + 1) * HEAD_DIM], gq_ref[...]) * q_gain
            q_scr[t, h * TQ:(h + 1) * TQ, 0:HEAD_DIM] = qh.astype(BF16)
        q_all = q_scr[t, :, 0:HEAD_DIM]

        s_t = lax.dot_general(kcmp_ref[0], q_all, (((1,), (1,)), ((), ())), preferred_element_type=F32)
        rr_start = pl.multiple_of((n_qtiles - 1 - tile) * SUBLANES, SUBLANES)
        psum = jnp.zeros((n_cmp_rows, TQ), F32)
        for h in range(hg):
            sh = s_t[:, h * TQ:(h + 1) * TQ] + rr_ref[h, pl.ds(rr_start, n_cmp_rows), :]
            if bounded:
                e = jnp.exp2(sh)
            else:
                e = jnp.exp2(sh - jnp.maximum(jnp.max(sh, axis=0, keepdims=True), 0.5 * NEG))
            den = jnp.sum(e, axis=0, keepdims=True)
            p = e * (1.0 / jnp.where(den > 0.0, den, 1.0))
            psum = psum + p
            ocmp_scr[t, h * TQ:(h + 1) * TQ, :] = lax.dot_general(
                p.astype(BF16), vcmp_ref[0], (((0,), (0,)), ((), ())), preferred_element_type=F32)

        w0 = pl.multiple_of(jnp.maximum(t0 - WINDOW, 0), LANES)
        s_w = lax.dot_general(q_all, kw_ref[pl.ds(w0, wk), :], (((1,), (1,)), ((), ())),
                              preferred_element_type=F32)
        vwin = vwa_ref[pl.ds(w0, wk), :]
        wb_start = pl.multiple_of(WINDOW - (t0 - w0), LANES)
        for h in range(hg):
            sh = s_w[h * TQ:(h + 1) * TQ] + wb_ref[h, :, pl.ds(wb_start, wk)]
            e = jnp.exp2(sh if bounded else sh - jnp.max(sh, axis=1, keepdims=True))
            o_aug = jnp.dot(e.astype(BF16), vwin, preferred_element_type=F32)
            owin_scr[t, h * TQ:(h + 1) * TQ, :] = o_aug[:, 0:HEAD_DIM] / o_aug[:, HEAD_DIM:AUG_W]

        psum_scr[t, 0:SUBLANES, :] = jnp.zeros((SUBLANES, TQ), F32)
        psum_scr[t, SUBLANES:, :] = psum
        ratio = SEL_BLOCK // CMP_STRIDE
        imp = psum_scr[t, pl.ds(SUBLANES - 1, n_sb, stride=ratio), :]
        for k in range(ratio):
            imp = imp + psum_scr[t, pl.ds(SUBLANES + k, n_sb, stride=ratio), :]
        blk = lax.broadcasted_iota(jnp.int32, (n_sb, 1), 0)
        cur = (t0 + lax.broadcasted_iota(jnp.int32, (1, TQ), 1)) // SEL_BLOCK
        blk_ok = blk <= cur
        forced = (blk == 0) | (blk == cur) | (blk == cur - 1)
        score = jnp.where(blk_ok, jnp.where(forced, FORCE, imp), -FORCE)
        sel_t = _topk_rows(score, blk, min(N_SELECT, n_sb)) & blk_ok
        not_sel = jnp.where(sel_t, 0.0, 1.0).T
        if n_sb < HEAD_DIM:
            not_sel = jnp.concatenate([not_sel, jnp.zeros((TQ, HEAD_DIM - n_sb), F32)], axis=1)
        for h in range(hg):
            q_scr[t, h * TQ:(h + 1) * TQ, HEAD_DIM:AUG_W] = not_sel.astype(BF16)

    half = 0
    n_kv = (q0 + TQ + TK_SEL - 1) // TK_SEL
    rep = TK_SEL // LANES

    def scores(j, s_ref):
        k0 = pl.multiple_of(j * TK_SEL, TK_SEL)
        s_ref[...] = lax.dot_general(q_scr[half], ksa_ref[pl.ds(k0, TK_SEL), :], (((1,), (1,)), ((), ())),
                                     preferred_element_type=F32)

    def kv_tile(j, s_ref, bounded):
        k0 = pl.multiple_of(j * TK_SEL, TK_SEL)
        vt = vsa_ref[pl.ds(k0, TK_SEL), :]
        b_start = pl.multiple_of(jnp.maximum(bank_k - (q0 - k0), 0), LANES)
        for h in range(hg):
            rows = slice(h * TQ, (h + 1) * TQ)
            sh = s_ref[rows] + sb_ref[h, :, pl.ds(b_start, TK_SEL)]
            if bounded:
                acc_scr[rows] += jnp.dot(jnp.exp2(sh).astype(BF16), vt, preferred_element_type=F32)
            else:
                m_prev = m_scr[rows]
                m_next = jnp.maximum(m_prev, jnp.max(sh, axis=1, keepdims=True))
                alpha = jnp.exp2(m_prev - m_next)
                e = jnp.exp2(sh - jnp.tile(m_next, (1, rep)))
                acc_scr[rows] = (jnp.tile(alpha, (1, AUG_W // LANES)) * acc_scr[rows]
                                 + jnp.dot(e.astype(BF16), vt, preferred_element_type=F32))
                m_scr[rows] = m_next

    def selected_branch(bounded):
        m_scr[...] = jnp.full(m_scr.shape, NEG, F32)
        acc_scr[...] = jnp.zeros(acc_scr.shape, F32)

        if not bounded:
            def one_tile(j, carry):
                scores(j, s0_scr)
                kv_tile(j, s0_scr, False)
                return carry

            lax.fori_loop(0, n_kv, one_tile, 0)
            return

        def run(first, count):
            bufs = (s0_scr, s1_scr)
            for k in range(count):
                scores(first + k + 1, bufs[(k + 1) % 2])
                kv_tile(first + k, bufs[k % 2], bounded)

        def kv_run(i, carry):
            run(KV_RUN * i, KV_RUN)
            return carry

        scores(0, s0_scr)
        n_before = n_kv - 1
        lax.fori_loop(0, n_before // KV_RUN, kv_run, 0)
        size = KV_RUN // 2
        while size >= 2:
            start = n_before // (2 * size) * (2 * size)

            @pl.when(n_before - start >= size)
            def _(start=start, size=size):
                run(start, size)

            size //= 2

        done2 = n_before // 2 * 2

        @pl.when(n_before > done2)
        def _():
            scores(done2 + 1, s1_scr)
            kv_tile(done2, s0_scr, bounded)
            kv_tile(done2 + 1, s1_scr, bounded)

        @pl.when(n_before == done2)
        def _():
            kv_tile(done2, s0_scr, bounded)

    for bounded in (True, False):
        @pl.when(bounded_ref[0] == int(bounded))
        def _(bounded=bounded):
            per_tile_branches(0, bounded)
            selected_branch(bounded)

    sig_gl = _sigmoid(gl_ref[...])
    z_refs = (z0_ref, z1_ref, z2_ref)
    for h in range(hg):
        rows = slice(h * TQ, (h + 1) * TQ)
        cols = slice(h * HEAD_DIM, (h + 1) * HEAD_DIM)
        o_sel = acc_scr[rows, 0:HEAD_DIM] / acc_scr[rows, HEAD_DIM:AUG_W]
        out = jnp.zeros((TQ, HEAD_DIM), F32)
        for br, o_br in enumerate((ocmp_scr[half, rows, :], o_sel, owin_scr[half, rows, :])):
            z = z_refs[br][:, cols]
            c = br * hg + h
            out = out + sig_gl[:, c:c + 1] * o_br * (z * _sigmoid(z))
        o_ref[:, cols] = out.astype(BF16)


def _attn_prompt(proj, gl_g, g_q, kcmp, vcmp, ks_aug, vs_aug, kw_b, vw_aug, rr, sb, wb, bounded, bank_k):
    s_len = proj.shape[0]
    n_qt = s_len // TQ
    n_cmp_rows = kcmp.shape[2]
    gcols = N_HEADS * HEAD_DIM // GROUP_W
    qz = lambda seg: pl.BlockSpec((TQ, GROUP_W), lambda g, i, seg=seg: (i, seg * gcols + g))
    per_group = lambda shape: _resident((1,) + shape, lambda g, i: (g,) + (0,) * len(shape))
    kv_spec = lambda width: _resident((s_len, width), lambda g, i: (0, g))
    return pl.pallas_call(
        functools.partial(_attn_prompt_kernel, n_qtiles=n_qt, bank_k=bank_k),
        grid=(N_KV, n_qt),
        in_specs=[qz(0), qz(1), qz(2), qz(3),
                  pl.BlockSpec((TQ, LANES), lambda g, i: (i, g)),
                  _resident((1, HEAD_DIM), lambda g, i: (0, 0)),
                  per_group((n_cmp_rows, HEAD_DIM)), per_group((n_cmp_rows, HEAD_DIM)),
                  kv_spec(AUG_W), kv_spec(AUG_W), kv_spec(HEAD_DIM), kv_spec(AUG_W),
                  _resident((HEADS_PER_GROUP,) + rr.shape[1:], lambda g, i: (g, 0, 0)),
                  _resident((HEADS_PER_GROUP,) + sb.shape[1:], lambda g, i: (g, 0, 0)),
                  _resident((HEADS_PER_GROUP,) + wb.shape[1:], lambda g, i: (g, 0, 0)),
                  pl.BlockSpec(memory_space=pltpu.SMEM)],
        out_specs=pl.BlockSpec((TQ, GROUP_W), lambda g, i: (i, g)),
        out_shape=jax.ShapeDtypeStruct((s_len, N_HEADS * HEAD_DIM), BF16),
        scratch_shapes=[pltpu.VMEM((1, HEADS_PER_GROUP * TQ, AUG_W), BF16),
                        pltpu.VMEM((1, n_cmp_rows + SUBLANES, TQ), F32),
                        pltpu.VMEM((1, HEADS_PER_GROUP * TQ, HEAD_DIM), F32),
                        pltpu.VMEM((1, HEADS_PER_GROUP * TQ, HEAD_DIM), F32),
                        pltpu.VMEM((HEADS_PER_GROUP * TQ, LANES), F32),
                        pltpu.VMEM((HEADS_PER_GROUP * TQ, AUG_W), F32),
                        pltpu.VMEM((HEADS_PER_GROUP * TQ, TK_SEL), F32),
                        pltpu.VMEM((HEADS_PER_GROUP * TQ, TK_SEL), F32)],
        compiler_params=_cparams(("arbitrary", "arbitrary")),
        name="attn_prompt",
    )(proj, proj, proj, proj, gl_g, g_q.reshape(1, HEAD_DIM), kcmp[0], vcmp[0],
      ks_aug, vs_aug, kw_b, vw_aug, rr, sb, wb, bounded)


def _attn_sample_kernel(*refs, n_tok, pages_per_chunk, page, past_len):
    pt_ref = refs[0]
    q_ref, z_ref, gl_ref, gq_ref, kcmp_ref, vcmp_ref = refs[1:7]
    kpages = refs[7:7 + pages_per_chunk]
    vpages = refs[7 + pages_per_chunk:7 + 2 * pages_per_chunk]
    (ksn_ref, vsn_ref, kwn_ref, vwn_ref, kwc_ref, vwc_ref, bcmp_ref, bsel_ref, bwin_ref, ov_ref, ex_ref,
     o_ref, sel_scr, m_scr, l_scr, acc_scr, q_scr, ocmp_scr) = refs[7 + 2 * pages_per_chunk:]
    del pt_ref
    c = pl.program_id(1)
    n_chunks = pl.num_programs(1)
    hg = HEADS_PER_GROUP
    nr = hg * n_tok
    scale = HEAD_DIM ** -0.5
    n_cmp_rows = kcmp_ref.shape[2]
    n_sb = ov_ref.shape[0]
    chunk = pages_per_chunk * page
    tok = lax.broadcasted_iota(jnp.int32, (N_KV * nr, 1), 0) % n_tok

    @pl.when(c == 0)
    def _():
        q_scr[...] = _rms_rows(q_ref[0], gq_ref[...]).astype(BF16)
        m_scr[...] = jnp.full(m_scr.shape, NEG, F32)
        l_scr[...] = jnp.zeros(l_scr.shape, F32)
        acc_scr[...] = jnp.zeros(acc_scr.shape, F32)
        ci = lax.broadcasted_iota(jnp.int32, (1, n_cmp_rows), 1)
        cmp_valid = ci < n_cmp_rows - 1
        s = jnp.concatenate(
            [lax.dot_general(q_scr[g * nr:(g + 1) * nr], kcmp_ref[0, g], (((1,), (1,)), ((), ())),
                             preferred_element_type=F32) for g in range(N_KV)], axis=0)
        s = jnp.where(cmp_valid, s * scale + bcmp_ref[...], NEG)
        e = jnp.where(cmp_valid, jnp.exp(s - jnp.max(s, axis=1, keepdims=True)), 0.0)
        p = e / jnp.sum(e, axis=1, keepdims=True)
        pb = p.astype(BF16)
        psums = []
        for g in range(N_KV):
            rows = slice(g * nr, (g + 1) * nr)
            ocmp_scr[rows] = jnp.dot(pb[rows], vcmp_ref[0, g], preferred_element_type=F32)
            psum = p[g * nr:g * nr + n_tok]
            for h in range(1, hg):
                psum = psum + p[g * nr + h * n_tok:g * nr + (h + 1) * n_tok]
            psums.append(psum)
        rest = jnp.concatenate(psums, axis=0)
        imp_t = jnp.zeros((n_sb, N_KV * n_tok), F32)
        for _ in range(3):
            piece = rest.astype(BF16)
            rest = rest - piece.astype(F32)
            imp_t = imp_t + lax.dot_general(ov_ref[...], piece, (((1,), (1,)), ((), ())),
                                            preferred_element_type=F32)
        blk = lax.broadcasted_iota(jnp.int32, (n_sb, 1), 0)
        forced = (blk == 0) | (blk == n_sb - 1)
        score = jnp.where(forced, FORCE, imp_t)
        sel_t = _topk_rows(score, blk, min(N_SELECT, n_sb + 1) - 1)
        sel_f = jnp.where(sel_t, 1.0, 0.0)
        if N_KV * n_tok < LANES:
            sel_f = jnp.concatenate([sel_f, jnp.zeros((n_sb, LANES - N_KV * n_tok), F32)], axis=1)
        sel_f = sel_f.T
        for g in range(N_KV):
            sel_scr[g * nr:(g + 1) * nr] = jnp.tile(sel_f[g * n_tok:(g + 1) * n_tok], (hg, 1)).astype(BF16)

    def group_rows(ref, g):
        return ref[0, pl.ds(g, ref.shape[1] // N_KV, stride=N_KV), :].astype(BF16)

    def all_scores(k_of_group):
        return jnp.concatenate(
            [lax.dot_general(q_scr[g * nr:(g + 1) * nr], k_of_group(g), (((1,), (1,)), ((), ())),
                             preferred_element_type=F32) for g in range(N_KV)], axis=0)

    def all_values(e, v_of_group):
        eb = e.astype(BF16)
        return jnp.concatenate(
            [jnp.dot(eb[g * nr:(g + 1) * nr], v_of_group(g), preferred_element_type=F32) for g in range(N_KV)],
            axis=0)

    def online_update(sh, v_of_group):
        m_prev = m_scr[...]
        m_next = jnp.maximum(m_prev, jnp.max(sh, axis=1, keepdims=True))
        alpha = jnp.exp(m_prev - m_next)
        e = jnp.exp(sh - m_next[:, 0:1])
        l_scr[...] = alpha * l_scr[...] + jnp.sum(e, axis=1, keepdims=True)
        acc_scr[...] = alpha * acc_scr[...] + all_values(e, v_of_group)
        m_scr[...] = m_next

    k0 = pl.multiple_of(c * chunk, chunk)
    s = all_scores(lambda g: jnp.concatenate([group_rows(r, g) for r in kpages], axis=0))
    picked = jnp.dot(sel_scr[...], ex_ref[:, pl.ds(k0, chunk)], preferred_element_type=F32)
    sh = jnp.where(picked > 0.5, s * scale + bsel_ref[:, pl.ds(k0, chunk)], NEG)
    online_update(sh, lambda g: jnp.concatenate([group_rows(r, g) for r in vpages], axis=0))

    @pl.when(c == n_chunks - 1)
    def _():
        new_w = ksn_ref.shape[1] // N_KV
        jn = lax.broadcasted_iota(jnp.int32, (1, new_w), 1)
        new_valid = jn <= tok
        wbuf = kwc_ref.shape[1] // N_KV
        jw = lax.broadcasted_iota(jnp.int32, (1, wbuf), 1)
        dist_c = (past_len + tok) - (past_len - wbuf + jw)
        win_valid_c = dist_c < WINDOW
        s = all_scores(lambda g: group_rows(ksn_ref, g))
        sh = jnp.where(new_valid, s * scale + bsel_ref[:, pl.ds(past_len, new_w)], NEG)
        online_update(sh, lambda g: group_rows(vsn_ref, g))
        o_sel = acc_scr[...] / l_scr[...]
        s_c = all_scores(lambda g: group_rows(kwc_ref, g))
        s_n = all_scores(lambda g: group_rows(kwn_ref, g))
        s_c = jnp.where(win_valid_c, s_c * scale + bwin_ref[:, 0:wbuf], NEG)
        s_n = jnp.where(new_valid, s_n * scale + bwin_ref[:, wbuf:wbuf + new_w], NEG)
        mx = jnp.maximum(jnp.max(s_c, axis=1, keepdims=True), jnp.max(s_n, axis=1, keepdims=True))
        e_c = jnp.exp(s_c - mx)
        e_n = jnp.exp(s_n - mx)
        den = jnp.sum(e_c, axis=1, keepdims=True) + jnp.sum(e_n, axis=1, keepdims=True)
        o_win = (all_values(e_c, lambda g: group_rows(vwc_ref, g))
                 + all_values(e_n, lambda g: group_rows(vwn_ref, g))) / den
        sig_gl = _sigmoid(gl_ref[0])
        out = jnp.zeros((N_KV * nr, HEAD_DIM), F32)
        for br, o_br in enumerate((ocmp_scr[...], o_sel, o_win)):
            z = z_ref[0, br]
            out = out + sig_gl[br] * o_br * (z * _sigmoid(z))
        o_ref[0] = out.astype(BF16)


def _attn_sample(q_s, z_s, gl_s, g_q, kcmp, vcmp, ksel_pages, vsel_pages, page_table, ks_new, vs_new,
                 kw_new, vw_new, kw_cache, vw_cache, bcmp, bsel, bwin, overlap, expand, n_tok):
    nb = q_s.shape[0]
    n_pages = page_table.shape[1]
    page = ksel_pages.shape[1] // N_KV
    ppc = min(SAMPLE_PAGES_PER_STEP, n_pages)
    n_chunks = n_pages // ppc
    nrow = q_s.shape[1]
    per_b = lambda shape: pl.BlockSpec((1,) + shape, lambda b, c, pt: (b,) + (0,) * len(shape))
    full = lambda shape: _resident(shape, lambda b, c, pt: (0,) * len(shape))
    page_spec = lambda i: pl.BlockSpec((1, page * N_KV, HEAD_DIM),
                                       lambda b, c, pt, i=i: (pt[b, c * ppc + i], 0, 0))
    in_specs = ([per_b(q_s.shape[1:]), per_b(z_s.shape[1:]), per_b(gl_s.shape[1:]), full((1, HEAD_DIM)),
                 per_b(kcmp.shape[1:]), per_b(vcmp.shape[1:])]
                + [page_spec(i) for i in range(ppc)] + [page_spec(i) for i in range(ppc)]
                + [per_b(ks_new.shape[1:])] * 4 + [per_b(kw_cache.shape[1:])] * 2
                + [full(bcmp.shape), full(bsel.shape), full(bwin.shape), full(overlap.shape), full(expand.shape)])
    grid_spec = pltpu.PrefetchScalarGridSpec(
        num_scalar_prefetch=1,
        grid=(nb, n_chunks),
        in_specs=in_specs,
        out_specs=pl.BlockSpec((1, nrow, HEAD_DIM), lambda b, c, pt: (b, 0, 0)),
        scratch_shapes=[pltpu.VMEM((nrow, expand.shape[0]), BF16),
                        pltpu.VMEM((nrow, LANES), F32), pltpu.VMEM((nrow, LANES), F32),
                        pltpu.VMEM((nrow, HEAD_DIM), F32), pltpu.VMEM((nrow, HEAD_DIM), BF16),
                        pltpu.VMEM((nrow, HEAD_DIM), F32)],
    )
    return pl.pallas_call(
        functools.partial(_attn_sample_kernel, n_tok=n_tok, pages_per_chunk=ppc, page=page,
                          past_len=n_pages * page),
        grid_spec=grid_spec,
        out_shape=jax.ShapeDtypeStruct((nb, nrow, HEAD_DIM), BF16),
        compiler_params=_cparams(("arbitrary", "arbitrary")),
        name="attn_sample",
    )(page_table, q_s, z_s, gl_s, g_q.reshape(1, HEAD_DIM), kcmp, vcmp,
      *([ksel_pages] * ppc), *([vsel_pages] * ppc), ks_new, vs_new, kw_new, vw_new, kw_cache, vw_cache,
      bcmp, bsel, bwin, overlap, expand)


def kernel(x_prompt, x_sample, p_prompt, p_sample, state_conv, state_h, cache_k_cmp, cache_v_cmp, cache_k_sel, cache_v_sel, cache_k_win, cache_v_win, page_table, norm_g, a_w_in, a_conv_w, a_conv_b, a_w_r, a_b_r, a_w_i, a_b_i, a_lam, a_w_out, kv_norm_g, w_kv, g_k_cmp, g_k_sel, g_k_win, cmp_pos, cmp_w1a, cmp_w1b, cmp_b1, cmp_w2, cmp_b2, b_w_in, b_g_q, b_w_out, rel_bias, ple_w_proj, ple_norm_g, ple_w_gate):
    bp, s_len, d = x_prompt.shape
    db, ds, _ = x_sample.shape
    n_pool, page = cache_k_sel.shape[:2]
    n_pages = page_table.shape[1]
    past_len = n_pages * page
    wbuf = cache_k_win.shape[1]
    hd = N_HEADS * HEAD_DIM
    assert bp == 1 and norm_g.shape[0] == 2 and a_w_in.shape[0] == 1 and b_w_in.shape[0] == 1
    assert d == hd and s_len % TK_SEL == 0 and s_len >= WINDOW + TQ and page % SEL_BLOCK == 0
    assert wbuf == WINDOW and ds <= SEL_BLOCK and past_len % TK_SEL == 0 and s_len // SEL_BLOCK <= HEAD_DIM

    w_in0 = a_w_in[0].astype(BF16)
    w_ri = jnp.concatenate([a_w_r[0], a_w_i[0]], axis=-1).astype(BF16)
    w_out0 = a_w_out[0].astype(BF16)
    w_gate = ple_w_gate.astype(BF16)
    w_proj = ple_w_proj.astype(BF16)
    w_kv_b = w_kv.astype(BF16)
    w_in1 = b_w_in[0].astype(BF16)
    w_gl = b_w_in[0][:, 4 * hd:].reshape(d, 3, N_KV, HEADS_PER_GROUP).transpose(0, 2, 1, 3)
    w_gl = jnp.pad(w_gl.reshape(d, N_KV, 3 * HEADS_PER_GROUP), ((0, 0), (0, 0), (0, LANES - 3 * HEADS_PER_GROUP)))
    w_gl = w_gl.reshape(d, N_KV * LANES).astype(BF16)
    w_out1 = b_w_out[0].astype(BF16)
    w1 = jnp.concatenate([cmp_w1a, cmp_w1b], axis=-1).astype(BF16)
    w2 = cmp_w2.astype(BF16)
    pe = cmp_pos.reshape(2, 2, 1, CMP_STRIDE * HEAD_DIM)
    pe_hi = pe.astype(BF16)
    pe_lo = (pe - pe_hi.astype(F32)).astype(BF16)
    pe_rows = jnp.concatenate([pe_hi, pe_lo, jnp.zeros((2, 2, SUBLANES - 2, pe.shape[-1]), BF16)], axis=2)
    pe_rows = pe_rows.reshape(2, 2 * SUBLANES, pe.shape[-1])
    b1 = cmp_b1.reshape(2, 1, -1)
    b2 = cmp_b2.reshape(2, 1, HEAD_DIM)

    xp = x_prompt.reshape(s_len, d)
    xs = x_sample.reshape(db * ds, d)
    uz_p = _norm_matmul(xp, norm_g[0], w_in0, 1024)
    uz_s = _norm_matmul(xs, norm_g[0], w_in0, 1024)
    rg = (a_conv_w[0], a_conv_b[0], w_ri, a_b_r[0], a_b_i[0], a_lam[0])
    gated_p, h_p = _rglru_prompt(uz_p, jnp.zeros((SUBLANES, d), F32), jnp.zeros((1, d), F32), *rg)
    conv_p = uz_p[s_len - (CONV_W - 1):, :d].reshape(1, bp, CONV_W - 1, d)
    u_s = uz_s[:, :d].reshape(db, ds, d).transpose(1, 0, 2)
    z_s = uz_s[:, d:].reshape(db, ds, d).transpose(1, 0, 2)
    up_s = jnp.concatenate([state_conv[0].transpose(1, 0, 2), u_s], axis=0)
    gated_s, h_s = _rglru_sample(up_s, z_s, state_h[0], *rg)
    conv_s = up_s[ds:].transpose(1, 0, 2)[None]
    gated_s = gated_s.transpose(1, 0, 2).reshape(db * ds, d)
    ple0 = (w_proj[0], ple_norm_g[0], w_gate[0])
    h1_p = _outproj_ple(gated_p, xp, p_prompt[0].reshape(s_len, -1), w_out0, *ple0)
    h1_s = _outproj_ple(gated_s, xs, p_sample[0].reshape(db * ds, -1), w_out0, *ple0)

    kv_p = _kv_proj(h1_p, kv_norm_g, w_kv_b, g_k_sel, g_k_win)
    kv_s = _kv_proj(h1_s, kv_norm_g, w_kv_b, g_k_sel, g_k_win)
    kc_p, vc_p, ks_p, vs_p, kw_p, vw_p = kv_p[:6]
    kc_s, vc_s, ks_s, vs_s, kw_s, vw_s = kv_s[:6]

    p_pages = s_len // page
    ident = jnp.minimum(jnp.arange(p_pages + 1, dtype=jnp.int32), p_pages - 1)[None]
    cmp_w = (pe_rows, w1, b1, w2, b2, g_k_cmp)
    pieces = lambda x: x.reshape(-1, page // CMP_STRIDE, PIECE_ROWS, HEAD_DIM)
    kcmp_p, vcmp_p = _compress(pieces(kc_p), pieces(vc_p), ident, *cmp_w)
    table_ext = jnp.concatenate([page_table, page_table[:, -1:]], axis=1)
    kcmp_s, vcmp_s = _compress(pieces(cache_k_cmp), pieces(cache_v_cmp), table_ext, *cmp_w)

    proj_p, gl_p = _norm_matmul(h1_p, norm_g[1], w_in1, 1024, w_gl, n=4 * hd)
    proj_s, gl_s = _norm_matmul(h1_s, norm_g[1], w_in1, 1024, w_gl, n=4 * hd)

    n_qt = s_len // TQ
    n_cmp_rows = s_len // CMP_STRIDE
    bank_w = max(TK_SEL, WINDOW + TQ)
    bank_k = min(-(-(_first_constant_distance() + bank_w) // LANES) * LANES, s_len)
    gmax = lambda g: jnp.max(jnp.abs(g))
    k_gain = jnp.maximum(jnp.maximum(gmax(g_k_sel), gmax(g_k_win)), gmax(g_k_cmp))
    score_bound = (1.02 * HEAD_DIM * HEAD_DIM ** -0.5 * gmax(b_g_q[0]) * k_gain + gmax(rel_bias)) * LOG2E
    bounded = score_bound <= MAX_SCORE_BOUND
    shift = jnp.where(bounded, score_bound, 0.0)
    sb, wb, rr = _bias_banks(rel_bias, n_qt, n_cmp_rows // SUBLANES, bank_k, bank_w, shift)
    o_p = _attn_prompt(proj_p, gl_p, b_g_q[0], kcmp_p, vcmp_p, kv_p[6], kv_p[7], kv_p[8], kv_p[9],
                       rr, sb, wb, bounded.astype(jnp.int32).reshape(1), bank_k)

    def head_rows(x):
        return x.reshape(db, ds, N_HEADS, HEAD_DIM).transpose(0, 2, 1, 3).reshape(db, N_HEADS * ds, HEAD_DIM)

    q_s = head_rows(proj_s[:, :hd])
    zz_s = jnp.stack([head_rows(proj_s[:, (1 + br) * hd:(2 + br) * hd]) for br in range(3)], axis=1)
    gl_s = gl_s.reshape(db, ds, N_KV, LANES)[..., :3 * HEADS_PER_GROUP].reshape(db, ds, N_KV, 3, HEADS_PER_GROUP)
    gl_s = gl_s.transpose(0, 3, 2, 4, 1).reshape(db, 3, N_HEADS * ds, 1)
    new_w = LANES
    group_rows = lambda x: x.reshape(db, -1, HEAD_DIM)
    pad_new = lambda x: jnp.pad(group_rows(x), ((0, 0), (0, (new_w - ds) * N_KV), (0, 0)))
    n_cmp_s = past_len // CMP_STRIDE
    bsel = _query_rows_bias(rel_bias, past_len, ds, past_len + new_w).reshape(N_HEADS * ds, -1)
    bwin = bsel[:, past_len - wbuf:]
    bcmp = bsel[:, CMP_BLOCK - 1::CMP_STRIDE][:, :n_cmp_s]
    n_sb_s = past_len // SEL_BLOCK
    ci = jnp.arange(n_cmp_s)[:, None] * CMP_STRIDE
    sj = jnp.arange(n_sb_s)[None, :] * SEL_BLOCK
    overlap = ((ci < sj + SEL_BLOCK) & (ci + CMP_BLOCK > sj) & (ci < past_len - CMP_STRIDE)).T.astype(BF16)
    expand_s = (jnp.arange(past_len)[None, :] // SEL_BLOCK == jnp.arange(n_sb_s)[:, None]).astype(BF16)
    o_s = _attn_sample(q_s, zz_s, gl_s, b_g_q[0], kcmp_s, vcmp_s,
                       cache_k_sel.reshape(n_pool, -1, HEAD_DIM), cache_v_sel.reshape(n_pool, -1, HEAD_DIM),
                       page_table, pad_new(ks_s), pad_new(vs_s), pad_new(kw_s), pad_new(vw_s),
                       group_rows(cache_k_win), group_rows(cache_v_win),
                       bcmp, bsel, bwin, overlap, expand_s, ds)
    o_s = o_s.reshape(db, N_HEADS, ds, HEAD_DIM).transpose(0, 2, 1, 3).reshape(db * ds, hd)

    ple1 = (w_proj[1], ple_norm_g[1], w_gate[1])
    y_p = _outproj_ple(o_p, h1_p, p_prompt[1].reshape(s_len, -1), w_out1, *ple1)
    y_s = _outproj_ple(o_s, h1_s, p_sample[1].reshape(db * ds, -1), w_out1, *ple1)

    rows4 = lambda x, b, t: x.reshape(b, t, N_KV, HEAD_DIM)
    wk_p = min(WINDOW, s_len)
    kwin_s = jnp.concatenate([cache_k_win, rows4(kw_s, db, ds)], axis=1)[:, -WINDOW:]
    vwin_s = jnp.concatenate([cache_v_win, rows4(vw_s, db, ds)], axis=1)[:, -WINDOW:]
    return (y_p.reshape(bp, s_len, d), y_s.reshape(db, ds, d),
            conv_p, h_p.reshape(1, bp, d),
            rows4(kc_p, bp, s_len), rows4(vc_p, bp, s_len), rows4(ks_p, bp, s_len), rows4(vs_p, bp, s_len),
            rows4(kw_p, bp, s_len)[:, -wk_p:], rows4(vw_p, bp, s_len)[:, -wk_p:],
            conv_s, h_s.reshape(1, db, d),
            rows4(kc_s, db, ds), rows4(vc_s, db, ds), rows4(ks_s, db, ds), rows4(vs_s, db, ds),
            kwin_s, vwin_s)
```

```python
import functools
import math

import jax
import jax.numpy as jnp
from jax import lax
from jax.experimental import pallas as pl
from jax.experimental.pallas import tpu as pltpu

F32 = jnp.float32
BF16 = jnp.bfloat16

N_RNN_BLOCKS = 8
CONV_W = 4
LRU_C = 8.0
N_HEADS = 16
HEAD_DIM = 128
N_KV = 4
HEADS_PER_GROUP = N_HEADS // N_KV
GROUP_W = HEADS_PER_GROUP * HEAD_DIM
KV_W = N_KV * HEAD_DIM
CMP_BLOCK = 32
CMP_STRIDE = 16
SEL_BLOCK = 64
N_SELECT = 16
WINDOW = 512
N_BUCKETS = 32
MAX_DISTANCE = 4096
EPS = 1e-6
NEG = -1e30
FORCE = 1e9
REMOVED = -2.0 ** 126
LOG2E = math.log2(math.e)
MAX_SCORE_BOUND = 60.0

LANES = 128
SUBLANES = 8
VMEM_LIMIT = 56 * 1024 * 1024

TQ = 128
TK_SEL = 512
KV_RUN = 8
CMP_PAGES_PER_STEP = 16
SAMPLE_PAGES_PER_STEP = 32


def _cparams(sem):
    return pltpu.CompilerParams(dimension_semantics=sem, vmem_limit_bytes=VMEM_LIMIT)


def _resident(shape, index_map):
    return pl.BlockSpec(shape, index_map, pipeline_mode=pl.Buffered(1))


def _rms_rows(x, g):
    return x * lax.rsqrt(jnp.mean(x * x, axis=-1, keepdims=True) + EPS) * g


def _sigmoid(x):
    return 1.0 / (1.0 + jnp.exp(-x))


def _norm_matmul_kernel(x_ref, g_ref, w_ref, *rest):
    if len(rest) == 4:
        w2_ref, o_ref, o2_ref, xn_ref = rest
    else:
        (o_ref, xn_ref), w2_ref, o2_ref = rest, None, None

    @pl.when(pl.program_id(1) == 0)
    def _():
        xn_ref[...] = _rms_rows(x_ref[...], g_ref[...]).astype(BF16)
        if w2_ref is not None:
            o2_ref[...] = jnp.dot(xn_ref[...], w2_ref[...], preferred_element_type=F32)

    o_ref[...] = jnp.dot(xn_ref[...], w_ref[...], preferred_element_type=F32)


def _norm_matmul(x, g, w, tn, w2=None, n=None):
    m, d = x.shape
    n = w.shape[1] if n is None else n
    tm = min(m, 1024)
    in_specs = [pl.BlockSpec((tm, d), lambda i, j: (i, 0)),
                pl.BlockSpec((1, d), lambda i, j: (0, 0)),
                pl.BlockSpec((d, tn), lambda i, j: (0, j))]
    out_specs = [pl.BlockSpec((tm, tn), lambda i, j: (i, j))]
    out_shape = [jax.ShapeDtypeStruct((m, n), F32)]
    args = [x, g.reshape(1, d), w]
    if w2 is not None:
        in_specs.append(_resident(w2.shape, lambda i, j: (0, 0)))
        out_specs.append(pl.BlockSpec((tm, w2.shape[1]), lambda i, j: (i, 0)))
        out_shape.append(jax.ShapeDtypeStruct((m, w2.shape[1]), F32))
        args.append(w2)
    outs = pl.pallas_call(
        _norm_matmul_kernel,
        grid=(m // tm, n // tn),
        in_specs=in_specs,
        out_specs=out_specs,
        out_shape=out_shape,
        scratch_shapes=[pltpu.VMEM((tm, d), BF16)],
        compiler_params=_cparams(("parallel", "arbitrary")),
        name="norm_matmul",
    )(*args)
    return outs if w2 is not None else outs[0]


def _lru_gates(xc, wri_ref, br_ref, bi_ref, sp_ref, n, bw):
    sl = slice(n * bw, (n + 1) * bw)
    ri = jnp.dot(xc.astype(BF16), wri_ref[n], preferred_element_type=F32)
    r = _sigmoid(ri[:, :bw] + br_ref[:, sl])
    i = _sigmoid(ri[:, bw:] + bi_ref[:, sl])
    log_a = (-LRU_C) * r * sp_ref[:, sl]
    a = jnp.exp(log_a)
    b = jnp.sqrt(1.0 - a * a) * (i * xc)
    return a, b


def _softplus(x):
    return jnp.maximum(x, 0.0) + jnp.log(1.0 + jnp.exp(-jnp.abs(x)))


def _rglru_prompt_kernel(u_ref, z_ref, cinit_ref, hinit_ref, cw_ref, cb_ref, wri_ref, br_ref, bi_ref,
                         lam_ref, g_ref, hlast_ref, ubuf, a_scr, b_scr, hs_scr, hcar, sp_scr):
    tm, d = u_ref.shape
    bw = d // N_RNN_BLOCKS

    @pl.when(pl.program_id(0) == 0)
    def _():
        ubuf[0:SUBLANES, :] = cinit_ref[...]
        hcar[...] = hinit_ref[...]
        sp_scr[...] = _softplus(-lam_ref[...])

    ubuf[SUBLANES:, :] = u_ref[...]
    for n in range(N_RNN_BLOCKS):
        sl = slice(n * bw, (n + 1) * bw)
        up = ubuf[:, sl]
        xc = cb_ref[:, sl] + up * cw_ref[CONV_W - 1:CONV_W, sl]
        for k in range(CONV_W - 1):
            xc = xc + pltpu.roll(up, CONV_W - 1 - k, axis=0) * cw_ref[k:k + 1, sl]
        xc = xc[SUBLANES:]
        a, b = _lru_gates(xc, wri_ref, br_ref, bi_ref, sp_scr, n, bw)
        a_scr[:, sl] = a
        b_scr[:, sl] = b
    ubuf[0:SUBLANES, :] = u_ref[tm - SUBLANES:tm, :]

    def step(t, h):
        h = a_scr[pl.ds(t, 1), :] * h + b_scr[pl.ds(t, 1), :]
        hs_scr[pl.ds(t, 1), :] = h
        return h

    h = lax.fori_loop(0, tm, step, hcar[...], unroll=8)
    hcar[...] = h
    hlast_ref[...] = h
    z = z_ref[...]
    g_ref[...] = (hs_scr[...] * (z * _sigmoid(z))).astype(BF16)


def _rglru_prompt(uz, conv_init, h_init, cw, cb, wri, br, bi, lam):
    t, d2 = uz.shape
    d = d2 // 2
    tm = min(t, 256)
    full = lambda shape: pl.BlockSpec(shape, lambda i: (0,) * len(shape))
    return pl.pallas_call(
        _rglru_prompt_kernel,
        grid=(t // tm,),
        in_specs=[pl.BlockSpec((tm, d), lambda i: (i, 0)),
                  pl.BlockSpec((tm, d), lambda i: (i, 1)),
                  full((SUBLANES, d)), full((1, d)), full((CONV_W, d)), full((1, d)),
                  full(wri.shape), full((1, d)), full((1, d)), full((1, d))],
        out_specs=[pl.BlockSpec((tm, d), lambda i: (i, 0)), full((1, d))],
        out_shape=[jax.ShapeDtypeStruct((t, d), BF16), jax.ShapeDtypeStruct((1, d), F32)],
        scratch_shapes=[pltpu.VMEM((tm + SUBLANES, d), F32), pltpu.VMEM((tm, d), F32),
                        pltpu.VMEM((tm, d), F32), pltpu.VMEM((tm, d), F32),
                        pltpu.VMEM((1, d), F32), pltpu.VMEM((1, d), F32)],
        compiler_params=_cparams(("arbitrary",)),
        name="rglru_prompt",
    )(uz, uz, conv_init, h_init, cw, cb.reshape(1, d), wri, br.reshape(1, d), bi.reshape(1, d),
      lam.reshape(1, d))


def _rglru_sample_kernel(up_ref, z_ref, h0_ref, cw_ref, cb_ref, wri_ref, br_ref, bi_ref, lam_ref,
                         g_ref, hlast_ref, sp_scr):
    n_t = z_ref.shape[0]
    d = z_ref.shape[2]
    bw = d // N_RNN_BLOCKS
    sp_scr[...] = _softplus(-lam_ref[...])
    for n in range(N_RNN_BLOCKS):
        sl = slice(n * bw, (n + 1) * bw)
        h = h0_ref[:, sl]
        for t in range(n_t):
            xc = cb_ref[:, sl]
            for k in range(CONV_W):
                xc = xc + up_ref[t + k, :, sl] * cw_ref[k:k + 1, sl]
            a, b = _lru_gates(xc, wri_ref, br_ref, bi_ref, sp_scr, n, bw)
            h = a * h + b
            z = z_ref[t, :, sl]
            g_ref[t, :, sl] = (h * (z * _sigmoid(z))).astype(BF16)
        hlast_ref[:, sl] = h


def _rglru_sample(up, z, h0, cw, cb, wri, br, bi, lam):
    n_t, nb, d = z.shape
    return pl.pallas_call(
        _rglru_sample_kernel,
        out_shape=[jax.ShapeDtypeStruct((n_t, nb, d), BF16), jax.ShapeDtypeStruct((nb, d), F32)],
        scratch_shapes=[pltpu.VMEM((1, d), F32)],
        compiler_params=pltpu.CompilerParams(vmem_limit_bytes=VMEM_LIMIT),
        name="rglru_sample",
    )(up, z, h0, cw, cb.reshape(1, d), wri, br.reshape(1, d), bi.reshape(1, d), lam.reshape(1, d))


def _outproj_ple_kernel(a_ref, res_ref, p_ref, wo_ref, wp_ref, gn_ref, wg_ref, o_ref, h_scr, hn_scr):
    tm, d = res_ref.shape
    ch = min(d, 512)
    a = a_ref[...]
    ssq = jnp.zeros((tm, 1), F32)
    for c in range(d // ch):
        sl = slice(c * ch, (c + 1) * ch)
        h = res_ref[:, sl] + jnp.dot(a, wo_ref[:, sl], preferred_element_type=F32)
        h_scr[:, sl] = h
        ssq = ssq + jnp.sum(h * h, axis=-1, keepdims=True)
    inv = lax.rsqrt(ssq * (1.0 / d) + EPS)
    for c in range(d // ch):
        sl = slice(c * ch, (c + 1) * ch)
        hn_scr[:, sl] = (h_scr[:, sl] * inv * gn_ref[:, sl]).astype(BF16)
    pb = p_ref[...].astype(BF16)
    hn = hn_scr[...]
    for c in range(d // ch):
        sl = slice(c * ch, (c + 1) * ch)
        gate = _sigmoid(jnp.dot(hn, wg_ref[:, sl], preferred_element_type=F32))
        pp = jnp.dot(pb, wp_ref[:, sl], preferred_element_type=F32)
        o_ref[:, sl] = h_scr[:, sl] + pp * gate


def _outproj_ple(a, res, p, w_out, w_proj, g_norm, w_gate):
    m, d = res.shape
    pd = p.shape[1]
    tm = min(m, 512)
    return pl.pallas_call(
        _outproj_ple_kernel,
        grid=(m // tm,),
        in_specs=[pl.BlockSpec((tm, a.shape[1]), lambda i: (i, 0)),
                  pl.BlockSpec((tm, d), lambda i: (i, 0)),
                  pl.BlockSpec((tm, pd), lambda i: (i, 0)),
                  _resident(w_out.shape, lambda i: (0, 0)),
                  _resident(w_proj.shape, lambda i: (0, 0)),
                  _resident((1, d), lambda i: (0, 0)),
                  _resident(w_gate.shape, lambda i: (0, 0))],
        out_specs=pl.BlockSpec((tm, d), lambda i: (i, 0)),
        out_shape=jax.ShapeDtypeStruct((m, d), F32),
        scratch_shapes=[pltpu.VMEM((tm, d), F32), pltpu.VMEM((tm, d), BF16)],
        compiler_params=_cparams(("parallel",)),
        name="outproj_ple",
    )(a, res, p, w_out, w_proj, g_norm.reshape(1, d), w_gate)


AUG_W = 2 * HEAD_DIM
MASK_OFF = -2.0 ** 60


def _kv_proj_kernel(x_ref, g_ref, w_ref, gsel_ref, gwin_ref,
                    kc_ref, vc_ref, ks_ref, vs_ref, kw_ref, vw_ref, ksa_ref, vsa_ref, kwb_ref, vwa_ref):
    tm = x_ref.shape[0]
    xn = _rms_rows(x_ref[...], g_ref[...]).astype(BF16)
    outs = (kc_ref, vc_ref, ks_ref, vs_ref, kw_ref, vw_ref)
    head_gain = {2: gsel_ref, 4: gwin_ref}
    pos = pl.program_id(0) * tm + lax.broadcasted_iota(jnp.int32, (tm, HEAD_DIM), 0)
    blk = lax.broadcasted_iota(jnp.int32, (tm, HEAD_DIM), 1)
    block_mask = jnp.where(blk == pos // SEL_BLOCK, MASK_OFF, 0.0).astype(BF16)
    ones = jnp.ones((tm, HEAD_DIM), BF16)
    for s in range(6):
        y = jnp.dot(xn, w_ref[:, s * KV_W:(s + 1) * KV_W], preferred_element_type=F32)
        for g in range(N_KV):
            yg = y[:, g * HEAD_DIM:(g + 1) * HEAD_DIM]
            if s in head_gain:
                yg = _rms_rows(yg, head_gain[s][...])
            outs[s][pl.ds(g, tm, stride=N_KV), :] = yg
            lo = slice(g * AUG_W, g * AUG_W + HEAD_DIM)
            hi = slice(g * AUG_W + HEAD_DIM, (g + 1) * AUG_W)
            if s == 2:
                ksa_ref[:, lo] = yg.astype(BF16)
                ksa_ref[:, hi] = block_mask
            elif s == 3:
                vsa_ref[:, lo] = yg.astype(BF16)
                vsa_ref[:, hi] = ones
            elif s == 4:
                kwb_ref[:, g * HEAD_DIM:(g + 1) * HEAD_DIM] = yg.astype(BF16)
            elif s == 5:
                vwa_ref[:, lo] = yg.astype(BF16)
                vwa_ref[:, hi] = ones


def _kv_proj(x, g, w, g_sel, g_win):
    m, d = x.shape
    tm = min(m, 512)
    row = lambda i: (i, 0)
    aug = N_KV * AUG_W
    return pl.pallas_call(
        _kv_proj_kernel,
        grid=(m // tm,),
        in_specs=[pl.BlockSpec((tm, d), row), _resident((1, d), lambda i: (0, 0)),
                  _resident(w.shape, lambda i: (0, 0)),
                  _resident((1, HEAD_DIM), lambda i: (0, 0)), _resident((1, HEAD_DIM), lambda i: (0, 0))],
        out_specs=([pl.BlockSpec((tm * N_KV, HEAD_DIM), row)] * 6
                   + [pl.BlockSpec((tm, aug), row), pl.BlockSpec((tm, aug), row),
                      pl.BlockSpec((tm, KV_W), row), pl.BlockSpec((tm, aug), row)]),
        out_shape=([jax.ShapeDtypeStruct((m * N_KV, HEAD_DIM), F32)] * 6
                   + [jax.ShapeDtypeStruct((m, aug), BF16), jax.ShapeDtypeStruct((m, aug), BF16),
                      jax.ShapeDtypeStruct((m, KV_W), BF16), jax.ShapeDtypeStruct((m, aug), BF16)]),
        compiler_params=_cparams(("parallel",)),
        name="kv_proj",
    )(x, g.reshape(1, d), w, g_sel.reshape(1, HEAD_DIM), g_win.reshape(1, HEAD_DIM))


PIECE_ROWS = CMP_STRIDE * N_KV
PIECE_PITCH = PIECE_ROWS + SUBLANES


def _piece_rows(buf, r, n):
    flat = buf.reshape(buf.shape[0] * PIECE_PITCH, HEAD_DIM)
    return flat[pl.ds(r, n, stride=PIECE_PITCH), :]


def _compress_kernel(pt_ref, kc_hbm, vc_hbm, pe_ref, w1_ref, b1_ref, w2_ref, b2_ref, gk_ref,
                     kcmp_ref, vcmp_ref, kbuf, vbuf, sem, lhs, bias_scr, *, n_quarters, pages_per_step):
    s = pl.program_id(0)
    n_steps = pl.num_programs(0)
    pieces_per_page = kc_hbm.shape[1]
    n_piece = pages_per_step * pieces_per_page
    n_load = n_piece + SUBLANES
    rows_g = n_piece + 2 * SUBLANES
    hidden = w2_ref.shape[1]

    def copies(step, slot):
        b = step // n_quarters
        q = step % n_quarters
        out = []
        for hbm, buf, which in ((kc_hbm, kbuf, 0), (vc_hbm, vbuf, 1)):
            for p in range(pages_per_step):
                pg = pt_ref[b, q * pages_per_step + p]
                dst = buf.at[slot, pl.ds(p * pieces_per_page, pieces_per_page), pl.ds(0, PIECE_ROWS), :]
                out.append(pltpu.make_async_copy(hbm.at[pg], dst, sem.at[slot, which]))
            pg = pt_ref[b, q * pages_per_step + pages_per_step]
            out.append(pltpu.make_async_copy(hbm.at[pg, 0], buf.at[slot, n_piece, pl.ds(0, PIECE_ROWS), :],
                                             sem.at[slot, which]))
        return out

    slot = s % 2

    @pl.when(s == 0)
    def _():
        for c in copies(0, 0):
            c.start()
        kbuf[:, n_piece + 1:] = jnp.zeros((2, SUBLANES - 1) + kbuf.shape[2:], F32)
        vbuf[:, n_piece + 1:] = jnp.zeros((2, SUBLANES - 1) + vbuf.shape[2:], F32)
        lhs[...] = jnp.zeros(lhs.shape, BF16)
        for which in range(2):
            pw = jnp.dot(pe_ref[which], w1_ref[which], preferred_element_type=F32)
            bias_scr[which] = (jnp.sum(pw[:SUBLANES, :hidden], axis=0, keepdims=True)
                               + jnp.sum(pw[SUBLANES:, hidden:], axis=0, keepdims=True) + b1_ref[which])

    @pl.when(s + 1 < n_steps)
    def _():
        for c in copies(s + 1, 1 - slot):
            c.start()

    for c in copies(s, slot):
        c.wait()

    for which, buf, out_ref in ((0, kbuf, kcmp_ref), (1, vbuf, vcmp_ref)):
        for g in range(N_KV):
            for pos in range(CMP_STRIDE):
                x = _piece_rows(buf.at[slot], pos * N_KV + g, n_load)
                lhs[which, g * rows_g:g * rows_g + n_load, pos * HEAD_DIM:(pos + 1) * HEAD_DIM] = x.astype(BF16)
        hab = jnp.dot(lhs[which], w1_ref[which], preferred_element_type=F32)
        for g in range(N_KV):
            blk = hab[g * rows_g:(g + 1) * rows_g]
            nxt = pltpu.roll(blk[:, hidden:], rows_g - 1, axis=0)
            hid = blk[:n_piece, :hidden] + nxt[:n_piece] + bias_scr[which]
            hid = hid * _sigmoid(hid)
            y = jnp.dot(hid.astype(BF16), w2_ref[which], preferred_element_type=F32) + b2_ref[which]
            if which == 0:
                y = _rms_rows(y, gk_ref[...])
            out_ref[0, g] = y.astype(BF16)


def _compress(kc_pages, vc_pages, table, pe_rows, w1, b1, w2, b2, g_k):
    nb, n_pages = table.shape[0], table.shape[1] - 1
    ppp = kc_pages.shape[1]
    pps = min(CMP_PAGES_PER_STEP, n_pages)
    n_quarters = n_pages // pps
    n_piece = pps * ppp
    kdim = CMP_STRIDE * HEAD_DIM
    hidden = w2.shape[1]
    full = lambda shape: pl.BlockSpec(shape, lambda s, pt: (0,) * len(shape))
    out_spec = pl.BlockSpec((1, N_KV, n_piece, HEAD_DIM), lambda s, pt: (s // n_quarters, 0, s % n_quarters, 0))
    out_sd = jax.ShapeDtypeStruct((nb, N_KV, n_pages * ppp, HEAD_DIM), BF16)
    buf = pltpu.VMEM((2, n_piece + SUBLANES, PIECE_PITCH, HEAD_DIM), F32)
    lhs = pltpu.VMEM((2, N_KV * (n_piece + 2 * SUBLANES), kdim), BF16)
    grid_spec = pltpu.PrefetchScalarGridSpec(
        num_scalar_prefetch=1,
        grid=(nb * n_quarters,),
        in_specs=[pl.BlockSpec(memory_space=pl.ANY), pl.BlockSpec(memory_space=pl.ANY),
                  full(pe_rows.shape), full(w1.shape), full(b1.shape), full(w2.shape), full(b2.shape),
                  full((1, HEAD_DIM))],
        out_specs=[out_spec, out_spec],
        scratch_shapes=[buf, buf, pltpu.SemaphoreType.DMA((2, 2)), lhs, pltpu.VMEM((2, 1, hidden), F32)],
    )
    return pl.pallas_call(
        functools.partial(_compress_kernel, n_quarters=n_quarters, pages_per_step=pps),
        grid_spec=grid_spec,
        out_shape=[out_sd, out_sd],
        compiler_params=_cparams(("arbitrary",)),
        name="compress",
    )(table, kc_pages, vc_pages, pe_rows, w1, b1, w2, b2, g_k.reshape(1, HEAD_DIM))


def _rel_bucket(dist):
    n = jnp.maximum(dist, 0)
    max_exact = N_BUCKETS // 2
    nf = jnp.maximum(n, 1).astype(F32)
    large = max_exact + (jnp.log(nf / max_exact) / math.log(MAX_DISTANCE / max_exact)
                         * (N_BUCKETS - max_exact)).astype(jnp.int32)
    large = jnp.minimum(large, N_BUCKETS - 1)
    return jnp.where(n < max_exact, n, large) & (N_BUCKETS - 1)


def _bias_table(rel_bias, dist, masked, gain=1.0):
    dist = lax.optimization_barrier(dist)
    onehot = (_rel_bucket(dist)[None, :] == jnp.arange(N_BUCKETS)[:, None]).astype(F32)
    table = jnp.dot(rel_bias.astype(F32).T * gain, onehot, precision=lax.Precision.HIGHEST)
    return jnp.where(dist[None, :] >= 0, table, NEG) if masked else table


def _bias_bank_kernel(gs_ref, gw_ref, g2_ref, sb_ref, wb_ref, rr_ref):
    for g_ref, o_ref in ((gs_ref, sb_ref), (gw_ref, wb_ref)):
        wp = g_ref.shape[2]
        x = jnp.broadcast_to(g_ref[0], (TQ, wp))
        o_ref[0] = pltpu.roll(x, wp - (TQ - 1), 1, stride=1, stride_axis=0)[:, :wp - TQ]

    n_blocks = g2_ref.shape[1]
    for il in range(SUBLANES):
        left = CMP_STRIDE * (SUBLANES - 1 - il)
        y = pltpu.roll(g2_ref[0], (2 * TQ - left) % (2 * TQ), 1)
        rr_ref[0, pl.ds(il, n_blocks, stride=SUBLANES), :] = y[:, :TQ]


def _bias_banks(rel_bias, n_qt, n_row_blocks, bank_k, bank_w, shift):
    nh = rel_bias.shape[1]
    wk = WINDOW + TQ
    gs = _bias_table(rel_bias, bank_k + TQ - 1 - jnp.arange(bank_k + bank_w + TQ), True, LOG2E)
    dw = WINDOW + TQ - 1 - jnp.arange(WINDOW + wk + TQ)
    gw = _bias_table(rel_bias, jnp.where(dw < WINDOW, dw, -1), True, LOG2E)
    n_blocks = -(-(n_qt + n_row_blocks - 1) // SUBLANES) * SUBLANES
    d2 = (TQ * (n_qt - 1 - jnp.arange(n_blocks))[:, None] + jnp.arange(2 * TQ)[None, :]
          - (CMP_BLOCK - 1) - CMP_STRIDE * (SUBLANES - 1))
    g2 = _bias_table(rel_bias, d2.reshape(-1), True, LOG2E).reshape(nh, n_blocks, 2 * TQ)
    spec3 = lambda a: pl.BlockSpec((1,) + a.shape[1:], lambda h: (h, 0, 0))
    outs = [jax.ShapeDtypeStruct((nh, TQ, bank_k + bank_w), F32), jax.ShapeDtypeStruct((nh, TQ, WINDOW + wk), F32),
            jax.ShapeDtypeStruct((nh, n_blocks * SUBLANES, TQ), F32)]
    gs, gw, g2 = gs[:, None] - shift, gw[:, None] - shift, g2 - shift
    return pl.pallas_call(
        _bias_bank_kernel,
        grid=(nh,),
        in_specs=[spec3(gs), spec3(gw), spec3(g2)],
        out_specs=[spec3(o) for o in outs],
        out_shape=outs,
        compiler_params=_cparams(("parallel",)),
        name="bias_banks",
    )(gs, gw, g2)


def _query_rows_bias(rel_bias, t0, n_tok, n_keys):
    dist = (t0 + jnp.arange(n_tok)[:, None] - jnp.arange(n_keys)[None, :]).reshape(-1)
    return _bias_table(rel_bias, dist, False).reshape(-1, n_tok, n_keys)


def _first_constant_distance():
    ratio = MAX_DISTANCE / (N_BUCKETS // 2)
    return int(math.ceil((N_BUCKETS // 2) * ratio ** ((N_BUCKETS // 2 - 1) / (N_BUCKETS // 2)))) + 2


def _topk_rows(score, blk, k_top):
    n_blk = score.shape[0]
    for _ in range(k_top):
        m = jnp.max(score, axis=0, keepdims=True)
        idx = jnp.min(jnp.where(score == m, blk, n_blk), axis=0, keepdims=True)
        score = jnp.where(blk == idx, REMOVED, score)
    return score == REMOVED


def _attn_prompt_kernel(q_ref, z0_ref, z1_ref, z2_ref, gl_ref, gq_ref, kcmp_ref, vcmp_ref,
                        ksa_ref, vsa_ref, kw_ref, vwa_ref, rr_ref, sb_ref, wb_ref, bounded_ref, o_ref,
                        q_scr, psum_scr, ocmp_scr, owin_scr, m_scr, acc_scr, s0_scr, s1_scr, *, n_qtiles, bank_k):
    qi = pl.program_id(1)
    q0 = qi * TQ
    n_cmp_rows = kcmp_ref.shape[1]
    n_sb = n_cmp_rows * CMP_STRIDE // SEL_BLOCK
    hg = HEADS_PER_GROUP
    q_gain = HEAD_DIM ** -0.5 * LOG2E
    wk = WINDOW + TQ

    def per_tile_branches(t, bounded):
        tile = qi + t
        t0 = tile * TQ
        for h in range(hg):
            qh = _rms_rows(q_ref[t * TQ:(t + 1) * TQ, h * HEAD_DIM:(h + 1) * HEAD_DIM], gq_ref[...]) * q_gain
            q_scr[t, h * TQ:(h + 1) * TQ, 0:HEAD_DIM] = qh.astype(BF16)
        q_all = q_scr[t, :, 0:HEAD_DIM]

        s_t = lax.dot_general(kcmp_ref[0], q_all, (((1,), (1,)), ((), ())), preferred_element_type=F32)
        rr_start = pl.multiple_of((n_qtiles - 1 - tile) * SUBLANES, SUBLANES)
        psum = jnp.zeros((n_cmp_rows, TQ), F32)
        for h in range(hg):
            sh = s_t[:, h * TQ:(h + 1) * TQ] + rr_ref[h, pl.ds(rr_start, n_cmp_rows), :]
            if bounded:
                e = jnp.exp2(sh)
            else:
                e = jnp.exp2(sh - jnp.maximum(jnp.max(sh, axis=0, keepdims=True), 0.5 * NEG))
            den = jnp.sum(e, axis=0, keepdims=True)
            p = e * (1.0 / jnp.where(den > 0.0, den, 1.0))
            psum = psum + p
            ocmp_scr[t, h * TQ:(h + 1) * TQ, :] = lax.dot_general(
                p.astype(BF16), vcmp_ref[0], (((0,), (0,)), ((), ())), preferred_element_type=F32)

        w0 = pl.multiple_of(jnp.maximum(t0 - WINDOW, 0), LANES)
        s_w = lax.dot_general(q_all, kw_ref[pl.ds(w0, wk), :], (((1,), (1,)), ((), ())),
                              preferred_element_type=F32)
        vwin = vwa_ref[pl.ds(w0, wk), :]
        wb_start = pl.multiple_of(WINDOW - (t0 - w0), LANES)
        for h in range(hg):
            sh = s_w[h * TQ:(h + 1) * TQ] + wb_ref[h, :, pl.ds(wb_start, wk)]
            e = jnp.exp2(sh if bounded else sh - jnp.max(sh, axis=1, keepdims=True))
            o_aug = jnp.dot(e.astype(BF16), vwin, preferred_element_type=F32)
            owin_scr[t, h * TQ:(h + 1) * TQ, :] = o_aug[:, 0:HEAD_DIM] / o_aug[:, HEAD_DIM:AUG_W]

        psum_scr[t, 0:SUBLANES, :] = jnp.zeros((SUBLANES, TQ), F32)
        psum_scr[t, SUBLANES:, :] = psum
        ratio = SEL_BLOCK // CMP_STRIDE
        imp = psum_scr[t, pl.ds(SUBLANES - 1, n_sb, stride=ratio), :]
        for k in range(ratio):
            imp = imp + psum_scr[t, pl.ds(SUBLANES + k, n_sb, stride=ratio), :]
        blk = lax.broadcasted_iota(jnp.int32, (n_sb, 1), 0)
        cur = (t0 + lax.broadcasted_iota(jnp.int32, (1, TQ), 1)) // SEL_BLOCK
        blk_ok = blk <= cur
        forced = (blk == 0) | (blk == cur) | (blk == cur - 1)
        score = jnp.where(blk_ok, jnp.where(forced, FORCE, imp), -FORCE)
        sel_t = _topk_rows(score, blk, min(N_SELECT, n_sb)) & blk_ok
        not_sel = jnp.where(sel_t, 0.0, 1.0).T
        if n_sb < HEAD_DIM:
            not_sel = jnp.concatenate([not_sel, jnp.zeros((TQ, HEAD_DIM - n_sb), F32)], axis=1)
        for h in range(hg):
            q_scr[t, h * TQ:(h + 1) * TQ, HEAD_DIM:AUG_W] = not_sel.astype(BF16)

    half = 0
    n_kv = (q0 + TQ + TK_SEL - 1) // TK_SEL
    rep = TK_SEL // LANES

    def scores(j, s_ref):
        k0 = pl.multiple_of(j * TK_SEL, TK_SEL)
        s_ref[...] = lax.dot_general(q_scr[half], ksa_ref[pl.ds(k0, TK_SEL), :], (((1,), (1,)), ((), ())),
                                     preferred_element_type=F32)

    def kv_tile(j, s_ref, bounded):
        k0 = pl.multiple_of(j * TK_SEL, TK_SEL)
        vt = vsa_ref[pl.ds(k0, TK_SEL), :]
        b_start = pl.multiple_of(jnp.maximum(bank_k - (q0 - k0), 0), LANES)
        for h in range(hg):
            rows = slice(h * TQ, (h + 1) * TQ)
            sh = s_ref[rows] + sb_ref[h, :, pl.ds(b_start, TK_SEL)]
            if bounded:
                acc_scr[rows] += jnp.dot(jnp.exp2(sh).astype(BF16), vt, preferred_element_type=F32)
            else:
                m_prev = m_scr[rows]
                m_next = jnp.maximum(m_prev, jnp.max(sh, axis=1, keepdims=True))
                alpha = jnp.exp2(m_prev - m_next)
                e = jnp.exp2(sh - jnp.tile(m_next, (1, rep)))
                acc_scr[rows] = (jnp.tile(alpha, (1, AUG_W // LANES)) * acc_scr[rows]
                                 + jnp.dot(e.astype(BF16), vt, preferred_element_type=F32))
                m_scr[rows] = m_next

    def selected_branch(bounded):
        m_scr[...] = jnp.full(m_scr.shape, NEG, F32)
        acc_scr[...] = jnp.zeros(acc_scr.shape, F32)

        if not bounded:
            def one_tile(j, carry):
                scores(j, s0_scr)
                kv_tile(j, s0_scr, False)
                return carry

            lax.fori_loop(0, n_kv, one_tile, 0)
            return

        def run(first, count):
            bufs = (s0_scr, s1_scr)
            for k in range(count):
                scores(first + k + 1, bufs[(k + 1) % 2])
                kv_tile(first + k, bufs[k % 2], bounded)

        def kv_run(i, carry):
            run(KV_RUN * i, KV_RUN)
            return carry

        scores(0, s0_scr)
        n_before = n_kv - 1
        lax.fori_loop(0, n_before // KV_RUN, kv_run, 0)
        size = KV_RUN // 2
        while size >= 2:
            start = n_before // (2 * size) * (2 * size)

            @pl.when(n_before - start >= size)
            def _(start=start, size=size):
                run(start, size)

            size //= 2

        done2 = n_before // 2 * 2

        @pl.when(n_before > done2)
        def _():
            scores(done2 + 1, s1_scr)
            kv_tile(done2, s0_scr, bounded)
            kv_tile(done2 + 1, s1_scr, bounded)

        @pl.when(n_before == done2)
        def _():
            kv_tile(done2, s0_scr, bounded)

    for bounded in (True, False):
        @pl.when(bounded_ref[0] == int(bounded))
        def _(bounded=bounded):
            per_tile_branches(0, bounded)
            selected_branch(bounded)

    sig_gl = _sigmoid(gl_ref[...])
    z_refs = (z0_ref, z1_ref, z2_ref)
    for h in range(hg):
        rows = slice(h * TQ, (h + 1) * TQ)
        cols = slice(h * HEAD_DIM, (h + 1) * HEAD_DIM)
        o_sel = acc_scr[rows, 0:HEAD_DIM] / acc_scr[rows, HEAD_DIM:AUG_W]
        out = jnp.zeros((TQ, HEAD_DIM), F32)
        for br, o_br in enumerate((ocmp_scr[half, rows, :], o_sel, owin_scr[half, rows, :])):
            z = z_refs[br][:, cols]
            c = br * hg + h
            out = out + sig_gl[:, c:c + 1] * o_br * (z * _sigmoid(z))
        o_ref[:, cols] = out.astype(BF16)


def _attn_prompt(proj, gl_g, g_q, kcmp, vcmp, ks_aug, vs_aug, kw_b, vw_aug, rr, sb, wb, bounded, bank_k):
    s_len = proj.shape[0]
    n_qt = s_len // TQ
    n_cmp_rows = kcmp.shape[2]
    gcols = N_HEADS * HEAD_DIM // GROUP_W
    qz = lambda seg: pl.BlockSpec((TQ, GROUP_W), lambda g, i, seg=seg: (i, seg * gcols + g))
    per_group = lambda shape: _resident((1,) + shape, lambda g, i: (g,) + (0,) * len(shape))
    kv_spec = lambda width: _resident((s_len, width), lambda g, i: (0, g))
    return pl.pallas_call(
        functools.partial(_attn_prompt_kernel, n_qtiles=n_qt, bank_k=bank_k),
        grid=(N_KV, n_qt),
        in_specs=[qz(0), qz(1), qz(2), qz(3),
                  pl.BlockSpec((TQ, LANES), lambda g, i: (i, g)),
                  _resident((1, HEAD_DIM), lambda g, i: (0, 0)),
                  per_group((n_cmp_rows, HEAD_DIM)), per_group((n_cmp_rows, HEAD_DIM)),
                  kv_spec(AUG_W), kv_spec(AUG_W), kv_spec(HEAD_DIM), kv_spec(AUG_W),
                  _resident((HEADS_PER_GROUP,) + rr.shape[1:], lambda g, i: (g, 0, 0)),
                  _resident((HEADS_PER_GROUP,) + sb.shape[1:], lambda g, i: (g, 0, 0)),
                  _resident((HEADS_PER_GROUP,) + wb.shape[1:], lambda g, i: (g, 0, 0)),
                  pl.BlockSpec(memory_space=pltpu.SMEM)],
        out_specs=pl.BlockSpec((TQ, GROUP_W), lambda g, i: (i, g)),
        out_shape=jax.ShapeDtypeStruct((s_len, N_HEADS * HEAD_DIM), BF16),
        scratch_shapes=[pltpu.VMEM((1, HEADS_PER_GROUP * TQ, AUG_W), BF16),
                        pltpu.VMEM((1, n_cmp_rows + SUBLANES, TQ), F32),
                        pltpu.VMEM((1, HEADS_PER_GROUP * TQ, HEAD_DIM), F32),
                        pltpu.VMEM((1, HEADS_PER_GROUP * TQ, HEAD_DIM), F32),
                        pltpu.VMEM((HEADS_PER_GROUP * TQ, LANES), F32),
                        pltpu.VMEM((HEADS_PER_GROUP * TQ, AUG_W), F32),
                        pltpu.VMEM((HEADS_PER_GROUP * TQ, TK_SEL), F32),
                        pltpu.VMEM((HEADS_PER_GROUP * TQ, TK_SEL), F32)],
        compiler_params=_cparams(("arbitrary", "arbitrary")),
        name="attn_prompt",
    )(proj, proj, proj, proj, gl_g, g_q.reshape(1, HEAD_DIM), kcmp[0], vcmp[0],
      ks_aug, vs_aug, kw_b, vw_aug, rr, sb, wb, bounded)


def _attn_sample_kernel(*refs, n_tok, pages_per_chunk, page, past_len):
    pt_ref = refs[0]
    q_ref, z_ref, gl_ref, gq_ref, kcmp_ref, vcmp_ref = refs[1:7]
    kpages = refs[7:7 + pages_per_chunk]
    vpages = refs[7 + pages_per_chunk:7 + 2 * pages_per_chunk]
    (ksn_ref, vsn_ref, kwn_ref, vwn_ref, kwc_ref, vwc_ref, bcmp_ref, bsel_ref, bwin_ref, ov_ref, ex_ref,
     o_ref, sel_scr, m_scr, l_scr, acc_scr, q_scr, ocmp_scr) = refs[7 + 2 * pages_per_chunk:]
    del pt_ref
    c = pl.program_id(1)
    n_chunks = pl.num_programs(1)
    hg = HEADS_PER_GROUP
    nr = hg * n_tok
    scale = HEAD_DIM ** -0.5
    n_cmp_rows = kcmp_ref.shape[2]
    n_sb = ov_ref.shape[0]
    chunk = pages_per_chunk * page
    tok = lax.broadcasted_iota(jnp.int32, (N_KV * nr, 1), 0) % n_tok

    @pl.when(c == 0)
    def _():
        q_scr[...] = _rms_rows(q_ref[0], gq_ref[...]).astype(BF16)
        m_scr[...] = jnp.full(m_scr.shape, NEG, F32)
        l_scr[...] = jnp.zeros(l_scr.shape, F32)
        acc_scr[...] = jnp.zeros(acc_scr.shape, F32)
        ci = lax.broadcasted_iota(jnp.int32, (1, n_cmp_rows), 1)
        cmp_valid = ci < n_cmp_rows - 1
        s = jnp.concatenate(
            [lax.dot_general(q_scr[g * nr:(g + 1) * nr], kcmp_ref[0, g], (((1,), (1,)), ((), ())),
                             preferred_element_type=F32) for g in range(N_KV)], axis=0)
        s = jnp.where(cmp_valid, s * scale + bcmp_ref[...], NEG)
        e = jnp.where(cmp_valid, jnp.exp(s - jnp.max(s, axis=1, keepdims=True)), 0.0)
        p = e / jnp.sum(e, axis=1, keepdims=True)
        pb = p.astype(BF16)
        psums = []
        for g in range(N_KV):
            rows = slice(g * nr, (g + 1) * nr)
            ocmp_scr[rows] = jnp.dot(pb[rows], vcmp_ref[0, g], preferred_element_type=F32)
            psum = p[g * nr:g * nr + n_tok]
            for h in range(1, hg):
                psum = psum + p[g * nr + h * n_tok:g * nr + (h + 1) * n_tok]
            psums.append(psum)
        rest = jnp.concatenate(psums, axis=0)
        imp_t = jnp.zeros((n_sb, N_KV * n_tok), F32)
        for _ in range(3):
            piece = rest.astype(BF16)
            rest = rest - piece.astype(F32)
            imp_t = imp_t + lax.dot_general(ov_ref[...], piece, (((1,), (1,)), ((), ())),
                                            preferred_element_type=F32)
        blk = lax.broadcasted_iota(jnp.int32, (n_sb, 1), 0)
        forced = (blk == 0) | (blk == n_sb - 1)
        score = jnp.where(forced, FORCE, imp_t)
        sel_t = _topk_rows(score, blk, min(N_SELECT, n_sb + 1) - 1)
        sel_f = jnp.where(sel_t, 1.0, 0.0)
        if N_KV * n_tok < LANES:
            sel_f = jnp.concatenate([sel_f, jnp.zeros((n_sb, LANES - N_KV * n_tok), F32)], axis=1)
        sel_f = sel_f.T
        for g in range(N_KV):
            sel_scr[g * nr:(g + 1) * nr] = jnp.tile(sel_f[g * n_tok:(g + 1) * n_tok], (hg, 1)).astype(BF16)

    def group_rows(ref, g):
        return ref[0, pl.ds(g, ref.shape[1] // N_KV, stride=N_KV), :].astype(BF16)

    def all_scores(k_of_group):
        return jnp.concatenate(
            [lax.dot_general(q_scr[g * nr:(g + 1) * nr], k_of_group(g), (((1,), (1,)), ((), ())),
                             preferred_element_type=F32) for g in range(N_KV)], axis=0)

    def all_values(e, v_of_group):
        eb = e.astype(BF16)
        return jnp.concatenate(
            [jnp.dot(eb[g * nr:(g + 1) * nr], v_of_group(g), preferred_element_type=F32) for g in range(N_KV)],
            axis=0)

    def online_update(sh, v_of_group):
        m_prev = m_scr[...]
        m_next = jnp.maximum(m_prev, jnp.max(sh, axis=1, keepdims=True))
        alpha = jnp.exp(m_prev - m_next)
        e = jnp.exp(sh - m_next[:, 0:1])
        l_scr[...] = alpha * l_scr[...] + jnp.sum(e, axis=1, keepdims=True)
        acc_scr[...] = alpha * acc_scr[...] + all_values(e, v_of_group)
        m_scr[...] = m_next

    k0 = pl.multiple_of(c * chunk, chunk)
    s = all_scores(lambda g: jnp.concatenate([group_rows(r, g) for r in kpages], axis=0))
    picked = jnp.dot(sel_scr[...], ex_ref[:, pl.ds(k0, chunk)], preferred_element_type=F32)
    sh = jnp.where(picked > 0.5, s * scale + bsel_ref[:, pl.ds(k0, chunk)], NEG)
    online_update(sh, lambda g: jnp.concatenate([group_rows(r, g) for r in vpages], axis=0))

    @pl.when(c == n_chunks - 1)
    def _():
        new_w = ksn_ref.shape[1] // N_KV
        jn = lax.broadcasted_iota(jnp.int32, (1, new_w), 1)
        new_valid = jn <= tok
        wbuf = kwc_ref.shape[1] // N_KV
        jw = lax.broadcasted_iota(jnp.int32, (1, wbuf), 1)
        dist_c = (past_len + tok) - (past_len - wbuf + jw)
        win_valid_c = dist_c < WINDOW
        s = all_scores(lambda g: group_rows(ksn_ref, g))
        sh = jnp.where(new_valid, s * scale + bsel_ref[:, pl.ds(past_len, new_w)], NEG)
        online_update(sh, lambda g: group_rows(vsn_ref, g))
        o_sel = acc_scr[...] / l_scr[...]
        s_c = all_scores(lambda g: group_rows(kwc_ref, g))
        s_n = all_scores(lambda g: group_rows(kwn_ref, g))
        s_c = jnp.where(win_valid_c, s_c * scale + bwin_ref[:, 0:wbuf], NEG)
        s_n = jnp.where(new_valid, s_n * scale + bwin_ref[:, wbuf:wbuf + new_w], NEG)
        mx = jnp.maximum(jnp.max(s_c, axis=1, keepdims=True), jnp.max(s_n, axis=1, keepdims=True))
        e_c = jnp.exp(s_c - mx)
        e_n = jnp.exp(s_n - mx)
        den = jnp.sum(e_c, axis=1, keepdims=True) + jnp.sum(e_n, axis=1, keepdims=True)
        o_win = (all_values(e_c, lambda g: group_rows(vwc_ref, g))
                 + all_values(e_n, lambda g: group_rows(vwn_ref, g))) / den
        sig_gl = _sigmoid(gl_ref[0])
        out = jnp.zeros((N_KV * nr, HEAD_DIM), F32)
        for br, o_br in enumerate((ocmp_scr[...], o_sel, o_win)):
            z = z_ref[0, br]
            out = out + sig_gl[br] * o_br * (z * _sigmoid(z))
        o_ref[0] = out.astype(BF16)


def _attn_sample(q_s, z_s, gl_s, g_q, kcmp, vcmp, ksel_pages, vsel_pages, page_table, ks_new, vs_new,
                 kw_new, vw_new, kw_cache, vw_cache, bcmp, bsel, bwin, overlap, expand, n_tok):
    nb = q_s.shape[0]
    n_pages = page_table.shape[1]
    page = ksel_pages.shape[1] // N_KV
    ppc = min(SAMPLE_PAGES_PER_STEP, n_pages)
    n_chunks = n_pages // ppc
    nrow = q_s.shape[1]
    per_b = lambda shape: pl.BlockSpec((1,) + shape, lambda b, c, pt: (b,) + (0,) * len(shape))
    full = lambda shape: _resident(shape, lambda b, c, pt: (0,) * len(shape))
    page_spec = lambda i: pl.BlockSpec((1, page * N_KV, HEAD_DIM),
                                       lambda b, c, pt, i=i: (pt[b, c * ppc + i], 0, 0))
    in_specs = ([per_b(q_s.shape[1:]), per_b(z_s.shape[1:]), per_b(gl_s.shape[1:]), full((1, HEAD_DIM)),
                 per_b(kcmp.shape[1:]), per_b(vcmp.shape[1:])]
                + [page_spec(i) for i in range(ppc)] + [page_spec(i) for i in range(ppc)]
                + [per_b(ks_new.shape[1:])] * 4 + [per_b(kw_cache.shape[1:])] * 2
                + [full(bcmp.shape), full(bsel.shape), full(bwin.shape), full(overlap.shape), full(expand.shape)])
    grid_spec = pltpu.PrefetchScalarGridSpec(
        num_scalar_prefetch=1,
        grid=(nb, n_chunks),
        in_specs=in_specs,
        out_specs=pl.BlockSpec((1, nrow, HEAD_DIM), lambda b, c, pt: (b, 0, 0)),
        scratch_shapes=[pltpu.VMEM((nrow, expand.shape[0]), BF16),
                        pltpu.VMEM((nrow, LANES), F32), pltpu.VMEM((nrow, LANES), F32),
                        pltpu.VMEM((nrow, HEAD_DIM), F32), pltpu.VMEM((nrow, HEAD_DIM), BF16),
                        pltpu.VMEM((nrow, HEAD_DIM), F32)],
    )
    return pl.pallas_call(
        functools.partial(_attn_sample_kernel, n_tok=n_tok, pages_per_chunk=ppc, page=page,
                          past_len=n_pages * page),
        grid_spec=grid_spec,
        out_shape=jax.ShapeDtypeStruct((nb, nrow, HEAD_DIM), BF16),
        compiler_params=_cparams(("arbitrary", "arbitrary")),
        name="attn_sample",
    )(page_table, q_s, z_s, gl_s, g_q.reshape(1, HEAD_DIM), kcmp, vcmp,
      *([ksel_pages] * ppc), *([vsel_pages] * ppc), ks_new, vs_new, kw_new, vw_new, kw_cache, vw_cache,
      bcmp, bsel, bwin, overlap, expand)


def kernel(x_prompt, x_sample, p_prompt, p_sample, state_conv, state_h, cache_k_cmp, cache_v_cmp, cache_k_sel, cache_v_sel, cache_k_win, cache_v_win, page_table, norm_g, a_w_in, a_conv_w, a_conv_b, a_w_r, a_b_r, a_w_i, a_b_i, a_lam, a_w_out, kv_norm_g, w_kv, g_k_cmp, g_k_sel, g_k_win, cmp_pos, cmp_w1a, cmp_w1b, cmp_b1, cmp_w2, cmp_b2, b_w_in, b_g_q, b_w_out, rel_bias, ple_w_proj, ple_norm_g, ple_w_gate):
    bp, s_len, d = x_prompt.shape
    db, ds, _ = x_sample.shape
    n_pool, page = cache_k_sel.shape[:2]
    n_pages = page_table.shape[1]
    past_len = n_pages * page
    wbuf = cache_k_win.shape[1]
    hd = N_HEADS * HEAD_DIM
    assert bp == 1 and norm_g.shape[0] == 2 and a_w_in.shape[0] == 1 and b_w_in.shape[0] == 1
    assert d == hd and s_len % TK_SEL == 0 and s_len >= WINDOW + TQ and page % SEL_BLOCK == 0
    assert wbuf == WINDOW and ds <= SEL_BLOCK and past_len % TK_SEL == 0 and s_len // SEL_BLOCK <= HEAD_DIM

    w_in0 = a_w_in[0].astype(BF16)
    w_ri = jnp.concatenate([a_w_r[0], a_w_i[0]], axis=-1).astype(BF16)
    w_out0 = a_w_out[0].astype(BF16)
    w_gate = ple_w_gate.astype(BF16)
    w_proj = ple_w_proj.astype(BF16)
    w_kv_b = w_kv.astype(BF16)
    w_in1 = b_w_in[0].astype(BF16)
    w_gl = b_w_in[0][:, 4 * hd:].reshape(d, 3, N_KV, HEADS_PER_GROUP).transpose(0, 2, 1, 3)
    w_gl = jnp.pad(w_gl.reshape(d, N_KV, 3 * HEADS_PER_GROUP), ((0, 0), (0, 0), (0, LANES - 3 * HEADS_PER_GROUP)))
    w_gl = w_gl.reshape(d, N_KV * LANES).astype(BF16)
    w_out1 = b_w_out[0].astype(BF16)
    w1 = jnp.concatenate([cmp_w1a, cmp_w1b], axis=-1).astype(BF16)
    w2 = cmp_w2.astype(BF16)
    pe = cmp_pos.reshape(2, 2, 1, CMP_STRIDE * HEAD_DIM)
    pe_hi = pe.astype(BF16)
    pe_lo = (pe - pe_hi.astype(F32)).astype(BF16)
    pe_rows = jnp.concatenate([pe_hi, pe_lo, jnp.zeros((2, 2, SUBLANES - 2, pe.shape[-1]), BF16)], axis=2)
    pe_rows = pe_rows.reshape(2, 2 * SUBLANES, pe.shape[-1])
    b1 = cmp_b1.reshape(2, 1, -1)
    b2 = cmp_b2.reshape(2, 1, HEAD_DIM)

    xp = x_prompt.reshape(s_len, d)
    xs = x_sample.reshape(db * ds, d)
    uz_p = _norm_matmul(xp, norm_g[0], w_in0, 1024)
    uz_s = _norm_matmul(xs, norm_g[0], w_in0, 1024)
    rg = (a_conv_w[0], a_conv_b[0], w_ri, a_b_r[0], a_b_i[0], a_lam[0])
    gated_p, h_p = _rglru_prompt(uz_p, jnp.zeros((SUBLANES, d), F32), jnp.zeros((1, d), F32), *rg)
    conv_p = uz_p[s_len - (CONV_W - 1):, :d].reshape(1, bp, CONV_W - 1, d)
    u_s = uz_s[:, :d].reshape(db, ds, d).transpose(1, 0, 2)
    z_s = uz_s[:, d:].reshape(db, ds, d).transpose(1, 0, 2)
    up_s = jnp.concatenate([state_conv[0].transpose(1, 0, 2), u_s], axis=0)
    gated_s, h_s = _rglru_sample(up_s, z_s, state_h[0], *rg)
    conv_s = up_s[ds:].transpose(1, 0, 2)[None]
    gated_s = gated_s.transpose(1, 0, 2).reshape(db * ds, d)
    ple0 = (w_proj[0], ple_norm_g[0], w_gate[0])
    h1_p = _outproj_ple(gated_p, xp, p_prompt[0].reshape(s_len, -1), w_out0, *ple0)
    h1_s = _outproj_ple(gated_s, xs, p_sample[0].reshape(db * ds, -1), w_out0, *ple0)

    kv_p = _kv_proj(h1_p, kv_norm_g, w_kv_b, g_k_sel, g_k_win)
    kv_s = _kv_proj(h1_s, kv_norm_g, w_kv_b, g_k_sel, g_k_win)
    kc_p, vc_p, ks_p, vs_p, kw_p, vw_p = kv_p[:6]
    kc_s, vc_s, ks_s, vs_s, kw_s, vw_s = kv_s[:6]

    p_pages = s_len // page
    ident = jnp.minimum(jnp.arange(p_pages + 1, dtype=jnp.int32), p_pages - 1)[None]
    cmp_w = (pe_rows, w1, b1, w2, b2, g_k_cmp)
    pieces = lambda x: x.reshape(-1, page // CMP_STRIDE, PIECE_ROWS, HEAD_DIM)
    kcmp_p, vcmp_p = _compress(pieces(kc_p), pieces(vc_p), ident, *cmp_w)
    table_ext = jnp.concatenate([page_table, page_table[:, -1:]], axis=1)
    kcmp_s, vcmp_s = _compress(pieces(cache_k_cmp), pieces(cache_v_cmp), table_ext, *cmp_w)

    proj_p, gl_p = _norm_matmul(h1_p, norm_g[1], w_in1, 1024, w_gl, n=4 * hd)
    proj_s, gl_s = _norm_matmul(h1_s, norm_g[1], w_in1, 1024, w_gl, n=4 * hd)

    n_qt = s_len // TQ
    n_cmp_rows = s_len // CMP_STRIDE
    bank_w = max(TK_SEL, WINDOW + TQ)
    bank_k = min(-(-(_first_constant_distance() + bank_w) // LANES) * LANES, s_len)
    gmax = lambda g: jnp.max(jnp.abs(g))
    k_gain = jnp.maximum(jnp.maximum(gmax(g_k_sel), gmax(g_k_win)), gmax(g_k_cmp))
    score_bound = (1.02 * HEAD_DIM * HEAD_DIM ** -0.5 * gmax(b_g_q[0]) * k_gain + gmax(rel_bias)) * LOG2E
    bounded = score_bound <= MAX_SCORE_BOUND
    shift = jnp.where(bounded, score_bound, 0.0)
    sb, wb, rr = _bias_banks(rel_bias, n_qt, n_cmp_rows // SUBLANES, bank_k, bank_w, shift)
    o_p = _attn_prompt(proj_p, gl_p, b_g_q[0], kcmp_p, vcmp_p, kv_p[6], kv_p[7], kv_p[8], kv_p[9],
                       rr, sb, wb, bounded.astype(jnp.int32).reshape(1), bank_k)

    def head_rows(x):
        return x.reshape(db, ds, N_HEADS, HEAD_DIM).transpose(0, 2, 1, 3).reshape(db, N_HEADS * ds, HEAD_DIM)

    q_s = head_rows(proj_s[:, :hd])
    zz_s = jnp.stack([head_rows(proj_s[:, (1 + br) * hd:(2 + br) * hd]) for br in range(3)], axis=1)
    gl_s = gl_s.reshape(db, ds, N_KV, LANES)[..., :3 * HEADS_PER_GROUP].reshape(db, ds, N_KV, 3, HEADS_PER_GROUP)
    gl_s = gl_s.transpose(0, 3, 2, 4, 1).reshape(db, 3, N_HEADS * ds, 1)
    new_w = LANES
    group_rows = lambda x: x.reshape(db, -1, HEAD_DIM)
    pad_new = lambda x: jnp.pad(group_rows(x), ((0, 0), (0, (new_w - ds) * N_KV), (0, 0)))
    n_cmp_s = past_len // CMP_STRIDE
    bsel = _query_rows_bias(rel_bias, past_len, ds, past_len + new_w).reshape(N_HEADS * ds, -1)
    bwin = bsel[:, past_len - wbuf:]
    bcmp = bsel[:, CMP_BLOCK - 1::CMP_STRIDE][:, :n_cmp_s]
    n_sb_s = past_len // SEL_BLOCK
    ci = jnp.arange(n_cmp_s)[:, None] * CMP_STRIDE
    sj = jnp.arange(n_sb_s)[None, :] * SEL_BLOCK
    overlap = ((ci < sj + SEL_BLOCK) & (ci + CMP_BLOCK > sj) & (ci < past_len - CMP_STRIDE)).T.astype(BF16)
    expand_s = (jnp.arange(past_len)[None, :] // SEL_BLOCK == jnp.arange(n_sb_s)[:, None]).astype(BF16)
    o_s = _attn_sample(q_s, zz_s, gl_s, b_g_q[0], kcmp_s, vcmp_s,
                       cache_k_sel.reshape(n_pool, -1, HEAD_DIM), cache_v_sel.reshape(n_pool, -1, HEAD_DIM),
                       page_table, pad_new(ks_s), pad_new(vs_s), pad_new(kw_s), pad_new(vw_s),
                       group_rows(cache_k_win), group_rows(cache_v_win),
                       bcmp, bsel, bwin, overlap, expand_s, ds)
    o_s = o_s.reshape(db, N_HEADS, ds, HEAD_DIM).transpose(0, 2, 1, 3).reshape(db * ds, hd)

    ple1 = (w_proj[1], ple_norm_g[1], w_gate[1])
    y_p = _outproj_ple(o_p, h1_p, p_prompt[1].reshape(s_len, -1), w_out1, *ple1)
    y_s = _outproj_ple(o_s, h1_s, p_sample[1].reshape(db * ds, -1), w_out1, *ple1)

    rows4 = lambda x, b, t: x.reshape(b, t, N_KV, HEAD_DIM)
    wk_p = min(WINDOW, s_len)
    kwin_s = jnp.concatenate([cache_k_win, rows4(kw_s, db, ds)], axis=1)[:, -WINDOW:]
    vwin_s = jnp.concatenate([cache_v_win, rows4(vw_s, db, ds)], axis=1)[:, -WINDOW:]
    return (y_p.reshape(bp, s_len, d), y_s.reshape(db, ds, d),
            conv_p, h_p.reshape(1, bp, d),
            rows4(kc_p, bp, s_len), rows4(vc_p, bp, s_len), rows4(ks_p, bp, s_len), rows4(vs_p, bp, s_len),
            rows4(kw_p, bp, s_len)[:, -wk_p:], rows4(vw_p, bp, s_len)[:, -wk_p:],
            conv_s, h_s.reshape(1, db, d),
            rows4(kc_s, db, ds), rows4(vc_s, db, ds), rows4(ks_s, db, ds), rows4(vs_s, db, ds),
            kwin_s, vwin_s)
```

```python
import functools
import math

import jax
import jax.numpy as jnp
from jax import lax
from jax.experimental import pallas as pl
from jax.experimental.pallas import tpu as pltpu

F32 = jnp.float32
BF16 = jnp.bfloat16

N_RNN_BLOCKS = 8
CONV_W = 4
LRU_C = 8.0
N_HEADS = 16
HEAD_DIM = 128
N_KV = 4
HEADS_PER_GROUP = N_HEADS // N_KV
GROUP_W = HEADS_PER_GROUP * HEAD_DIM
KV_W = N_KV * HEAD_DIM
CMP_BLOCK = 32
CMP_STRIDE = 16
SEL_BLOCK = 64
N_SELECT = 16
WINDOW = 512
N_BUCKETS = 32
MAX_DISTANCE = 4096
EPS = 1e-6
NEG = -1e30
FORCE = 1e9
REMOVED = -2.0 ** 126
LOG2E = math.log2(math.e)
MAX_SCORE_BOUND = 60.0

LANES = 128
SUBLANES = 8
VMEM_LIMIT = 56 * 1024 * 1024

TQ = 128
TK_SEL = 512
KV_RUN = 8
CMP_PAGES_PER_STEP = 16
SAMPLE_PAGES_PER_STEP = 32


def _cparams(sem):
    return pltpu.CompilerParams(dimension_semantics=sem, vmem_limit_bytes=VMEM_LIMIT)


def _resident(shape, index_map):
    return pl.BlockSpec(shape, index_map, pipeline_mode=pl.Buffered(1))


def _rms_rows(x, g):
    return x * lax.rsqrt(jnp.mean(x * x, axis=-1, keepdims=True) + EPS) * g


def _sigmoid(x):
    return 1.0 / (1.0 + jnp.exp(-x))


def _norm_matmul_kernel(x_ref, g_ref, w_ref, *rest):
    if len(rest) == 4:
        w2_ref, o_ref, o2_ref, xn_ref = rest
    else:
        (o_ref, xn_ref), w2_ref, o2_ref = rest, None, None

    @pl.when(pl.program_id(1) == 0)
    def _():
        xn_ref[...] = _rms_rows(x_ref[...], g_ref[...]).astype(BF16)
        if w2_ref is not None:
            o2_ref[...] = jnp.dot(xn_ref[...], w2_ref[...], preferred_element_type=F32)

    o_ref[...] = jnp.dot(xn_ref[...], w_ref[...], preferred_element_type=F32)


def _norm_matmul(x, g, w, tn, w2=None, n=None):
    m, d = x.shape
    n = w.shape[1] if n is None else n
    tm = min(m, 1024)
    in_specs = [pl.BlockSpec((tm, d), lambda i, j: (i, 0)),
                pl.BlockSpec((1, d), lambda i, j: (0, 0)),
                pl.BlockSpec((d, tn), lambda i, j: (0, j))]
    out_specs = [pl.BlockSpec((tm, tn), lambda i, j: (i, j))]
    out_shape = [jax.ShapeDtypeStruct((m, n), F32)]
    args = [x, g.reshape(1, d), w]
    if w2 is not None:
        in_specs.append(_resident(w2.shape, lambda i, j: (0, 0)))
        out_specs.append(pl.BlockSpec((tm, w2.shape[1]), lambda i, j: (i, 0)))
        out_shape.append(jax.ShapeDtypeStruct((m, w2.shape[1]), F32))
        args.append(w2)
    outs = pl.pallas_call(
        _norm_matmul_kernel,
        grid=(m // tm, n // tn),
        in_specs=in_specs,
        out_specs=out_specs,
        out_shape=out_shape,
        scratch_shapes=[pltpu.VMEM((tm, d), BF16)],
        compiler_params=_cparams(("parallel", "arbitrary")),
        name="norm_matmul",
    )(*args)
    return outs if w2 is not None else outs[0]


def _lru_gates(xc, wri_ref, br_ref, bi_ref, sp_ref, n, bw):
    sl = slice(n * bw, (n + 1) * bw)
    ri = jnp.dot(xc.astype(BF16), wri_ref[n], preferred_element_type=F32)
    r = _sigmoid(ri[:, :bw] + br_ref[:, sl])
    i = _sigmoid(ri[:, bw:] + bi_ref[:, sl])
    log_a = (-LRU_C) * r * sp_ref[:, sl]
    a = jnp.exp(log_a)
    b = jnp.sqrt(1.0 - a * a) * (i * xc)
    return a, b


def _softplus(x):
    return jnp.maximum(x, 0.0) + jnp.log(1.0 + jnp.exp(-jnp.abs(x)))


def _rglru_prompt_kernel(u_ref, z_ref, cinit_ref, hinit_ref, cw_ref, cb_ref, wri_ref, br_ref, bi_ref,
                         lam_ref, g_ref, hlast_ref, ubuf, a_scr, b_scr, hs_scr, hcar, sp_scr):
    tm, d = u_ref.shape
    bw = d // N_RNN_BLOCKS

    @pl.when(pl.program_id(0) == 0)
    def _():
        ubuf[0:SUBLANES, :] = cinit_ref[...]
        hcar[...] = hinit_ref[...]
        sp_scr[...] = _softplus(-lam_ref[...])

    ubuf[SUBLANES:, :] = u_ref[...]
    for n in range(N_RNN_BLOCKS):
        sl = slice(n * bw, (n + 1) * bw)
        up = ubuf[:, sl]
        xc = cb_ref[:, sl] + up * cw_ref[CONV_W - 1:CONV_W, sl]
        for k in range(CONV_W - 1):
            xc = xc + pltpu.roll(up, CONV_W - 1 - k, axis=0) * cw_ref[k:k + 1, sl]
        xc = xc[SUBLANES:]
        a, b = _lru_gates(xc, wri_ref, br_ref, bi_ref, sp_scr, n, bw)
        a_scr[:, sl] = a
        b_scr[:, sl] = b
    ubuf[0:SUBLANES, :] = u_ref[tm - SUBLANES:tm, :]

    def step(t, h):
        h = a_scr[pl.ds(t, 1), :] * h + b_scr[pl.ds(t, 1), :]
        hs_scr[pl.ds(t, 1), :] = h
        return h

    h = lax.fori_loop(0, tm, step, hcar[...], unroll=8)
    hcar[...] = h
    hlast_ref[...] = h
    z = z_ref[...]
    g_ref[...] = (hs_scr[...] * (z * _sigmoid(z))).astype(BF16)


def _rglru_prompt(uz, conv_init, h_init, cw, cb, wri, br, bi, lam):
    t, d2 = uz.shape
    d = d2 // 2
    tm = min(t, 256)
    full = lambda shape: pl.BlockSpec(shape, lambda i: (0,) * len(shape))
    return pl.pallas_call(
        _rglru_prompt_kernel,
        grid=(t // tm,),
        in_specs=[pl.BlockSpec((tm, d), lambda i: (i, 0)),
                  pl.BlockSpec((tm, d), lambda i: (i, 1)),
                  full((SUBLANES, d)), full((1, d)), full((CONV_W, d)), full((1, d)),
                  full(wri.shape), full((1, d)), full((1, d)), full((1, d))],
        out_specs=[pl.BlockSpec((tm, d), lambda i: (i, 0)), full((1, d))],
        out_shape=[jax.ShapeDtypeStruct((t, d), BF16), jax.ShapeDtypeStruct((1, d), F32)],
        scratch_shapes=[pltpu.VMEM((tm + SUBLANES, d), F32), pltpu.VMEM((tm, d), F32),
                        pltpu.VMEM((tm, d), F32), pltpu.VMEM((tm, d), F32),
                        pltpu.VMEM((1, d), F32), pltpu.VMEM((1, d), F32)],
        compiler_params=_cparams(("arbitrary",)),
        name="rglru_prompt",
    )(uz, uz, conv_init, h_init, cw, cb.reshape(1, d), wri, br.reshape(1, d), bi.reshape(1, d),
      lam.reshape(1, d))


def _rglru_sample_kernel(up_ref, z_ref, h0_ref, cw_ref, cb_ref, wri_ref, br_ref, bi_ref, lam_ref,
                         g_ref, hlast_ref, sp_scr):
    n_t = z_ref.shape[0]
    d = z_ref.shape[2]
    bw = d // N_RNN_BLOCKS
    sp_scr[...] = _softplus(-lam_ref[...])
    for n in range(N_RNN_BLOCKS):
        sl = slice(n * bw, (n + 1) * bw)
        h = h0_ref[:, sl]
        for t in range(n_t):
            xc = cb_ref[:, sl]
            for k in range(CONV_W):
                xc = xc + up_ref[t + k, :, sl] * cw_ref[k:k + 1, sl]
            a, b = _lru_gates(xc, wri_ref, br_ref, bi_ref, sp_scr, n, bw)
            h = a * h + b
            z = z_ref[t, :, sl]
            g_ref[t, :, sl] = (h * (z * _sigmoid(z))).astype(BF16)
        hlast_ref[:, sl] = h


def _rglru_sample(up, z, h0, cw, cb, wri, br, bi, lam):
    n_t, nb, d = z.shape
    return pl.pallas_call(
        _rglru_sample_kernel,
        out_shape=[jax.ShapeDtypeStruct((n_t, nb, d), BF16), jax.ShapeDtypeStruct((nb, d), F32)],
        scratch_shapes=[pltpu.VMEM((1, d), F32)],
        compiler_params=pltpu.CompilerParams(vmem_limit_bytes=VMEM_LIMIT),
        name="rglru_sample",
    )(up, z, h0, cw, cb.reshape(1, d), wri, br.reshape(1, d), bi.reshape(1, d), lam.reshape(1, d))


def _outproj_ple_kernel(a_ref, res_ref, p_ref, wo_ref, wp_ref, gn_ref, wg_ref, o_ref, h_scr, hn_scr):
    tm, d = res_ref.shape
    ch = min(d, 512)
    a = a_ref[...]
    ssq = jnp.zeros((tm, 1), F32)
    for c in range(d // ch):
        sl = slice(c * ch, (c + 1) * ch)
        h = res_ref[:, sl] + jnp.dot(a, wo_ref[:, sl], preferred_element_type=F32)
        h_scr[:, sl] = h
        ssq = ssq + jnp.sum(h * h, axis=-1, keepdims=True)
    inv = lax.rsqrt(ssq * (1.0 / d) + EPS)
    for c in range(d // ch):
        sl = slice(c * ch, (c + 1) * ch)
        hn_scr[:, sl] = (h_scr[:, sl] * inv * gn_ref[:, sl]).astype(BF16)
    pb = p_ref[...].astype(BF16)
    hn = hn_scr[...]
    for c in range(d // ch):
        sl = slice(c * ch, (c + 1) * ch)
        gate = _sigmoid(jnp.dot(hn, wg_ref[:, sl], preferred_element_type=F32))
        pp = jnp.dot(pb, wp_ref[:, sl], preferred_element_type=F32)
        o_ref[:, sl] = h_scr[:, sl] + pp * gate


def _outproj_ple(a, res, p, w_out, w_proj, g_norm, w_gate):
    m, d = res.shape
    pd = p.shape[1]
    tm = min(m, 512)
    return pl.pallas_call(
        _outproj_ple_kernel,
        grid=(m // tm,),
        in_specs=[pl.BlockSpec((tm, a.shape[1]), lambda i: (i, 0)),
                  pl.BlockSpec((tm, d), lambda i: (i, 0)),
                  pl.BlockSpec((tm, pd), lambda i: (i, 0)),
                  _resident(w_out.shape, lambda i: (0, 0)),
                  _resident(w_proj.shape, lambda i: (0, 0)),
                  _resident((1, d), lambda i: (0, 0)),
                  _resident(w_gate.shape, lambda i: (0, 0))],
        out_specs=pl.BlockSpec((tm, d), lambda i: (i, 0)),
        out_shape=jax.ShapeDtypeStruct((m, d), F32),
        scratch_shapes=[pltpu.VMEM((tm, d), F32), pltpu.VMEM((tm, d), BF16)],
        compiler_params=_cparams(("parallel",)),
        name="outproj_ple",
    )(a, res, p, w_out, w_proj, g_norm.reshape(1, d), w_gate)


AUG_W = 2 * HEAD_DIM
MASK_OFF = -2.0 ** 60


def _kv_proj_kernel(x_ref, g_ref, w_ref, gsel_ref, gwin_ref,
                    kc_ref, vc_ref, ks_ref, vs_ref, kw_ref, vw_ref, ksa_ref, vsa_ref, kwb_ref, vwa_ref):
    tm = x_ref.shape[0]
    xn = _rms_rows(x_ref[...], g_ref[...]).astype(BF16)
    outs = (kc_ref, vc_ref, ks_ref, vs_ref, kw_ref, vw_ref)
    head_gain = {2: gsel_ref, 4: gwin_ref}
    pos = pl.program_id(0) * tm + lax.broadcasted_iota(jnp.int32, (tm, HEAD_DIM), 0)
    blk = lax.broadcasted_iota(jnp.int32, (tm, HEAD_DIM), 1)
    block_mask = jnp.where(blk == pos // SEL_BLOCK, MASK_OFF, 0.0).astype(BF16)
    ones = jnp.ones((tm, HEAD_DIM), BF16)
    for s in range(6):
        y = jnp.dot(xn, w_ref[:, s * KV_W:(s + 1) * KV_W], preferred_element_type=F32)
        for g in range(N_KV):
            yg = y[:, g * HEAD_DIM:(g + 1) * HEAD_DIM]
            if s in head_gain:
                yg = _rms_rows(yg, head_gain[s][...])
            outs[s][pl.ds(g, tm, stride=N_KV), :] = yg
            lo = slice(g * AUG_W, g * AUG_W + HEAD_DIM)
            hi = slice(g * AUG_W + HEAD_DIM, (g + 1) * AUG_W)
            if s == 2:
                ksa_ref[:, lo] = yg.astype(BF16)
                ksa_ref[:, hi] = block_mask
            elif s == 3:
                vsa_ref[:, lo] = yg.astype(BF16)
                vsa_ref[:, hi] = ones
            elif s == 4:
                kwb_ref[:, g * HEAD_DIM:(g + 1) * HEAD_DIM] = yg.astype(BF16)
            elif s == 5:
                vwa_ref[:, lo] = yg.astype(BF16)
                vwa_ref[:, hi] = ones


def _kv_proj(x, g, w, g_sel, g_win):
    m, d = x.shape
    tm = min(m, 512)
    row = lambda i: (i, 0)
    aug = N_KV * AUG_W
    return pl.pallas_call(
        _kv_proj_kernel,
        grid=(m // tm,),
        in_specs=[pl.BlockSpec((tm, d), row), _resident((1, d), lambda i: (0, 0)),
                  _resident(w.shape, lambda i: (0, 0)),
                  _resident((1, HEAD_DIM), lambda i: (0, 0)), _resident((1, HEAD_DIM), lambda i: (0, 0))],
        out_specs=([pl.BlockSpec((tm * N_KV, HEAD_DIM), row)] * 6
                   + [pl.BlockSpec((tm, aug), row), pl.BlockSpec((tm, aug), row),
                      pl.BlockSpec((tm, KV_W), row), pl.BlockSpec((tm, aug), row)]),
        out_shape=([jax.ShapeDtypeStruct((m * N_KV, HEAD_DIM), F32)] * 6
                   + [jax.ShapeDtypeStruct((m, aug), BF16), jax.ShapeDtypeStruct((m, aug), BF16),
                      jax.ShapeDtypeStruct((m, KV_W), BF16), jax.ShapeDtypeStruct((m, aug), BF16)]),
        compiler_params=_cparams(("parallel",)),
        name="kv_proj",
    )(x, g.reshape(1, d), w, g_sel.reshape(1, HEAD_DIM), g_win.reshape(1, HEAD_DIM))


PIECE_ROWS = CMP_STRIDE * N_KV
PIECE_PITCH = PIECE_ROWS + SUBLANES


def _piece_rows(buf, r, n):
    flat = buf.reshape(buf.shape[0] * PIECE_PITCH, HEAD_DIM)
    return flat[pl.ds(r, n, stride=PIECE_PITCH), :]


def _compress_kernel(pt_ref, kc_hbm, vc_hbm, pe_ref, w1_ref, b1_ref, w2_ref, b2_ref, gk_ref,
                     kcmp_ref, vcmp_ref, kbuf, vbuf, sem, lhs, bias_scr, *, n_quarters, pages_per_step):
    s = pl.program_id(0)
    n_steps = pl.num_programs(0)
    pieces_per_page = kc_hbm.shape[1]
    n_piece = pages_per_step * pieces_per_page
    n_load = n_piece + SUBLANES
    rows_g = n_piece + 2 * SUBLANES
    hidden = w2_ref.shape[1]

    def copies(step, slot):
        b = step // n_quarters
        q = step % n_quarters
        out = []
        for hbm, buf, which in ((kc_hbm, kbuf, 0), (vc_hbm, vbuf, 1)):
            for p in range(pages_per_step):
                pg = pt_ref[b, q * pages_per_step + p]
                dst = buf.at[slot, pl.ds(p * pieces_per_page, pieces_per_page), pl.ds(0, PIECE_ROWS), :]
                out.append(pltpu.make_async_copy(hbm.at[pg], dst, sem.at[slot, which]))
            pg = pt_ref[b, q * pages_per_step + pages_per_step]
            out.append(pltpu.make_async_copy(hbm.at[pg, 0], buf.at[slot, n_piece, pl.ds(0, PIECE_ROWS), :],
                                             sem.at[slot, which]))
        return out

    slot = s % 2

    @pl.when(s == 0)
    def _():
        for c in copies(0, 0):
            c.start()
        kbuf[:, n_piece + 1:] = jnp.zeros((2, SUBLANES - 1) + kbuf.shape[2:], F32)
        vbuf[:, n_piece + 1:] = jnp.zeros((2, SUBLANES - 1) + vbuf.shape[2:], F32)
        lhs[...] = jnp.zeros(lhs.shape, BF16)
        for which in range(2):
            pw = jnp.dot(pe_ref[which], w1_ref[which], preferred_element_type=F32)
            bias_scr[which] = (jnp.sum(pw[:SUBLANES, :hidden], axis=0, keepdims=True)
                               + jnp.sum(pw[SUBLANES:, hidden:], axis=0, keepdims=True) + b1_ref[which])

    @pl.when(s + 1 < n_steps)
    def _():
        for c in copies(s + 1, 1 - slot):
            c.start()

    for c in copies(s, slot):
        c.wait()

    for which, buf, out_ref in ((0, kbuf, kcmp_ref), (1, vbuf, vcmp_ref)):
        for g in range(N_KV):
            for pos in range(CMP_STRIDE):
                x = _piece_rows(buf.at[slot], pos * N_KV + g, n_load)
                lhs[which, g * rows_g:g * rows_g + n_load, pos * HEAD_DIM:(pos + 1) * HEAD_DIM] = x.astype(BF16)
        hab = jnp.dot(lhs[which], w1_ref[which], preferred_element_type=F32)
        for g in range(N_KV):
            blk = hab[g * rows_g:(g + 1) * rows_g]
            nxt = pltpu.roll(blk[:, hidden:], rows_g - 1, axis=0)
            hid = blk[:n_piece, :hidden] + nxt[:n_piece] + bias_scr[which]
            hid = hid * _sigmoid(hid)
            y = jnp.dot(hid.astype(BF16), w2_ref[which], preferred_element_type=F32) + b2_ref[which]
            if which == 0:
                y = _rms_rows(y, gk_ref[...])
            out_ref[0, g] = y.astype(BF16)


def _compress(kc_pages, vc_pages, table, pe_rows, w1, b1, w2, b2, g_k):
    nb, n_pages = table.shape[0], table.shape[1] - 1
    ppp = kc_pages.shape[1]
    pps = min(CMP_PAGES_PER_STEP, n_pages)
    n_quarters = n_pages // pps
    n_piece = pps * ppp
    kdim = CMP_STRIDE * HEAD_DIM
    hidden = w2.shape[1]
    full = lambda shape: pl.BlockSpec(shape, lambda s, pt: (0,) * len(shape))
    out_spec = pl.BlockSpec((1, N_KV, n_piece, HEAD_DIM), lambda s, pt: (s // n_quarters, 0, s % n_quarters, 0))
    out_sd = jax.ShapeDtypeStruct((nb, N_KV, n_pages * ppp, HEAD_DIM), BF16)
    buf = pltpu.VMEM((2, n_piece + SUBLANES, PIECE_PITCH, HEAD_DIM), F32)
    lhs = pltpu.VMEM((2, N_KV * (n_piece + 2 * SUBLANES), kdim), BF16)
    grid_spec = pltpu.PrefetchScalarGridSpec(
        num_scalar_prefetch=1,
        grid=(nb * n_quarters,),
        in_specs=[pl.BlockSpec(memory_space=pl.ANY), pl.BlockSpec(memory_space=pl.ANY),
                  full(pe_rows.shape), full(w1.shape), full(b1.shape), full(w2.shape), full(b2.shape),
                  full((1, HEAD_DIM))],
        out_specs=[out_spec, out_spec],
        scratch_shapes=[buf, buf, pltpu.SemaphoreType.DMA((2, 2)), lhs, pltpu.VMEM((2, 1, hidden), F32)],
    )
    return pl.pallas_call(
        functools.partial(_compress_kernel, n_quarters=n_quarters, pages_per_step=pps),
        grid_spec=grid_spec,
        out_shape=[out_sd, out_sd],
        compiler_params=_cparams(("arbitrary",)),
        name="compress",
    )(table, kc_pages, vc_pages, pe_rows, w1, b1, w2, b2, g_k.reshape(1, HEAD_DIM))


def _rel_bucket(dist):
    n = jnp.maximum(dist, 0)
    max_exact = N_BUCKETS // 2
    nf = jnp.maximum(n, 1).astype(F32)
    large = max_exact + (jnp.log(nf / max_exact) / math.log(MAX_DISTANCE / max_exact)
                         * (N_BUCKETS - max_exact)).astype(jnp.int32)
    large = jnp.minimum(large, N_BUCKETS - 1)
    return jnp.where(n < max_exact, n, large) & (N_BUCKETS - 1)


def _bias_table(rel_bias, dist, masked, gain=1.0):
    dist = lax.optimization_barrier(dist)
    onehot = (_rel_bucket(dist)[None, :] == jnp.arange(N_BUCKETS)[:, None]).astype(F32)
    table = jnp.dot(rel_bias.astype(F32).T * gain, onehot, precision=lax.Precision.HIGHEST)
    return jnp.where(dist[None, :] >= 0, table, NEG) if masked else table


def _bias_bank_kernel(gs_ref, gw_ref, g2_ref, sb_ref, wb_ref, rr_ref):
    for g_ref, o_ref in ((gs_ref, sb_ref), (gw_ref, wb_ref)):
        wp = g_ref.shape[2]
        x = jnp.broadcast_to(g_ref[0], (TQ, wp))
        o_ref[0] = pltpu.roll(x, wp - (TQ - 1), 1, stride=1, stride_axis=0)[:, :wp - TQ]

    n_blocks = g2_ref.shape[1]
    for il in range(SUBLANES):
        left = CMP_STRIDE * (SUBLANES - 1 - il)
        y = pltpu.roll(g2_ref[0], (2 * TQ - left) % (2 * TQ), 1)
        rr_ref[0, pl.ds(il, n_blocks, stride=SUBLANES), :] = y[:, :TQ]


def _bias_banks(rel_bias, n_qt, n_row_blocks, bank_k, bank_w, shift):
    nh = rel_bias.shape[1]
    wk = WINDOW + TQ
    gs = _bias_table(rel_bias, bank_k + TQ - 1 - jnp.arange(bank_k + bank_w + TQ), True, LOG2E)
    dw = WINDOW + TQ - 1 - jnp.arange(WINDOW + wk + TQ)
    gw = _bias_table(rel_bias, jnp.where(dw < WINDOW, dw, -1), True, LOG2E)
    n_blocks = -(-(n_qt + n_row_blocks - 1) // SUBLANES) * SUBLANES
    d2 = (TQ * (n_qt - 1 - jnp.arange(n_blocks))[:, None] + jnp.arange(2 * TQ)[None, :]
          - (CMP_BLOCK - 1) - CMP_STRIDE * (SUBLANES - 1))
    g2 = _bias_table(rel_bias, d2.reshape(-1), True, LOG2E).reshape(nh, n_blocks, 2 * TQ)
    spec3 = lambda a: pl.BlockSpec((1,) + a.shape[1:], lambda h: (h, 0, 0))
    outs = [jax.ShapeDtypeStruct((nh, TQ, bank_k + bank_w), F32), jax.ShapeDtypeStruct((nh, TQ, WINDOW + wk), F32),
            jax.ShapeDtypeStruct((nh, n_blocks * SUBLANES, TQ), F32)]
    gs, gw, g2 = gs[:, None] - shift, gw[:, None] - shift, g2 - shift
    return pl.pallas_call(
        _bias_bank_kernel,
        grid=(nh,),
        in_specs=[spec3(gs), spec3(gw), spec3(g2)],
        out_specs=[spec3(o) for o in outs],
        out_shape=outs,
        compiler_params=_cparams(("parallel",)),
        name="bias_banks",
    )(gs, gw, g2)


def _query_rows_bias(rel_bias, t0, n_tok, n_keys):
    dist = (t0 + jnp.arange(n_tok)[:, None] - jnp.arange(n_keys)[None, :]).reshape(-1)
    return _bias_table(rel_bias, dist, False).reshape(-1, n_tok, n_keys)


def _first_constant_distance():
    ratio = MAX_DISTANCE / (N_BUCKETS // 2)
    return int(math.ceil((N_BUCKETS // 2) * ratio ** ((N_BUCKETS // 2 - 1) / (N_BUCKETS // 2)))) + 2


def _topk_rows(score, blk, k_top):
    n_blk = score.shape[0]
    for _ in range(k_top):
        m = jnp.max(score, axis=0, keepdims=True)
        idx = jnp.min(jnp.where(score == m, blk, n_blk), axis=0, keepdims=True)
        score = jnp.where(blk == idx, REMOVED, score)
    return score == REMOVED


def _attn_prompt_kernel(q_ref, z0_ref, z1_ref, z2_ref, gl_ref, gq_ref, kcmp_ref, vcmp_ref,
                        ksa_ref, vsa_ref, kw_ref, vwa_ref, rr_ref, sb_ref, wb_ref, bounded_ref, o_ref,
                        q_scr, psum_scr, ocmp_scr, owin_scr, m_scr, acc_scr, s0_scr, s1_scr, *, n_qtiles, bank_k):
    qi = pl.program_id(1)
    q0 = qi * TQ
    n_cmp_rows = kcmp_ref.shape[1]
    n_sb = n_cmp_rows * CMP_STRIDE // SEL_BLOCK
    hg = HEADS_PER_GROUP
    q_gain = HEAD_DIM ** -0.5 * LOG2E
    wk = WINDOW + TQ

    def per_tile_branches(t, bounded):
        tile = qi + t
        t0 = tile * TQ
        for h in range(hg):
            qh = _rms_rows(q_ref[t * TQ:(t + 1) * TQ, h * HEAD_DIM:(h + 1) * HEAD_DIM], gq_ref[...]) * q_gain
            q_scr[t, h * TQ:(h + 1) * TQ, 0:HEAD_DIM] = qh.astype(BF16)
        q_all = q_scr[t, :, 0:HEAD_DIM]

        s_t = lax.dot_general(kcmp_ref[0], q_all, (((1,), (1,)), ((), ())), preferred_element_type=F32)
        rr_start = pl.multiple_of((n_qtiles - 1 - tile) * SUBLANES, SUBLANES)
        psum = jnp.zeros((n_cmp_rows, TQ), F32)
        for h in range(hg):
            sh = s_t[:, h * TQ:(h + 1) * TQ] + rr_ref[h, pl.ds(rr_start, n_cmp_rows), :]
            if bounded:
                e = jnp.exp2(sh)
            else:
                e = jnp.exp2(sh - jnp.maximum(jnp.max(sh, axis=0, keepdims=True), 0.5 * NEG))
            den = jnp.sum(e, axis=0, keepdims=True)
            p = e * (1.0 / jnp.where(den > 0.0, den, 1.0))
            psum = psum + p
            ocmp_scr[t, h * TQ:(h + 1) * TQ, :] = lax.dot_general(
                p.astype(BF16), vcmp_ref[0], (((0,), (0,)), ((), ())), preferred_element_type=F32)

        w0 = pl.multiple_of(jnp.maximum(t0 - WINDOW, 0), LANES)
        s_w = lax.dot_general(q_all, kw_ref[pl.ds(w0, wk), :], (((1,), (1,)), ((), ())),
                              preferred_element_type=F32)
        vwin = vwa_ref[pl.ds(w0, wk), :]
        wb_start = pl.multiple_of(WINDOW - (t0 - w0), LANES)
        for h in range(hg):
            sh = s_w[h * TQ:(h + 1) * TQ] + wb_ref[h, :, pl.ds(wb_start, wk)]
            e = jnp.exp2(sh if bounded else sh - jnp.max(sh, axis=1, keepdims=True))
            o_aug = jnp.dot(e.astype(BF16), vwin, preferred_element_type=F32)
            owin_scr[t, h * TQ:(h + 1) * TQ, :] = o_aug[:, 0:HEAD_DIM] / o_aug[:, HEAD_DIM:AUG_W]

        psum_scr[t, 0:SUBLANES, :] = jnp.zeros((SUBLANES, TQ), F32)
        psum_scr[t, SUBLANES:, :] = psum
        ratio = SEL_BLOCK // CMP_STRIDE
        imp = psum_scr[t, pl.ds(SUBLANES - 1, n_sb, stride=ratio), :]
        for k in range(ratio):
            imp = imp + psum_scr[t, pl.ds(SUBLANES + k, n_sb, stride=ratio), :]
        blk = lax.broadcasted_iota(jnp.int32, (n_sb, 1), 0)
        cur = (t0 + lax.broadcasted_iota(jnp.int32, (1, TQ), 1)) // SEL_BLOCK
        blk_ok = blk <= cur
        forced = (blk == 0) | (blk == cur) | (blk == cur - 1)
        k_top = min(N_SELECT, n_sb)
        score = jnp.where(blk_ok, jnp.where(forced, REMOVED, imp), -FORCE)

        def pick_round(sc, active=None):
            m = jnp.max(sc, axis=0, keepdims=True)
            idx = jnp.min(jnp.where(sc == m, blk, n_sb), axis=0, keepdims=True)
            hit = blk == idx if active is None else (blk == idx) & active
            return jnp.where(hit, REMOVED, sc)

        for _ in range(k_top - 3):
            score = pick_round(score)
        psum_scr[t, 0:n_sb, :] = score

        @pl.when(tile == 0)
        def _():
            n_forced = jnp.sum(jnp.where(forced & blk_ok, 1, 0), axis=0, keepdims=True)
            sc = psum_scr[t, 0:n_sb, :]
            for r in range(k_top - 3, k_top - 1):
                sc = pick_round(sc, r < k_top - n_forced)
            psum_scr[t, 0:n_sb, :] = sc

        sel_t = (psum_scr[t, 0:n_sb, :] == REMOVED) & blk_ok
        not_sel = jnp.where(sel_t, 0.0, 1.0).T
        if n_sb < HEAD_DIM:
            not_sel = jnp.concatenate([not_sel, jnp.zeros((TQ, HEAD_DIM - n_sb), F32)], axis=1)
        for h in range(hg):
            q_scr[t, h * TQ:(h + 1) * TQ, HEAD_DIM:AUG_W] = not_sel.astype(BF16)

    half = 0
    n_kv = (q0 + TQ + TK_SEL - 1) // TK_SEL
    rep = TK_SEL // LANES

    def scores(j, s_ref):
        k0 = pl.multiple_of(j * TK_SEL, TK_SEL)
        s_ref[...] = lax.dot_general(q_scr[half], ksa_ref[pl.ds(k0, TK_SEL), :], (((1,), (1,)), ((), ())),
                                     preferred_element_type=F32)

    def kv_tile(j, s_ref, bounded):
        k0 = pl.multiple_of(j * TK_SEL, TK_SEL)
        vt = vsa_ref[pl.ds(k0, TK_SEL), :]
        b_start = pl.multiple_of(jnp.maximum(bank_k - (q0 - k0), 0), LANES)
        for h in range(hg):
            rows = slice(h * TQ, (h + 1) * TQ)
            sh = s_ref[rows] + sb_ref[h, :, pl.ds(b_start, TK_SEL)]
            if bounded:
                acc_scr[rows] += jnp.dot(jnp.exp2(sh).astype(BF16), vt, preferred_element_type=F32)
            else:
                m_prev = m_scr[rows]
                m_next = jnp.maximum(m_prev, jnp.max(sh, axis=1, keepdims=True))
                alpha = jnp.exp2(m_prev - m_next)
                e = jnp.exp2(sh - jnp.tile(m_next, (1, rep)))
                acc_scr[rows] = (jnp.tile(alpha, (1, AUG_W // LANES)) * acc_scr[rows]
                                 + jnp.dot(e.astype(BF16), vt, preferred_element_type=F32))
                m_scr[rows] = m_next

    def selected_branch(bounded):
        m_scr[...] = jnp.full(m_scr.shape, NEG, F32)
        acc_scr[...] = jnp.zeros(acc_scr.shape, F32)

        if not bounded:
            def one_tile(j, carry):
                scores(j, s0_scr)
                kv_tile(j, s0_scr, False)
                return carry

            lax.fori_loop(0, n_kv, one_tile, 0)
            return

        def run(first, count):
            bufs = (s0_scr, s1_scr)
            for k in range(count):
                scores(first + k + 1, bufs[(k + 1) % 2])
                kv_tile(first + k, bufs[k % 2], bounded)

        def kv_run(i, carry):
            run(KV_RUN * i, KV_RUN)
            return carry

        scores(0, s0_scr)
        n_before = n_kv - 1
        lax.fori_loop(0, n_before // KV_RUN, kv_run, 0)
        size = KV_RUN // 2
        while size >= 2:
            start = n_before // (2 * size) * (2 * size)

            @pl.when(n_before - start >= size)
            def _(start=start, size=size):
                run(start, size)

            size //= 2

        done2 = n_before // 2 * 2

        @pl.when(n_before > done2)
        def _():
            scores(done2 + 1, s1_scr)
            kv_tile(done2, s0_scr, bounded)
            kv_tile(done2 + 1, s1_scr, bounded)

        @pl.when(n_before == done2)
        def _():
            kv_tile(done2, s0_scr, bounded)

    for bounded in (True, False):
        @pl.when(bounded_ref[0] == int(bounded))
        def _(bounded=bounded):
            per_tile_branches(0, bounded)
            selected_branch(bounded)

    sig_gl = _sigmoid(gl_ref[...])
    z_refs = (z0_ref, z1_ref, z2_ref)
    for h in range(hg):
        rows = slice(h * TQ, (h + 1) * TQ)
        cols = slice(h * HEAD_DIM, (h + 1) * HEAD_DIM)
        o_sel = acc_scr[rows, 0:HEAD_DIM] / acc_scr[rows, HEAD_DIM:AUG_W]
        out = jnp.zeros((TQ, HEAD_DIM), F32)
        for br, o_br in enumerate((ocmp_scr[half, rows, :], o_sel, owin_scr[half, rows, :])):
            z = z_refs[br][:, cols]
            c = br * hg + h
            out = out + sig_gl[:, c:c + 1] * o_br * (z * _sigmoid(z))
        o_ref[:, cols] = out.astype(BF16)


def _attn_prompt(proj, gl_g, g_q, kcmp, vcmp, ks_aug, vs_aug, kw_b, vw_aug, rr, sb, wb, bounded, bank_k):
    s_len = proj.shape[0]
    n_qt = s_len // TQ
    n_cmp_rows = kcmp.shape[2]
    gcols = N_HEADS * HEAD_DIM // GROUP_W
    qz = lambda seg: pl.BlockSpec((TQ, GROUP_W), lambda g, i, seg=seg: (i, seg * gcols + g))
    per_group = lambda shape: _resident((1,) + shape, lambda g, i: (g,) + (0,) * len(shape))
    kv_spec = lambda width: _resident((s_len, width), lambda g, i: (0, g))
    return pl.pallas_call(
        functools.partial(_attn_prompt_kernel, n_qtiles=n_qt, bank_k=bank_k),
        grid=(N_KV, n_qt),
        in_specs=[qz(0), qz(1), qz(2), qz(3),
                  pl.BlockSpec((TQ, LANES), lambda g, i: (i, g)),
                  _resident((1, HEAD_DIM), lambda g, i: (0, 0)),
                  per_group((n_cmp_rows, HEAD_DIM)), per_group((n_cmp_rows, HEAD_DIM)),
                  kv_spec(AUG_W), kv_spec(AUG_W), kv_spec(HEAD_DIM), kv_spec(AUG_W),
                  _resident((HEADS_PER_GROUP,) + rr.shape[1:], lambda g, i: (g, 0, 0)),
                  _resident((HEADS_PER_GROUP,) + sb.shape[1:], lambda g, i: (g, 0, 0)),
                  _resident((HEADS_PER_GROUP,) + wb.shape[1:], lambda g, i: (g, 0, 0)),
                  pl.BlockSpec(memory_space=pltpu.SMEM)],
        out_specs=pl.BlockSpec((TQ, GROUP_W), lambda g, i: (i, g)),
        out_shape=jax.ShapeDtypeStruct((s_len, N_HEADS * HEAD_DIM), BF16),
        scratch_shapes=[pltpu.VMEM((1, HEADS_PER_GROUP * TQ, AUG_W), BF16),
                        pltpu.VMEM((1, n_cmp_rows + SUBLANES, TQ), F32),
                        pltpu.VMEM((1, HEADS_PER_GROUP * TQ, HEAD_DIM), F32),
                        pltpu.VMEM((1, HEADS_PER_GROUP * TQ, HEAD_DIM), F32),
                        pltpu.VMEM((HEADS_PER_GROUP * TQ, LANES), F32),
                        pltpu.VMEM((HEADS_PER_GROUP * TQ, AUG_W), F32),
                        pltpu.VMEM((HEADS_PER_GROUP * TQ, TK_SEL), F32),
                        pltpu.VMEM((HEADS_PER_GROUP * TQ, TK_SEL), F32)],
        compiler_params=_cparams(("arbitrary", "arbitrary")),
        name="attn_prompt",
    )(proj, proj, proj, proj, gl_g, g_q.reshape(1, HEAD_DIM), kcmp[0], vcmp[0],
      ks_aug, vs_aug, kw_b, vw_aug, rr, sb, wb, bounded)


def _attn_sample_kernel(*refs, n_tok, pages_per_chunk, page, past_len):
    pt_ref = refs[0]
    q_ref, z_ref, gl_ref, gq_ref, kcmp_ref, vcmp_ref = refs[1:7]
    kpages = refs[7:7 + pages_per_chunk]
    vpages = refs[7 + pages_per_chunk:7 + 2 * pages_per_chunk]
    (ksn_ref, vsn_ref, kwn_ref, vwn_ref, kwc_ref, vwc_ref, bcmp_ref, bsel_ref, bwin_ref, ov_ref, ex_ref,
     o_ref, sel_scr, m_scr, l_scr, acc_scr, q_scr, ocmp_scr) = refs[7 + 2 * pages_per_chunk:]
    del pt_ref
    c = pl.program_id(1)
    n_chunks = pl.num_programs(1)
    hg = HEADS_PER_GROUP
    nr = hg * n_tok
    scale = HEAD_DIM ** -0.5
    n_cmp_rows = kcmp_ref.shape[2]
    n_sb = ov_ref.shape[0]
    chunk = pages_per_chunk * page
    tok = lax.broadcasted_iota(jnp.int32, (N_KV * nr, 1), 0) % n_tok

    @pl.when(c == 0)
    def _():
        q_scr[...] = _rms_rows(q_ref[0], gq_ref[...]).astype(BF16)
        m_scr[...] = jnp.full(m_scr.shape, NEG, F32)
        l_scr[...] = jnp.zeros(l_scr.shape, F32)
        acc_scr[...] = jnp.zeros(acc_scr.shape, F32)
        ci = lax.broadcasted_iota(jnp.int32, (1, n_cmp_rows), 1)
        cmp_valid = ci < n_cmp_rows - 1
        s = jnp.concatenate(
            [lax.dot_general(q_scr[g * nr:(g + 1) * nr], kcmp_ref[0, g], (((1,), (1,)), ((), ())),
                             preferred_element_type=F32) for g in range(N_KV)], axis=0)
        s = jnp.where(cmp_valid, s * scale + bcmp_ref[...], NEG)
        e = jnp.where(cmp_valid, jnp.exp(s - jnp.max(s, axis=1, keepdims=True)), 0.0)
        p = e / jnp.sum(e, axis=1, keepdims=True)
        pb = p.astype(BF16)
        psums = []
        for g in range(N_KV):
            rows = slice(g * nr, (g + 1) * nr)
            ocmp_scr[rows] = jnp.dot(pb[rows], vcmp_ref[0, g], preferred_element_type=F32)
            psum = p[g * nr:g * nr + n_tok]
            for h in range(1, hg):
                psum = psum + p[g * nr + h * n_tok:g * nr + (h + 1) * n_tok]
            psums.append(psum)
        rest = jnp.concatenate(psums, axis=0)
        imp_t = jnp.zeros((n_sb, N_KV * n_tok), F32)
        for _ in range(3):
            piece = rest.astype(BF16)
            rest = rest - piece.astype(F32)
            imp_t = imp_t + lax.dot_general(ov_ref[...], piece, (((1,), (1,)), ((), ())),
                                            preferred_element_type=F32)
        blk = lax.broadcasted_iota(jnp.int32, (n_sb, 1), 0)
        forced = (blk == 0) | (blk == n_sb - 1)
        score = jnp.where(forced, FORCE, imp_t)
        sel_t = _topk_rows(score, blk, min(N_SELECT, n_sb + 1) - 1)
        sel_f = jnp.where(sel_t, 1.0, 0.0)
        if N_KV * n_tok < LANES:
            sel_f = jnp.concatenate([sel_f, jnp.zeros((n_sb, LANES - N_KV * n_tok), F32)], axis=1)
        sel_f = sel_f.T
        for g in range(N_KV):
            sel_scr[g * nr:(g + 1) * nr] = jnp.tile(sel_f[g * n_tok:(g + 1) * n_tok], (hg, 1)).astype(BF16)

    def group_rows(ref, g):
        return ref[0, pl.ds(g, ref.shape[1] // N_KV, stride=N_KV), :].astype(BF16)

    def all_scores(k_of_group):
        return jnp.concatenate(
            [lax.dot_general(q_scr[g * nr:(g + 1) * nr], k_of_group(g), (((1,), (1,)), ((), ())),
                             preferred_element_type=F32) for g in range(N_KV)], axis=0)

    def all_values(e, v_of_group):
        eb = e.astype(BF16)
        return jnp.concatenate(
            [jnp.dot(eb[g * nr:(g + 1) * nr], v_of_group(g), preferred_element_type=F32) for g in range(N_KV)],
            axis=0)

    def online_update(sh, v_of_group):
        m_prev = m_scr[...]
        m_next = jnp.maximum(m_prev, jnp.max(sh, axis=1, keepdims=True))
        alpha = jnp.exp(m_prev - m_next)
        e = jnp.exp(sh - m_next[:, 0:1])
        l_scr[...] = alpha * l_scr[...] + jnp.sum(e, axis=1, keepdims=True)
        acc_scr[...] = alpha * acc_scr[...] + all_values(e, v_of_group)
        m_scr[...] = m_next

    k0 = pl.multiple_of(c * chunk, chunk)
    s = all_scores(lambda g: jnp.concatenate([group_rows(r, g) for r in kpages], axis=0))
    picked = jnp.dot(sel_scr[...], ex_ref[:, pl.ds(k0, chunk)], preferred_element_type=F32)
    sh = jnp.where(picked > 0.5, s * scale + bsel_ref[:, pl.ds(k0, chunk)], NEG)
    online_update(sh, lambda g: jnp.concatenate([group_rows(r, g) for r in vpages], axis=0))

    @pl.when(c == n_chunks - 1)
    def _():
        new_w = ksn_ref.shape[1] // N_KV
        jn = lax.broadcasted_iota(jnp.int32, (1, new_w), 1)
        new_valid = jn <= tok
        wbuf = kwc_ref.shape[1] // N_KV
        jw = lax.broadcasted_iota(jnp.int32, (1, wbuf), 1)
        dist_c = (past_len + tok) - (past_len - wbuf + jw)
        win_valid_c = dist_c < WINDOW
        s = all_scores(lambda g: group_rows(ksn_ref, g))
        sh = jnp.where(new_valid, s * scale + bsel_ref[:, pl.ds(past_len, new_w)], NEG)
        online_update(sh, lambda g: group_rows(vsn_ref, g))
        o_sel = acc_scr[...] / l_scr[...]
        s_c = all_scores(lambda g: group_rows(kwc_ref, g))
        s_n = all_scores(lambda g: group_rows(kwn_ref, g))
        s_c = jnp.where(win_valid_c, s_c * scale + bwin_ref[:, 0:wbuf], NEG)
        s_n = jnp.where(new_valid, s_n * scale + bwin_ref[:, wbuf:wbuf + new_w], NEG)
        mx = jnp.maximum(jnp.max(s_c, axis=1, keepdims=True), jnp.max(s_n, axis=1, keepdims=True))
        e_c = jnp.exp(s_c - mx)
        e_n = jnp.exp(s_n - mx)
        den = jnp.sum(e_c, axis=1, keepdims=True) + jnp.sum(e_n, axis=1, keepdims=True)
        o_win = (all_values(e_c, lambda g: group_rows(vwc_ref, g))
                 + all_values(e_n, lambda g: group_rows(vwn_ref, g))) / den
        sig_gl = _sigmoid(gl_ref[0])
        out = jnp.zeros((N_KV * nr, HEAD_DIM), F32)
        for br, o_br in enumerate((ocmp_scr[...], o_sel, o_win)):
            z = z_ref[0, br]
            out = out + sig_gl[br] * o_br * (z * _sigmoid(z))
        o_ref[0] = out.astype(BF16)


def _attn_sample(q_s, z_s, gl_s, g_q, kcmp, vcmp, ksel_pages, vsel_pages, page_table, ks_new, vs_new,
                 kw_new, vw_new, kw_cache, vw_cache, bcmp, bsel, bwin, overlap, expand, n_tok):
    nb = q_s.shape[0]
    n_pages = page_table.shape[1]
    page = ksel_pages.shape[1] // N_KV
    ppc = min(SAMPLE_PAGES_PER_STEP, n_pages)
    n_chunks = n_pages // ppc
    nrow = q_s.shape[1]
    per_b = lambda shape: pl.BlockSpec((1,) + shape, lambda b, c, pt: (b,) + (0,) * len(shape))
    full = lambda shape: _resident(shape, lambda b, c, pt: (0,) * len(shape))
    page_spec = lambda i: pl.BlockSpec((1, page * N_KV, HEAD_DIM),
                                       lambda b, c, pt, i=i: (pt[b, c * ppc + i], 0, 0))
    in_specs = ([per_b(q_s.shape[1:]), per_b(z_s.shape[1:]), per_b(gl_s.shape[1:]), full((1, HEAD_DIM)),
                 per_b(kcmp.shape[1:]), per_b(vcmp.shape[1:])]
                + [page_spec(i) for i in range(ppc)] + [page_spec(i) for i in range(ppc)]
                + [per_b(ks_new.shape[1:])] * 4 + [per_b(kw_cache.shape[1:])] * 2
                + [full(bcmp.shape), full(bsel.shape), full(bwin.shape), full(overlap.shape), full(expand.shape)])
    grid_spec = pltpu.PrefetchScalarGridSpec(
        num_scalar_prefetch=1,
        grid=(nb, n_chunks),
        in_specs=in_specs,
        out_specs=pl.BlockSpec((1, nrow, HEAD_DIM), lambda b, c, pt: (b, 0, 0)),
        scratch_shapes=[pltpu.VMEM((nrow, expand.shape[0]), BF16),
                        pltpu.VMEM((nrow, LANES), F32), pltpu.VMEM((nrow, LANES), F32),
                        pltpu.VMEM((nrow, HEAD_DIM), F32), pltpu.VMEM((nrow, HEAD_DIM), BF16),
                        pltpu.VMEM((nrow, HEAD_DIM), F32)],
    )
    return pl.pallas_call(
        functools.partial(_attn_sample_kernel, n_tok=n_tok, pages_per_chunk=ppc, page=page,
                          past_len=n_pages * page),
        grid_spec=grid_spec,
        out_shape=jax.ShapeDtypeStruct((nb, nrow, HEAD_DIM), BF16),
        compiler_params=_cparams(("arbitrary", "arbitrary")),
        name="attn_sample",
    )(page_table, q_s, z_s, gl_s, g_q.reshape(1, HEAD_DIM), kcmp, vcmp,
      *([ksel_pages] * ppc), *([vsel_pages] * ppc), ks_new, vs_new, kw_new, vw_new, kw_cache, vw_cache,
      bcmp, bsel, bwin, overlap, expand)


def kernel(x_prompt, x_sample, p_prompt, p_sample, state_conv, state_h, cache_k_cmp, cache_v_cmp, cache_k_sel, cache_v_sel, cache_k_win, cache_v_win, page_table, norm_g, a_w_in, a_conv_w, a_conv_b, a_w_r, a_b_r, a_w_i, a_b_i, a_lam, a_w_out, kv_norm_g, w_kv, g_k_cmp, g_k_sel, g_k_win, cmp_pos, cmp_w1a, cmp_w1b, cmp_b1, cmp_w2, cmp_b2, b_w_in, b_g_q, b_w_out, rel_bias, ple_w_proj, ple_norm_g, ple_w_gate):
    bp, s_len, d = x_prompt.shape
    db, ds, _ = x_sample.shape
    n_pool, page = cache_k_sel.shape[:2]
    n_pages = page_table.shape[1]
    past_len = n_pages * page
    wbuf = cache_k_win.shape[1]
    hd = N_HEADS * HEAD_DIM
    assert bp == 1 and norm_g.shape[0] == 2 and a_w_in.shape[0] == 1 and b_w_in.shape[0] == 1
    assert d == hd and s_len % TK_SEL == 0 and s_len >= WINDOW + TQ and page % SEL_BLOCK == 0
    assert wbuf == WINDOW and ds <= SEL_BLOCK and past_len % TK_SEL == 0 and s_len // SEL_BLOCK <= HEAD_DIM

    w_in0 = a_w_in[0].astype(BF16)
    w_ri = jnp.concatenate([a_w_r[0], a_w_i[0]], axis=-1).astype(BF16)
    w_out0 = a_w_out[0].astype(BF16)
    w_gate = ple_w_gate.astype(BF16)
    w_proj = ple_w_proj.astype(BF16)
    w_kv_b = w_kv.astype(BF16)
    w_in1 = b_w_in[0].astype(BF16)
    w_gl = b_w_in[0][:, 4 * hd:].reshape(d, 3, N_KV, HEADS_PER_GROUP).transpose(0, 2, 1, 3)
    w_gl = jnp.pad(w_gl.reshape(d, N_KV, 3 * HEADS_PER_GROUP), ((0, 0), (0, 0), (0, LANES - 3 * HEADS_PER_GROUP)))
    w_gl = w_gl.reshape(d, N_KV * LANES).astype(BF16)
    w_out1 = b_w_out[0].astype(BF16)
    w1 = jnp.concatenate([cmp_w1a, cmp_w1b], axis=-1).astype(BF16)
    w2 = cmp_w2.astype(BF16)
    pe = cmp_pos.reshape(2, 2, 1, CMP_STRIDE * HEAD_DIM)
    pe_hi = pe.astype(BF16)
    pe_lo = (pe - pe_hi.astype(F32)).astype(BF16)
    pe_rows = jnp.concatenate([pe_hi, pe_lo, jnp.zeros((2, 2, SUBLANES - 2, pe.shape[-1]), BF16)], axis=2)
    pe_rows = pe_rows.reshape(2, 2 * SUBLANES, pe.shape[-1])
    b1 = cmp_b1.reshape(2, 1, -1)
    b2 = cmp_b2.reshape(2, 1, HEAD_DIM)

    xp = x_prompt.reshape(s_len, d)
    xs = x_sample.reshape(db * ds, d)
    uz_p = _norm_matmul(xp, norm_g[0], w_in0, 1024)
    uz_s = _norm_matmul(xs, norm_g[0], w_in0, 1024)
    rg = (a_conv_w[0], a_conv_b[0], w_ri, a_b_r[0], a_b_i[0], a_lam[0])
    gated_p, h_p = _rglru_prompt(uz_p, jnp.zeros((SUBLANES, d), F32), jnp.zeros((1, d), F32), *rg)
    conv_p = uz_p[s_len - (CONV_W - 1):, :d].reshape(1, bp, CONV_W - 1, d)
    u_s = uz_s[:, :d].reshape(db, ds, d).transpose(1, 0, 2)
    z_s = uz_s[:, d:].reshape(db, ds, d).transpose(1, 0, 2)
    up_s = jnp.concatenate([state_conv[0].transpose(1, 0, 2), u_s], axis=0)
    gated_s, h_s = _rglru_sample(up_s, z_s, state_h[0], *rg)
    conv_s = up_s[ds:].transpose(1, 0, 2)[None]
    gated_s = gated_s.transpose(1, 0, 2).reshape(db * ds, d)
    ple0 = (w_proj[0], ple_norm_g[0], w_gate[0])
    h1_p = _outproj_ple(gated_p, xp, p_prompt[0].reshape(s_len, -1), w_out0, *ple0)
    h1_s = _outproj_ple(gated_s, xs, p_sample[0].reshape(db * ds, -1), w_out0, *ple0)

    kv_p = _kv_proj(h1_p, kv_norm_g, w_kv_b, g_k_sel, g_k_win)
    kv_s = _kv_proj(h1_s, kv_norm_g, w_kv_b, g_k_sel, g_k_win)
    kc_p, vc_p, ks_p, vs_p, kw_p, vw_p = kv_p[:6]
    kc_s, vc_s, ks_s, vs_s, kw_s, vw_s = kv_s[:6]

    p_pages = s_len // page
    ident = jnp.minimum(jnp.arange(p_pages + 1, dtype=jnp.int32), p_pages - 1)[None]
    cmp_w = (pe_rows, w1, b1, w2, b2, g_k_cmp)
    pieces = lambda x: x.reshape(-1, page // CMP_STRIDE, PIECE_ROWS, HEAD_DIM)
    kcmp_p, vcmp_p = _compress(pieces(kc_p), pieces(vc_p), ident, *cmp_w)
    table_ext = jnp.concatenate([page_table, page_table[:, -1:]], axis=1)
    kcmp_s, vcmp_s = _compress(pieces(cache_k_cmp), pieces(cache_v_cmp), table_ext, *cmp_w)

    proj_p, gl_p = _norm_matmul(h1_p, norm_g[1], w_in1, 1024, w_gl, n=4 * hd)
    proj_s, gl_s = _norm_matmul(h1_s, norm_g[1], w_in1, 1024, w_gl, n=4 * hd)

    n_qt = s_len // TQ
    n_cmp_rows = s_len // CMP_STRIDE
    bank_w = max(TK_SEL, WINDOW + TQ)
    bank_k = min(-(-(_first_constant_distance() + bank_w) // LANES) * LANES, s_len)
    gmax = lambda g: jnp.max(jnp.abs(g))
    k_gain = jnp.maximum(jnp.maximum(gmax(g_k_sel), gmax(g_k_win)), gmax(g_k_cmp))
    score_bound = (1.02 * HEAD_DIM * HEAD_DIM ** -0.5 * gmax(b_g_q[0]) * k_gain + gmax(rel_bias)) * LOG2E
    bounded = score_bound <= MAX_SCORE_BOUND
    shift = jnp.where(bounded, score_bound, 0.0)
    sb, wb, rr = _bias_banks(rel_bias, n_qt, n_cmp_rows // SUBLANES, bank_k, bank_w, shift)
    o_p = _attn_prompt(proj_p, gl_p, b_g_q[0], kcmp_p, vcmp_p, kv_p[6], kv_p[7], kv_p[8], kv_p[9],
                       rr, sb, wb, bounded.astype(jnp.int32).reshape(1), bank_k)

    def head_rows(x):
        return x.reshape(db, ds, N_HEADS, HEAD_DIM).transpose(0, 2, 1, 3).reshape(db, N_HEADS * ds, HEAD_DIM)

    q_s = head_rows(proj_s[:, :hd])
    zz_s = jnp.stack([head_rows(proj_s[:, (1 + br) * hd:(2 + br) * hd]) for br in range(3)], axis=1)
    gl_s = gl_s.reshape(db, ds, N_KV, LANES)[..., :3 * HEADS_PER_GROUP].reshape(db, ds, N_KV, 3, HEADS_PER_GROUP)
    gl_s = gl_s.transpose(0, 3, 2, 4, 1).reshape(db, 3, N_HEADS * ds, 1)
    new_w = LANES
    group_rows = lambda x: x.reshape(db, -1, HEAD_DIM)
    pad_new = lambda x: jnp.pad(group_rows(x), ((0, 0), (0, (new_w - ds) * N_KV), (0, 0)))
    n_cmp_s = past_len // CMP_STRIDE
    bsel = _query_rows_bias(rel_bias, past_len, ds, past_len + new_w).reshape(N_HEADS * ds, -1)
    bwin = bsel[:, past_len - wbuf:]
    bcmp = bsel[:, CMP_BLOCK - 1::CMP_STRIDE][:, :n_cmp_s]
    n_sb_s = past_len // SEL_BLOCK
    ci = jnp.arange(n_cmp_s)[:, None] * CMP_STRIDE
    sj = jnp.arange(n_sb_s)[None, :] * SEL_BLOCK
    overlap = ((ci < sj + SEL_BLOCK) & (ci + CMP_BLOCK > sj) & (ci < past_len - CMP_STRIDE)).T.astype(BF16)
    expand_s = (jnp.arange(past_len)[None, :] // SEL_BLOCK == jnp.arange(n_sb_s)[:, None]).astype(BF16)
    o_s = _attn_sample(q_s, zz_s, gl_s, b_g_q[0], kcmp_s, vcmp_s,
                       cache_k_sel.reshape(n_pool, -1, HEAD_DIM), cache_v_sel.reshape(n_pool, -1, HEAD_DIM),
                       page_table, pad_new(ks_s), pad_new(vs_s), pad_new(kw_s), pad_new(vw_s),
                       group_rows(cache_k_win), group_rows(cache_v_win),
                       bcmp, bsel, bwin, overlap, expand_s, ds)
    o_s = o_s.reshape(db, N_HEADS, ds, HEAD_DIM).transpose(0, 2, 1, 3).reshape(db * ds, hd)

    ple1 = (w_proj[1], ple_norm_g[1], w_gate[1])
    y_p = _outproj_ple(o_p, h1_p, p_prompt[1].reshape(s_len, -1), w_out1, *ple1)
    y_s = _outproj_ple(o_s, h1_s, p_sample[1].reshape(db * ds, -1), w_out1, *ple1)

    rows4 = lambda x, b, t: x.reshape(b, t, N_KV, HEAD_DIM)
    wk_p = min(WINDOW, s_len)
    kwin_s = jnp.concatenate([cache_k_win, rows4(kw_s, db, ds)], axis=1)[:, -WINDOW:]
    vwin_s = jnp.concatenate([cache_v_win, rows4(vw_s, db, ds)], axis=1)[:, -WINDOW:]
    return (y_p.reshape(bp, s_len, d), y_s.reshape(db, ds, d),
            conv_p, h_p.reshape(1, bp, d),
            rows4(kc_p, bp, s_len), rows4(vc_p, bp, s_len), rows4(ks_p, bp, s_len), rows4(vs_p, bp, s_len),
            rows4(kw_p, bp, s_len)[:, -wk_p:], rows4(vw_p, bp, s_len)[:, -wk_p:],
            conv_s, h_s.reshape(1, db, d),
            rows4(kc_s, db, ds), rows4(vc_s, db, ds), rows4(ks_s, db, ds), rows4(vs_s, db, ds),
            kwin_s, vwin_s)
```
